```python
import math
import jax, jax.numpy as jnp
from jax import lax
import numpy as np

D_MODEL = 1024
BATCH = 8
SEQ = 2048
DEPTH = 4
DEC_BATCH = 128
DEC_SEQ = 4
PAST_LEN = 2048
PAGE_SIZE = 128

N_MIXERS = 3
N_LAYERS_A = (DEPTH + 2) // 3
N_LAYERS_B = (DEPTH + 1) // 3
N_LAYERS_C = DEPTH // 3

POOL_WINDOWS = (2, 4, 8, 16)
POOL_GROUPS = len(POOL_WINDOWS)
POOL_GROUP_DIM = D_MODEL // POOL_GROUPS
POOL_STATE_ROWS = max(POOL_WINDOWS) - 1

DILATED_GROUPS = ((128, 1), (512, 4), (2048, 16))
N_DIL_GROUPS = len(DILATED_GROUPS)
HEADS_PER_GROUP = 4
HEAD_DIM = 64
N_HEADS_B = N_DIL_GROUPS * HEADS_PER_GROUP
ATTN_INNER = N_HEADS_B * HEAD_DIM
QUERY_BLOCK = 128
NUM_BUCKETS = 32
MAX_DISTANCE = 2048

GLA_HEADS = 4
GLA_DK = D_MODEL // (2 * GLA_HEADS)
GLA_DV = D_MODEL // GLA_HEADS
GLA_QK_WIDTH = GLA_HEADS * GLA_DK
GLA_V_WIDTH = GLA_HEADS * GLA_DV
GATE_RANK = 16
GATE_TAU = 16.0
GLA_CHUNK = 64
GLA_IN_WIDTH = 2 * GLA_QK_WIDTH + 2 * GLA_V_WIDTH + GATE_RANK

D_FF = 128 * ((8 * D_MODEL // 3 + 127) // 128)
CONV_WIDTH = 3

N_MOD = 6
EPS = 1e-6
NEG_INF = -1e30

kernel_name = 'hybrid_pool_dilated_gla_step'


def rms_norm(x, gain):
    xf = x.astype(jnp.float32)
    y = xf * lax.rsqrt(jnp.mean(xf * xf, axis=-1, keepdims=True) + EPS)
    return (y * gain.astype(jnp.float32)).astype(x.dtype)


def t5_bucket(dist):
    max_exact = NUM_BUCKETS // 2
    d_f = jnp.maximum(dist, 1).astype(jnp.float32)
    large = max_exact + (jnp.log(d_f / max_exact) / math.log(MAX_DISTANCE / max_exact)
                         * (NUM_BUCKETS - max_exact)).astype(jnp.int32)
    large = jnp.minimum(large, NUM_BUCKETS - 1)
    return jnp.where(dist < max_exact, dist, large)


def pool_mixer(u, past, pos0, w_group, layer_scale):
    B, T, _ = u.shape
    P = past.shape[1]
    u_all = jnp.concatenate([past.astype(u.dtype), u], axis=1)
    uf = u_all.astype(jnp.float32)
    cs = jnp.concatenate([jnp.zeros((B, 1, D_MODEL), jnp.float32), jnp.cumsum(uf, axis=1)], axis=1)
    rows = P + jnp.arange(T)
    pos = pos0 + jnp.arange(T)
    hi = cs[:, P + 1:]
    means = []
    for g, w in enumerate(POOL_WINDOWS):
        sl = slice(g * POOL_GROUP_DIM, (g + 1) * POOL_GROUP_DIM)
        lo = jnp.maximum(rows + 1 - w, 0)
        count = jnp.minimum(pos + 1, w).astype(jnp.float32)
        means.append((hi[:, :, sl] - cs[:, lo, sl]) / count[None, :, None])
    d = jnp.concatenate(means, axis=-1) - uf[:, P:]
    y = jnp.einsum('btgc,gce->btge', d.reshape(B, T, POOL_GROUPS, POOL_GROUP_DIM),
                   w_group.astype(jnp.float32))
    y = y.reshape(B, T, D_MODEL) * layer_scale.astype(jnp.float32)
    keep = min(POOL_STATE_ROWS, P + T)
    return y.astype(u.dtype), u_all[:, P + T - keep:]


def dilated_group_attention(q, k_all, v_all, p_past, dilation, window, bias):
    B, Tq, H, Dh = q.shape
    nk = window // dilation + 1
    offs = jnp.arange(nk) * dilation
    qb = math.gcd(Tq, QUERY_BLOCK)
    nb = Tq // qb
    q_blocks = q.reshape(B, nb, qb, H, Dh).swapaxes(0, 1)
    bias_f = bias.astype(jnp.float32)[None, :, None, :]

    def one_block(args):
        blk, qblk = args
        rows = p_past + blk * qb + jnp.arange(qb)
        idx = rows[:, None] - offs[None, :]
        valid = idx >= 0
        idx_c = jnp.maximum(idx, 0)
        kg = k_all[:, idx_c]
        vg = v_all[:, idx_c]
        logits = jnp.einsum('bqhd,bqkhd->bhqk', qblk, kg, preferred_element_type=jnp.float32)
        logits = logits * (HEAD_DIM ** -0.5) + bias_f
        logits = jnp.where(valid[None, None], logits, NEG_INF)
        lse = jax.nn.logsumexp(logits, axis=-1)
        p = jnp.exp(logits - lse[..., None])
        out = jnp.einsum('bhqk,bqkhd->bqhd', p.astype(vg.dtype), vg)
        return out, lse.transpose(0, 2, 1)

    outs, lses = lax.map(one_block, (jnp.arange(nb), q_blocks))
    out = outs.swapaxes(0, 1).reshape(B, Tq, H, Dh)
    lse = lses.swapaxes(0, 1).reshape(B, Tq, H)
    return out, lse


def dilated_attention_mixer(h, pasts, w_in, w_out, rel_bias):
    B, T, _ = h.shape
    qkv = (h @ w_in).reshape(B, T, 3, N_DIL_GROUPS, HEADS_PER_GROUP, HEAD_DIM)
    outs, lses, new_bufs = [], [], []
    for g, (window, dil) in enumerate(DILATED_GROUPS):
        past = pasts[g].astype(h.dtype)
        P = past.shape[1]
        kv_all = jnp.concatenate([past, qkv[:, :, 1:, g].astype(h.dtype)], axis=1)
        buckets = t5_bucket(jnp.arange(window // dil + 1) * dil)
        bias = rel_bias[buckets][:, g * HEADS_PER_GROUP:(g + 1) * HEADS_PER_GROUP].T
        o, lse = dilated_group_attention(qkv[:, :, 0, g], kv_all[:, :, 0], kv_all[:, :, 1], P, dil, window, bias)
        outs.append(o)
        lses.append(lse)
        keep = min(window, P + T)
        new_bufs.append(kv_all[:, P + T - keep:])
    alpha = jax.nn.softmax(jnp.stack(lses, axis=2), axis=2)
    o_all = jnp.stack(outs, axis=2) * alpha[..., None].astype(h.dtype)
    return o_all.reshape(B, T, ATTN_INNER) @ w_out, new_bufs


def gla_mixer(h, S0, w_in, w_gate_up, b_gate, norm_gain, w_out):
    B, T, _ = h.shape
    proj = h @ w_in
    q, k, v, r, gd = jnp.split(proj, [GLA_QK_WIDTH, 2 * GLA_QK_WIDTH, 2 * GLA_QK_WIDTH + GLA_V_WIDTH,
                                      2 * GLA_QK_WIDTH + 2 * GLA_V_WIDTH], axis=-1)
    q = q.reshape(B, T, GLA_HEADS, GLA_DK).astype(jnp.float32) * (GLA_DK ** -0.5)
    k = k.reshape(B, T, GLA_HEADS, GLA_DK).astype(jnp.float32)
    v = v.reshape(B, T, GLA_HEADS, GLA_DV).astype(jnp.float32)
    log_a = jax.nn.log_sigmoid((gd @ w_gate_up + b_gate).astype(jnp.float32)) / GATE_TAU
    log_a = log_a.reshape(B, T, GLA_HEADS, GLA_DK)
    C = math.gcd(T, GLA_CHUNK)
    nc = T // C

    def to_chunks(a):
        return a.reshape(B, nc, C, *a.shape[2:]).swapaxes(0, 1)

    causal = jnp.tril(jnp.ones((C, C), dtype=bool))

    def step(S, inp):
        qc, kc, vc, gc = inp
        b = jnp.cumsum(gc, axis=1)
        o_inter = jnp.einsum('bthk,bhkv->bthv', qc * jnp.exp(b), S)
        diff = b[:, :, None] - b[:, None, :]
        decay = jnp.exp(jnp.where(causal[None, :, :, None, None], diff, -jnp.inf))
        att = jnp.einsum('bthk,bshk,btshk->bhts', qc, kc, decay)
        o_intra = jnp.einsum('bhts,bshv->bthv', att, vc)
        b_end = b[:, -1]
        S_new = jnp.exp(b_end)[..., None] * S + jnp.einsum('bshk,bshv->bhkv', kc * jnp.exp(b_end[:, None] - b), vc)
        return S_new, o_inter + o_intra

    S_fin, o = lax.scan(step, S0.astype(jnp.float32), (to_chunks(q), to_chunks(k), to_chunks(v), to_chunks(log_a)))
    o = o.swapaxes(0, 1).reshape(B, T, GLA_HEADS, GLA_DV)
    o = o * lax.rsqrt(jnp.mean(o * o, axis=-1, keepdims=True) + EPS)
    o = o.reshape(B, T, GLA_V_WIDTH) * norm_gain.astype(jnp.float32)
    out = (o.astype(h.dtype) * jax.nn.silu(r)) @ w_out
    return out, S_fin


def conv_ffn(h, past, w_in, conv_w, conv_b, w_down):
    T = h.shape[1]
    g, u = jnp.split(h @ w_in, 2, axis=-1)
    g_all = jnp.concatenate([past.astype(g.dtype), g], axis=1)
    gc = conv_w[0] * g_all[:, 0:T] + conv_w[1] * g_all[:, 1:T + 1] + conv_w[2] * g_all[:, 2:T + 2] + conv_b
    y = (jax.nn.gelu(gc, approximate=False) * u) @ w_down
    return y, g_all[:, -(CONV_WIDTH - 1):]


def trunk(x, c, pos0, pool_past, win_past1, win_past2, win_past3, gla_past, conv_past,
          w_ada, b_ada, norm_gain, final_gain, rel_bias, pool_w, pool_scale,
          attn_w_in, attn_w_out, gla_w_in, gla_w_gate_up, gla_b_gate, gla_norm_gain, gla_w_out,
          ffn_w_in, ffn_conv_w, ffn_conv_b, ffn_w_down):
    B = x.shape[0]
    c_act = jax.nn.silu(c)
    new_pool, new_w1, new_w2, new_w3, new_gla, new_conv = [], [], [], [], [], []
    for i in range(DEPTH):
        kind, j = i % N_MIXERS, i // N_MIXERS
        mod = (c_act @ w_ada[i] + b_ada[i]).reshape(B, N_MOD, 1, D_MODEL)
        shift1, scale1, gate1, shift2, scale2, gate2 = (mod[:, m] for m in range(N_MOD))
        h = rms_norm(x, norm_gain[i, 0]) * (1 + scale1) + shift1
        if kind == 0:
            mix, st = pool_mixer(h, pool_past[j], pos0, pool_w[j], pool_scale[j])
            new_pool.append(st)
        elif kind == 1:
            mix, bufs = dilated_attention_mixer(h, (win_past1[j], win_past2[j], win_past3[j]),
                                                attn_w_in[j], attn_w_out[j], rel_bias)
            new_w1.append(bufs[0])
            new_w2.append(bufs[1])
            new_w3.append(bufs[2])
        else:
            mix, st = gla_mixer(h, gla_past[j], gla_w_in[j], gla_w_gate_up[j], gla_b_gate[j],
                                gla_norm_gain[j], gla_w_out[j])
            new_gla.append(st)
        x = x + gate1 * mix
        h = rms_norm(x, norm_gain[i, 1]) * (1 + scale2) + shift2
        f, st = conv_ffn(h, conv_past[i], ffn_w_in[i], ffn_conv_w[i], ffn_conv_b[i], ffn_w_down[i])
        new_conv.append(st)
        x = x + gate2 * f
    y = rms_norm(x, final_gain)
    return (y, jnp.stack(new_pool), jnp.stack(new_w1), jnp.stack(new_w2), jnp.stack(new_w3),
            jnp.stack(new_gla), jnp.stack(new_conv))


def setup_inputs(seed: int = 0) -> dict:
    key = jax.random.key(seed)
    ks = jax.random.split(key, 32)

    def nrm(k, shape, s):
        return jax.random.normal(k, shape, jnp.float32) * s

    win_rows = [min(w, PAST_LEN) for (w, _) in DILATED_GROUPS]
    return {
        'x_prompt': nrm(ks[0], (BATCH, SEQ, D_MODEL), 1.0),
        'x_sample': nrm(ks[1], (DEC_BATCH, DEC_SEQ, D_MODEL), 1.0),
        'state_pool': nrm(ks[2], (N_LAYERS_A, DEC_BATCH, POOL_STATE_ROWS, D_MODEL), 1.0),
        'cache_win_g1': nrm(ks[3], (N_LAYERS_B, DEC_BATCH, win_rows[0], 2, HEADS_PER_GROUP, HEAD_DIM), 1.0),
        'cache_win_g2': nrm(ks[4], (N_LAYERS_B, DEC_BATCH, win_rows[1], 2, HEADS_PER_GROUP, HEAD_DIM), 1.0),
        'cache_win_g3': nrm(ks[5], (N_LAYERS_B, DEC_BATCH, win_rows[2], 2, HEADS_PER_GROUP, HEAD_DIM), 1.0),
        'state_gla': nrm(ks[6], (N_LAYERS_C, DEC_BATCH, GLA_HEADS, GLA_DK, GLA_DV), 1.0),
        'state_ffn_conv': nrm(ks[7], (DEPTH, DEC_BATCH, CONV_WIDTH - 1, D_FF), 1.0),
        'c_prompt': nrm(ks[8], (BATCH, D_MODEL), 1.0),
        'c_sample': nrm(ks[9], (DEC_BATCH, D_MODEL), 1.0),
        'w_ada': nrm(ks[10], (DEPTH, D_MODEL, N_MOD * D_MODEL), 0.3 * D_MODEL ** -0.5),
        'b_ada': nrm(ks[11], (DEPTH, N_MOD * D_MODEL), 0.02),
        'norm_gain': 1.0 + nrm(ks[12], (DEPTH, 2, D_MODEL), 0.05),
        'final_gain': 1.0 + nrm(ks[13], (D_MODEL,), 0.05),
        'rel_bias': nrm(ks[14], (NUM_BUCKETS, N_HEADS_B), 0.5),
        'pool_w': nrm(ks[15], (N_LAYERS_A, POOL_GROUPS, POOL_GROUP_DIM, POOL_GROUP_DIM), POOL_GROUP_DIM ** -0.5),
        'pool_scale': 1.0 + nrm(ks[16], (N_LAYERS_A, D_MODEL), 0.05),
        'attn_w_in': nrm(ks[17], (N_LAYERS_B, D_MODEL, 3 * ATTN_INNER), D_MODEL ** -0.5),
        'attn_w_out': nrm(ks[18], (N_LAYERS_B, ATTN_INNER, D_MODEL), ATTN_INNER ** -0.5),
        'gla_w_in': nrm(ks[19], (N_LAYERS_C, D_MODEL, GLA_IN_WIDTH), D_MODEL ** -0.5),
        'gla_w_gate_up': nrm(ks[20], (N_LAYERS_C, GATE_RANK, GLA_QK_WIDTH), GATE_RANK ** -0.5),
        'gla_b_gate': nrm(ks[21], (N_LAYERS_C, GLA_QK_WIDTH), 0.1),
        'gla_norm_gain': 1.0 + nrm(ks[22], (N_LAYERS_C, GLA_V_WIDTH), 0.05),
        'gla_w_out': nrm(ks[23], (N_LAYERS_C, GLA_V_WIDTH, D_MODEL), GLA_V_WIDTH ** -0.5),
        'ffn_w_in': nrm(ks[24], (DEPTH, D_MODEL, 2 * D_FF), D_MODEL ** -0.5),
        'ffn_conv_w': nrm(ks[25], (DEPTH, CONV_WIDTH, D_FF), CONV_WIDTH ** -0.5),
        'ffn_conv_b': nrm(ks[26], (DEPTH, D_FF), 0.02),
        'ffn_w_down': nrm(ks[27], (DEPTH, D_FF, D_MODEL), D_FF ** -0.5),
    }


def reference(x_prompt, x_sample, state_pool, cache_win_g1, cache_win_g2, cache_win_g3, state_gla, state_ffn_conv,
              c_prompt, c_sample, w_ada, b_ada, norm_gain, final_gain, rel_bias, pool_w, pool_scale,
              attn_w_in, attn_w_out, gla_w_in, gla_w_gate_up, gla_b_gate, gla_norm_gain, gla_w_out,
              ffn_w_in, ffn_conv_w, ffn_conv_b, ffn_w_down):
    weights = (w_ada, b_ada, norm_gain, final_gain, rel_bias, pool_w, pool_scale,
               attn_w_in, attn_w_out, gla_w_in, gla_w_gate_up, gla_b_gate, gla_norm_gain, gla_w_out,
               ffn_w_in, ffn_conv_w, ffn_conv_b, ffn_w_down)
    dt = x_prompt.dtype
    empty_pool = jnp.zeros((N_LAYERS_A, BATCH, 0, D_MODEL), dt)
    empty_win = jnp.zeros((N_LAYERS_B, BATCH, 0, 2, HEADS_PER_GROUP, HEAD_DIM), dt)
    zero_gla = jnp.zeros((N_LAYERS_C, BATCH, GLA_HEADS, GLA_DK, GLA_DV), jnp.float32)
    zero_conv = jnp.zeros((DEPTH, BATCH, CONV_WIDTH - 1, D_FF), dt)
    (y_prompt, pool_p, win1_p, win2_p, win3_p, gla_p, conv_p) = trunk(
        x_prompt, c_prompt, 0, empty_pool, empty_win, empty_win, empty_win, zero_gla, zero_conv, *weights)
    (y_sample, pool_s, win1_s, win2_s, win3_s, gla_s, conv_s) = trunk(
        x_sample, c_sample, PAST_LEN, state_pool, cache_win_g1, cache_win_g2, cache_win_g3, state_gla,
        state_ffn_conv, *weights)
    return (y_prompt, y_sample, pool_p, pool_s, win1_p, win1_s, win2_p, win2_s, win3_p, win3_s,
            gla_p, gla_s, conv_p, conv_s)
```

```python
import functools
import math

import numpy as np
import jax
import jax.numpy as jnp
from jax import lax
from jax.experimental import pallas as pl
from jax.experimental.pallas import tpu as pltpu

F32 = jnp.float32
BF16 = jnp.bfloat16

D_MODEL = 1024
DEPTH = 4
N_MOD = 6
EPS = 1e-6
NEG_INF = -1e30
POOL_WINDOWS = (2, 4, 8, 16)
POOL_GROUP_DIM = D_MODEL // len(POOL_WINDOWS)
POOL_STATE_ROWS = max(POOL_WINDOWS) - 1
POOL_CARRY_ROWS = 16
DILATED_GROUPS = ((128, 1), (512, 4), (2048, 16))
N_GROUPS = len(DILATED_GROUPS)
HEADS_PER_GROUP = 4
HEAD_DIM = 64
GROUP_WIDTH = HEADS_PER_GROUP * HEAD_DIM
ATTN_INNER = N_GROUPS * GROUP_WIDTH
KEYS_PER_QUERY = 129
QUERY_BLOCK = 128
NUM_BUCKETS = 32
MAX_DISTANCE = 2048
GLA_HEADS = 4
GLA_DK = 128
GLA_DV = 256
GLA_QK = GLA_HEADS * GLA_DK
GLA_V = GLA_HEADS * GLA_DV
GATE_RANK = 16
GATE_TAU = 16.0
GLA_CHUNK = 64
D_FF = 2816
CONV_WIDTH = 3

LANES = 128
SUBLANES = 8
FF_CHUNK = 256
N_FF_CHUNKS = D_FF // FF_CHUNK
VMEM_LIMIT_BYTES = 56 * 1024 * 1024
PROMPT_ROW_TILE = 512
GLA_TIME_TILE = 512
SAMPLE_DEC_PAD = 16


def _params(*semantics):
    return pltpu.CompilerParams(dimension_semantics=semantics, vmem_limit_bytes=VMEM_LIMIT_BYTES)


def _resident(shape):
    nd = len(shape)
    return pl.BlockSpec(shape, lambda *_: (0,) * nd, pipeline_mode=pl.Buffered(1))


def _dot(a, b):
    return jnp.dot(a, b, preferred_element_type=F32)


def _dot_nt(a, b):
    return lax.dot_general(a, b, (((1,), (1,)), ((), ())), preferred_element_type=F32)


def _dot_tn(a, b):
    return lax.dot_general(a, b, (((0,), (0,)), ((), ())), preferred_element_type=F32)


def _rms(x):
    return x * lax.rsqrt(jnp.mean(x * x, axis=-1, keepdims=True) + EPS)


def _bcast_rows(v, y, nb):
    if nb is None:
        return v * y
    rows, width = y.shape
    return (y.reshape(rows // nb, nb, width) * v[None]).reshape(rows, width)


def _norm_mod(x, gain, shift, scale, nb):
    y = _rms(x) * gain
    if nb is None:
        return y * (1.0 + scale) + shift
    rows, width = y.shape
    y3 = y.reshape(rows // nb, nb, width)
    return (y3 * (1.0 + scale)[None] + shift[None]).reshape(rows, width)


def _gelu(x):
    return 0.5 * x * (1.0 + lax.erf(x * (1.0 / math.sqrt(2.0))))


def _silu(x):
    return x * jax.nn.sigmoid(x)


def _split_bf16(a):
    hi = a.astype(BF16)
    lo = (a - hi.astype(F32)).astype(BF16)
    return hi, lo


def _mod_kernel(c_ref, w_ref, b_ref, o_ref):
    a_hi, a_lo = _split_bf16(_silu(c_ref[...]))
    w_hi, w_lo = _split_bf16(w_ref[0])
    o_ref[0] = _dot(a_hi, w_hi) + _dot(a_lo, w_hi) + _dot(a_hi, w_lo) + b_ref[0]


def _modulation(c_all, w_ada, b_ada):
    rows = c_all.shape[0]
    width = N_MOD * D_MODEL
    tn = 1536
    return pl.pallas_call(
        _mod_kernel,
        grid=(DEPTH, width // tn),
        in_specs=[pl.BlockSpec((rows, D_MODEL), lambda l, n: (0, 0)),
                  pl.BlockSpec((1, D_MODEL, tn), lambda l, n: (l, 0, n)),
                  pl.BlockSpec((1, 1, tn), lambda l, n: (l, 0, n))],
        out_specs=pl.BlockSpec((1, rows, tn), lambda l, n: (l, 0, n)),
        out_shape=jax.ShapeDtypeStruct((DEPTH, rows, width), F32),
        compiler_params=_params("parallel", "parallel"),
        name="adaln_mod",
    )(c_all, w_ada, b_ada.reshape(DEPTH, 1, width))


def _ffn_chunk_math(g, g_m1, g_m2, u, cw, cb):
    gc = cw[2:3] * g + cw[1:2] * g_m1 + cw[0:1] * g_m2 + cb
    return (_gelu(gc) * u).astype(BF16)


def _layer_prompt_kernel(mixer, last, tm, tiles_per_seq, *refs):
    refs = list(refs)
    x_ref, mod_ref, gains_ref, fg_ref = refs[:4]
    refs = refs[4:]
    if mixer == "pool":
        pw_ref, ps_ref = refs[:2]
    else:
        a_ref, wp_ref = refs[:2]
    wg_ref, wu_ref, wd_ref, cw_ref, cb_ref = refs[2:7]
    refs = refs[7:]
    if mixer == "pool":
        y_ref, cs_ref, pst_ref, h2_ref, acc_ref, gext_ref, cc_ref, hext_ref = refs
    else:
        y_ref, cs_ref, h2_ref, acc_ref, gext_ref, cc_ref = refs

    i = pl.program_id(0)
    tile_in_seq = i % tiles_per_seq

    @pl.when(tile_in_seq == 0)
    def _():
        cc_ref[...] = jnp.zeros_like(cc_ref)
        if mixer == "pool":
            hext_ref[0:POOL_CARRY_ROWS, :] = jnp.zeros((POOL_CARRY_ROWS, D_MODEL), F32)

    x = x_ref[...]
    m = mod_ref[0]
    gains = gains_ref[...]

    if mixer == "pool":
        h = _norm_mod(x, gains[0:1], m[0:1], m[1:2], None)
        hext_ref[POOL_CARRY_ROWS:, :] = h
        pos = tile_in_seq * tm + lax.broadcasted_iota(jnp.int32, (tm, 1), 0)
        parts = []
        for g, w in enumerate(POOL_WINDOWS):
            cols = slice(g * POOL_GROUP_DIM, (g + 1) * POOL_GROUP_DIM)
            s = hext_ref[pl.ds(POOL_CARRY_ROWS, tm), cols]
            for k in range(1, w):
                s = s + hext_ref[pl.ds(POOL_CARRY_ROWS - k, tm), cols]
            inv_count = 1.0 / jnp.minimum(pos + 1, w).astype(F32)
            d = s * inv_count - h[:, cols]
            parts.append(_dot(d.astype(BF16), pw_ref[g]))
        mix = jnp.concatenate(parts, axis=-1) * ps_ref[...]
        tail = hext_ref[pl.ds(tm, POOL_CARRY_ROWS), :]
        hext_ref[0:POOL_CARRY_ROWS, :] = tail
        pst_ref[0] = tail
    else:
        mix = _dot(a_ref[...], wp_ref[...])

    x1 = x + m[2:3] * mix
    y_ref[...] = x1
    h2_ref[...] = _norm_mod(x1, gains[1:2], m[3:4], m[4:5], None).astype(BF16)
    acc_ref[...] = jnp.zeros_like(acc_ref)

    def chunk(j, carry):
        h2 = h2_ref[...]
        g = _dot(h2, wg_ref[j])
        u = _dot(h2, wu_ref[j])
        gext_ref[0:SUBLANES, :] = cc_ref[j]
        gext_ref[SUBLANES:, :] = g
        a = _ffn_chunk_math(g, gext_ref[pl.ds(SUBLANES - 1, tm), :], gext_ref[pl.ds(SUBLANES - 2, tm), :],
                            u, cw_ref[j], cb_ref[j])
        acc_ref[...] += _dot(a, wd_ref[j])
        tail = g[tm - SUBLANES:tm, :]
        cc_ref[j] = tail
        cs_ref[0, j] = tail
        return carry

    lax.fori_loop(0, N_FF_CHUNKS, chunk, 0)
    xo = y_ref[...] + m[5:6] * acc_ref[...]
    if last:
        xo = _rms(xo) * fg_ref[...]
    y_ref[...] = xo


def _layer_prompt(mixer, last, x, mod, gains, fgain, mix_args, ffn):
    n = x.shape[0]
    batch = mod.shape[0]
    seq = n // batch
    tm = PROMPT_ROW_TILE
    tps = seq // tm
    in_specs = [pl.BlockSpec((tm, D_MODEL), lambda i: (i, 0)),
                pl.BlockSpec((1, N_MOD, D_MODEL), lambda i: (i // tps, 0, 0)),
                _resident((2, D_MODEL)), _resident((1, D_MODEL))]
    if mixer == "pool":
        pw, ps = mix_args
        in_specs += [_resident(pw.shape), _resident(ps.shape)]
    else:
        a, wp = mix_args
        in_specs += [pl.BlockSpec((tm, a.shape[1]), lambda i: (i, 0)), _resident(wp.shape)]
    in_specs += [_resident(w.shape) for w in ffn]
    out_shape = [jax.ShapeDtypeStruct((n, D_MODEL), F32),
                 jax.ShapeDtypeStruct((batch, N_FF_CHUNKS, SUBLANES, FF_CHUNK), F32)]
    out_specs = [pl.BlockSpec((tm, D_MODEL), lambda i: (i, 0)),
                 pl.BlockSpec((1, N_FF_CHUNKS, SUBLANES, FF_CHUNK), lambda i: (i // tps, 0, 0, 0))]
    scratch = [pltpu.VMEM((tm, D_MODEL), BF16), pltpu.VMEM((tm, D_MODEL), F32),
               pltpu.VMEM((tm + SUBLANES, FF_CHUNK), F32), pltpu.VMEM((N_FF_CHUNKS, SUBLANES, FF_CHUNK), F32)]
    if mixer == "pool":
        out_shape.append(jax.ShapeDtypeStruct((batch, POOL_CARRY_ROWS, D_MODEL), F32))
        out_specs.append(pl.BlockSpec((1, POOL_CARRY_ROWS, D_MODEL), lambda i: (i // tps, 0, 0)))
        scratch.append(pltpu.VMEM((tm + POOL_CARRY_ROWS, D_MODEL), F32))
    return pl.pallas_call(
        functools.partial(_layer_prompt_kernel, mixer, last, tm, tps),
        grid=(n // tm,), in_specs=in_specs, out_specs=out_specs, out_shape=out_shape,
        scratch_shapes=scratch, compiler_params=_params("arbitrary"),
        name=f"layer_prompt_{mixer}",
    )(x, mod, gains, fgain, *mix_args, *ffn)


def _layer_sample_kernel(mixer, last, nb, steps, *refs):
    refs = list(refs)
    x_ref, mod_ref, gains_ref, fg_ref = refs[:4]
    refs = refs[4:]
    if mixer == "pool":
        pw_ref, ps_ref, ppast_ref = refs[:3]
        refs = refs[3:]
    else:
        a_ref, wp_ref = refs[:2]
        refs = refs[2:]
    wg_ref, wu_ref, wd_ref, cw_ref, cb_ref, cpast_ref = refs[:6]
    refs = refs[6:]
    if mixer == "pool":
        y_ref, cs_ref, pst_ref, h2_ref, acc_ref = refs
    else:
        y_ref, cs_ref, h2_ref, acc_ref = refs
    rows = steps * nb

    x = x_ref[...]
    gains = gains_ref[...]
    if mixer == "pool":
        h = _norm_mod(x, gains[0:1], mod_ref[0], mod_ref[1], nb)
        new = [h[t * nb:(t + 1) * nb, :] for t in range(steps)]

        def u_rows(p, cols):
            if p < POOL_STATE_ROWS:
                return ppast_ref[p, :, cols]
            return new[p - POOL_STATE_ROWS][:, cols]

        parts = []
        for g, w in enumerate(POOL_WINDOWS):
            cols = slice(g * POOL_GROUP_DIM, (g + 1) * POOL_GROUP_DIM)
            ds = []
            for t in range(steps):
                s = u_rows(POOL_STATE_ROWS + t, cols)
                for k in range(1, w):
                    s = s + u_rows(POOL_STATE_ROWS + t - k, cols)
                ds.append(s * (1.0 / w) - new[t][:, cols])
            parts.append(_dot(jnp.concatenate(ds, axis=0).astype(BF16), pw_ref[g]))
        mix = jnp.concatenate(parts, axis=-1) * ps_ref[...]
        full = slice(0, D_MODEL)
        for p in range(POOL_STATE_ROWS):
            pst_ref[p] = u_rows(p + steps, full)
    else:
        mix = _dot(a_ref[...], wp_ref[...])

    x1 = x + _bcast_rows(mod_ref[2], mix, nb)
    y_ref[...] = x1
    h2_ref[...] = _norm_mod(x1, gains[1:2], mod_ref[3], mod_ref[4], nb).astype(BF16)
    acc_ref[...] = jnp.zeros_like(acc_ref)
    past_rows = (CONV_WIDTH - 1) * nb

    def chunk(j, carry):
        h2 = h2_ref[...]
        g = _dot(h2, wg_ref[j])
        u = _dot(h2, wu_ref[j])
        gall = jnp.concatenate([cpast_ref[j], g], axis=0)
        a = _ffn_chunk_math(g, gall[nb:nb + rows, :], gall[0:rows, :], u, cw_ref[j], cb_ref[j])
        acc_ref[...] += _dot(a, wd_ref[j])
        cs_ref[j] = gall[rows:rows + past_rows, :]
        return carry

    lax.fori_loop(0, N_FF_CHUNKS, chunk, 0)
    xo = y_ref[...] + _bcast_rows(mod_ref[5], acc_ref[...], nb)
    if last:
        xo = _rms(xo) * fg_ref[...]
    y_ref[...] = xo


def _layer_sample(mixer, last, x, mod, gains, fgain, mix_args, ffn, conv_past):
    rows = x.shape[0]
    nb = mod.shape[1]
    steps = rows // nb
    args = [x, mod, gains, fgain, *mix_args, *ffn, conv_past]
    out_shape = [jax.ShapeDtypeStruct((rows, D_MODEL), F32),
                 jax.ShapeDtypeStruct(conv_past.shape, F32)]
    if mixer == "pool":
        out_shape.append(jax.ShapeDtypeStruct((POOL_STATE_ROWS, nb, D_MODEL), F32))
    return pl.pallas_call(
        functools.partial(_layer_sample_kernel, mixer, last, nb, steps),
        grid=(1,),
        in_specs=[_resident(a.shape) for a in args],
        out_specs=[pl.BlockSpec(s.shape, functools.partial(lambda nd, i: (0,) * nd, len(s.shape))) for s in out_shape],
        out_shape=out_shape,
        scratch_shapes=[pltpu.VMEM((rows, D_MODEL), BF16), pltpu.VMEM((rows, D_MODEL), F32)],
        compiler_params=_params("arbitrary"),
        name=f"layer_sample_{mixer}",
    )(*args)


def _qkv_prompt_kernel(seq, x_ref, mod_ref, gains_ref, w_ref, qkvp_ref, kvn_ref, h_ref, slab_ref):
    g = pl.program_id(1)

    @pl.when(g == 0)
    def _():
        m = mod_ref[0]
        h_ref[...] = _norm_mod(x_ref[0], gains_ref[0:1, :], m[0:1], m[1:2], None).astype(BF16)

    h = h_ref[...]
    for c in range(3):
        r = _dot(h, w_ref[0, :, c * GROUP_WIDTH:(c + 1) * GROUP_WIDTH])
        if c == 0:
            r = r * (HEAD_DIM ** -0.5)
        else:
            kvn_ref[0, 0, :, (c - 1) * GROUP_WIDTH:c * GROUP_WIDTH] = r
        slab_ref[2 * c] = r[:, 0:LANES]
        slab_ref[2 * c + 1] = r[:, LANES:2 * LANES]

    for gi, (_, dil) in enumerate(DILATED_GROUPS):
        @pl.when(g == gi)
        def _(dil=dil):
            per_class = seq // dil
            for s in range(6):
                for rho in range(dil):
                    if dil == 1:
                        v = slab_ref[s]
                    else:
                        v = slab_ref[s, pl.ds(rho, per_class, stride=dil), :]
                    qkvp_ref[0, 0, rho * per_class:(rho + 1) * per_class, s * LANES:(s + 1) * LANES] = v.astype(BF16)


def _qkv_prompt(x3, mod, gains, w3):
    batch, seq, _ = x3.shape
    width = 3 * GROUP_WIDTH
    return pl.pallas_call(
        functools.partial(_qkv_prompt_kernel, seq),
        grid=(batch, N_GROUPS),
        in_specs=[pl.BlockSpec((1, seq, D_MODEL), lambda b, g: (b, 0, 0)),
                  pl.BlockSpec((1, N_MOD, D_MODEL), lambda b, g: (b, 0, 0)),
                  pl.BlockSpec((2, D_MODEL), lambda b, g: (0, 0)),
                  pl.BlockSpec((1, D_MODEL, width), lambda b, g: (g, 0, 0))],
        out_specs=[pl.BlockSpec((1, 1, seq, width), lambda b, g: (b, g, 0, 0)),
                   pl.BlockSpec((1, 1, seq, 2 * GROUP_WIDTH), lambda b, g: (b, g, 0, 0))],
        out_shape=[jax.ShapeDtypeStruct((batch, N_GROUPS, seq, width), BF16),
                   jax.ShapeDtypeStruct((batch, N_GROUPS, seq, 2 * GROUP_WIDTH), F32)],
        scratch_shapes=[pltpu.VMEM((seq, D_MODEL), BF16), pltpu.VMEM((6, seq, LANES), F32)],
        compiler_params=_params("arbitrary", "arbitrary"),
        name="qkv_prompt",
    )(x3, mod, gains, w3)


def _head_lane_mask(rows, h):
    lane = lax.broadcasted_iota(jnp.int32, (rows, GROUP_WIDTH), 1)
    return (lane >= h * HEAD_DIM) & (lane < (h + 1) * HEAD_DIM)


def _attn_block(q, k, v, bias_ref):
    o = jnp.zeros((QUERY_BLOCK, GROUP_WIDTH), F32)
    lse = jnp.zeros((QUERY_BLOCK, GROUP_WIDTH), F32)
    for h in range(HEADS_PER_GROUP):
        hm = _head_lane_mask(QUERY_BLOCK, h)
        qh = jnp.where(hm, q, jnp.zeros_like(q))
        s = _dot_nt(qh, k) + bias_ref[h]
        m = jnp.max(s, axis=-1, keepdims=True)
        p = jnp.exp(s - m)
        l = jnp.sum(p, axis=-1, keepdims=True)
        pv = _dot(p.astype(BF16), v)
        o = jnp.where(hm, pv * (1.0 / l), o)
        lse = jnp.where(hm, m + jnp.log(l), lse)
    return o, lse


def _attn_prompt_kernel(seq, qkv_ref, b1_ref, b2_ref, o_ref, os_ref, ls_ref):
    qc = slice(0, GROUP_WIDTH)
    kc = slice(GROUP_WIDTH, 2 * GROUP_WIDTH)
    vc = slice(2 * GROUP_WIDTH, 3 * GROUP_WIDTH)
    for g, (_, dil) in enumerate(DILATED_GROUPS):
        per_class = seq // dil
        blocks_per_class = per_class // QUERY_BLOCK

        def do_block(rho, blk, first, g=g, dil=dil, per_class=per_class):
            r0 = pl.multiple_of(rho * per_class + blk * QUERY_BLOCK, QUERY_BLOCK)
            q = qkv_ref[0, g, pl.ds(r0, QUERY_BLOCK), qc]
            if first:
                k = qkv_ref[0, g, pl.ds(r0, QUERY_BLOCK), kc]
                v = qkv_ref[0, g, pl.ds(r0, QUERY_BLOCK), vc]
                o, lse = _attn_block(q, k, v, b1_ref.at[g])
            else:
                rk = pl.multiple_of(r0 - QUERY_BLOCK, QUERY_BLOCK)
                k = qkv_ref[0, g, pl.ds(rk, 2 * QUERY_BLOCK), kc]
                v = qkv_ref[0, g, pl.ds(rk, 2 * QUERY_BLOCK), vc]
                o, lse = _attn_block(q, k, v, b2_ref.at[g])
            start = blk * QUERY_BLOCK * dil + rho
            for s in range(2):
                cols = slice(s * LANES, (s + 1) * LANES)
                if dil == 1:
                    os_ref[g, s, pl.ds(r0, QUERY_BLOCK), :] = o[:, cols]
                    ls_ref[g, s, pl.ds(r0, QUERY_BLOCK), :] = lse[:, cols]
                else:
                    os_ref[g, s, pl.ds(start, QUERY_BLOCK, stride=dil), :] = o[:, cols]
                    ls_ref[g, s, pl.ds(start, QUERY_BLOCK, stride=dil), :] = lse[:, cols]

        def class_body(rho, carry, do_block=do_block, blocks_per_class=blocks_per_class):
            do_block(rho, 0, True)
            if blocks_per_class > 1:
                def inner(blk, c):
                    do_block(rho, blk, False)
                    return c
                lax.fori_loop(1, blocks_per_class, inner, 0)
            return carry

        lax.fori_loop(0, dil, class_body, 0)

    def merge(i, carry):
        r0 = pl.multiple_of(i * QUERY_BLOCK, QUERY_BLOCK)
        rows = pl.ds(r0, QUERY_BLOCK)
        for s in range(2):
            ls = [ls_ref[g, s, rows, :] for g in range(N_GROUPS)]
            mx = jnp.maximum(jnp.maximum(ls[0], ls[1]), ls[2])
            es = [jnp.exp(l - mx) for l in ls]
            inv = 1.0 / (es[0] + es[1] + es[2])
            for g in range(N_GROUPS):
                c0 = g * GROUP_WIDTH + s * LANES
                o_ref[0, rows, c0:c0 + LANES] = (os_ref[g, s, rows, :] * (es[g] * inv)).astype(BF16)
        return carry

    lax.fori_loop(0, seq // QUERY_BLOCK, merge, 0)


def _attn_prompt(qkvp, bias1, bias2):
    batch, _, seq, width = qkvp.shape
    return pl.pallas_call(
        functools.partial(_attn_prompt_kernel, seq),
        grid=(batch,),
        in_specs=[pl.BlockSpec((1, N_GROUPS, seq, width), lambda b: (b, 0, 0, 0)),
                  _resident(bias1.shape), _resident(bias2.shape)],
        out_specs=pl.BlockSpec((1, seq, ATTN_INNER), lambda b: (b, 0, 0)),
        out_shape=jax.ShapeDtypeStruct((batch, seq, ATTN_INNER), BF16),
        scratch_shapes=[pltpu.VMEM((N_GROUPS, 2, seq, LANES), F32), pltpu.VMEM((N_GROUPS, 2, seq, LANES), F32)],
        compiler_params=_params("arbitrary"),
        name="attn_prompt",
    )(qkvp, bias1, bias2)


def _proj_sample_kernel(nb, x_ref, mod_ref, gains_ref, w_ref, o_ref):
    h = _norm_mod(x_ref[...], gains_ref[0:1, :], mod_ref[0], mod_ref[1], nb).astype(BF16)
    o_ref[...] = _dot(h, w_ref[...])


def _proj_sample(x, mod, gains, w):
    rows = x.shape[0]
    nb = mod.shape[1]
    args = [x, mod, gains, w]
    return pl.pallas_call(
        functools.partial(_proj_sample_kernel, nb),
        grid=(1,),
        in_specs=[_resident(a.shape) for a in args],
        out_specs=pl.BlockSpec((rows, w.shape[1]), lambda i: (0, 0)),
        out_shape=jax.ShapeDtypeStruct((rows, w.shape[1]), F32),
        compiler_params=_params("arbitrary"),
        name="qkv_sample",
    )(*args)


def _attn_sample_kernel(steps, q_ref, kn_ref, vn_ref, c1_ref, c2_ref, c3_ref, bp1_ref, bp2_ref, bp3_ref,
                        bn_ref, o_ref):
    nrow = HEADS_PER_GROUP * steps
    lane = lax.broadcasted_iota(jnp.int32, (nrow, GROUP_WIDTH), 1)
    row = lax.broadcasted_iota(jnp.int32, (nrow, GROUP_WIDTH), 0)
    hm = (lane // HEAD_DIM) == (row // steps)
    outs, lses = [], []
    for g, (c_ref, bp_ref) in enumerate(((c1_ref, bp1_ref), (c2_ref, bp2_ref), (c3_ref, bp3_ref))):
        qs = jnp.where(hm, q_ref[0, g] * (HEAD_DIM ** -0.5), 0.0)
        kp = c_ref[0, :, 0:GROUP_WIDTH].astype(BF16)
        vp = c_ref[0, :, GROUP_WIDTH:2 * GROUP_WIDTH].astype(BF16)
        s = _dot_nt(qs.astype(BF16), kp) + bp_ref[...]
        kn = kn_ref[0, g]
        vn = vn_ref[0, g]
        bn = bn_ref[g]
        sn = [jnp.sum(qs * kn[t:t + 1, :], axis=-1, keepdims=True) + bn[:, t:t + 1] for t in range(steps)]
        m = jnp.max(s, axis=-1, keepdims=True)
        for t in range(steps):
            m = jnp.maximum(m, sn[t])
        p = jnp.exp(s - m)
        l = jnp.sum(p, axis=-1, keepdims=True)
        o = _dot(p.astype(BF16), vp)
        for t in range(steps):
            pn = jnp.exp(sn[t] - m)
            l = l + pn
            o = o + pn * vn[t:t + 1, :]
        outs.append(o * (1.0 / l))
        lses.append(m + jnp.log(l))
    mx = jnp.maximum(jnp.maximum(lses[0], lses[1]), lses[2])
    es = [jnp.exp(l - mx) for l in lses]
    inv = 1.0 / (es[0] + es[1] + es[2])
    for g in range(N_GROUPS):
        o_ref[0, :, g * GROUP_WIDTH:(g + 1) * GROUP_WIDTH] = outs[g] * (es[g] * inv)


def _attn_sample(steps, q16, kn, vn, caches, bias_past, bias_new):
    nb = q16.shape[0]
    nrow = q16.shape[2]
    in_specs = [pl.BlockSpec((1, N_GROUPS, nrow, GROUP_WIDTH), lambda b: (b, 0, 0, 0)),
                pl.BlockSpec((1, N_GROUPS, SUBLANES, GROUP_WIDTH), lambda b: (b, 0, 0, 0)),
                pl.BlockSpec((1, N_GROUPS, SUBLANES, GROUP_WIDTH), lambda b: (b, 0, 0, 0))]
    in_specs += [pl.BlockSpec((1,) + c.shape[1:], lambda b: (b, 0, 0)) for c in caches]
    in_specs += [_resident(b.shape) for b in bias_past] + [_resident(bias_new.shape)]
    return pl.pallas_call(
        functools.partial(_attn_sample_kernel, steps),
        grid=(nb,), in_specs=in_specs,
        out_specs=pl.BlockSpec((1, nrow, ATTN_INNER), lambda b: (b, 0, 0)),
        out_shape=jax.ShapeDtypeStruct((nb, nrow, ATTN_INNER), F32),
        compiler_params=_params("arbitrary"),
        name="attn_sample",
    )(q16, kn, vn, *caches, *bias_past, bias_new)


def _gla_proj_kernel(nb, x_ref, mod_ref, gains_ref, w_ref, wgd_ref, wgu_ref, bg_ref,
                     q_ref, k_ref, v_ref, r_ref, la_ref):
    if nb is None:
        m = mod_ref[0]
        shift, scale = m[0:1], m[1:2]
    else:
        shift, scale = mod_ref[0], mod_ref[1]
    h = _norm_mod(x_ref[...], gains_ref[0:1, :], shift, scale, nb).astype(BF16)
    q_ref[...] = _dot(h, w_ref[:, 0:GLA_QK]) * (GLA_DK ** -0.5)
    k_ref[...] = _dot(h, w_ref[:, GLA_QK:2 * GLA_QK])
    v_ref[...] = _dot(h, w_ref[:, 2 * GLA_QK:2 * GLA_QK + GLA_V]).astype(BF16)
    r_ref[...] = _dot(h, w_ref[:, 2 * GLA_QK + GLA_V:2 * GLA_QK + 2 * GLA_V])
    gd = _dot(h, wgd_ref[...])
    gate = _dot(gd.astype(BF16), wgu_ref[...]) + bg_ref[...]
    la_ref[...] = jax.nn.log_sigmoid(gate) * (1.0 / GATE_TAU)


def _gla_proj(x, mod, gains, weights, nb, tm):
    rows = x.shape[0]
    if nb is None:
        tps = rows // mod.shape[0] // tm
        mod_spec = pl.BlockSpec((1, N_MOD, D_MODEL), lambda i: (i // tps, 0, 0))
    else:
        mod_spec = _resident(mod.shape)
    widths = (GLA_QK, GLA_QK, GLA_V, GLA_V, GLA_QK)
    dtypes = (F32, F32, BF16, F32, F32)
    return pl.pallas_call(
        functools.partial(_gla_proj_kernel, nb),
        grid=(rows // tm,),
        in_specs=[pl.BlockSpec((tm, D_MODEL), lambda i: (i, 0)), mod_spec, _resident((2, D_MODEL))]
                 + [_resident(w.shape) for w in weights],
        out_specs=[pl.BlockSpec((tm, w), lambda i: (i, 0)) for w in widths],
        out_shape=[jax.ShapeDtypeStruct((rows, w), dt) for w, dt in zip(widths, dtypes)],
        compiler_params=_params("arbitrary"),
        name="gla_proj",
    )(x, mod, gains, *weights)


def _cumsum_rows(g):
    rows = g.shape[0]
    row = lax.broadcasted_iota(jnp.int32, g.shape, 0)
    b = g
    shift = 1
    while shift < rows:
        b = b + jnp.where(row >= shift, pltpu.roll(b, shift, 0), 0.0)
        shift *= 2
    return b


def _gla_chunk(q, k, v, g, r, gain, mid, get_state, set_state):
    c = q.shape[0]
    b = _cumsum_rows(g)
    b_end = b[c - 1:c, :]
    b_mid = b[mid:mid + 1, :]
    q_in = (q * jnp.exp(b)).astype(BF16)
    q_rel = (q * jnp.exp(b - b_mid)).astype(BF16)
    k_rel = (k * jnp.exp(b_mid - b)).astype(BF16)
    k_out = (k * jnp.exp(b_end - b)).astype(BF16)
    decay_end = jnp.broadcast_to(jnp.exp(b_end), (SUBLANES, GLA_QK))
    ti = lax.broadcasted_iota(jnp.int32, (c, c), 0)
    si = lax.broadcasted_iota(jnp.int32, (c, c), 1)
    outs = []
    for h in range(GLA_HEADS):
        ks = slice(h * GLA_DK, (h + 1) * GLA_DK)
        vs = slice(h * GLA_DV, (h + 1) * GLA_DV)
        state = get_state(h)
        o = _dot(q_in[:, ks], state.astype(BF16))
        att = jnp.where(si <= ti, _dot_nt(q_rel[:, ks], k_rel[:, ks]), 0.0)
        o = o + _dot(att.astype(BF16), v[:, vs])
        decay_col = decay_end[:, ks].T[:, 0:1]
        set_state(h, decay_col * state + _dot_tn(k_out[:, ks], v[:, vs]))
        outs.append(_rms(o) * gain[:, vs] * _silu(r[:, vs]))
    return jnp.concatenate(outs, axis=-1)


def _gla_prompt_kernel(tt, q_ref, k_ref, v_ref, r_ref, la_ref, gain_ref, a_ref, so_ref, s_ref):
    t = pl.program_id(1)

    @pl.when(t == 0)
    def _():
        s_ref[...] = jnp.zeros_like(s_ref)

    gain = gain_ref[...]

    def get_state(h):
        return s_ref[h]

    def set_state(h, val):
        s_ref[h] = val

    def chunk(ci, carry):
        rows = pl.ds(pl.multiple_of(ci * GLA_CHUNK, GLA_CHUNK), GLA_CHUNK)
        a = _gla_chunk(q_ref[0, rows, :], k_ref[0, rows, :], v_ref[0, rows, :], la_ref[0, rows, :],
                       r_ref[0, rows, :], gain, GLA_CHUNK // 2, get_state, set_state)
        a_ref[0, rows, :] = a.astype(BF16)
        return carry

    lax.fori_loop(0, tt // GLA_CHUNK, chunk, 0)

    @pl.when(t == pl.num_programs(1) - 1)
    def _():
        so_ref[0] = s_ref[...]


def _gla_prompt(q, k, v, r, la, gain):
    batch, seq, _ = q.shape
    tt = GLA_TIME_TILE

    def spec(width):
        return pl.BlockSpec((1, tt, width), lambda b, t: (b, t, 0))

    return pl.pallas_call(
        functools.partial(_gla_prompt_kernel, tt),
        grid=(batch, seq // tt),
        in_specs=[spec(GLA_QK), spec(GLA_QK), spec(GLA_V), spec(GLA_V), spec(GLA_QK),
                  pl.BlockSpec((1, GLA_V), lambda b, t: (0, 0))],
        out_specs=[spec(GLA_V), pl.BlockSpec((1, GLA_HEADS, GLA_DK, GLA_DV), lambda b, t: (b, 0, 0, 0))],
        out_shape=[jax.ShapeDtypeStruct((batch, seq, GLA_V), BF16),
                   jax.ShapeDtypeStruct((batch, GLA_HEADS, GLA_DK, GLA_DV), F32)],
        scratch_shapes=[pltpu.VMEM((GLA_HEADS, GLA_DK, GLA_DV), F32)],
        compiler_params=_params("arbitrary", "arbitrary"),
        name="gla_prompt",
    )(q, k, v, r, la, gain)


def _gla_sample_kernel(sb, q_ref, k_ref, v_ref, r_ref, la_ref, gain_ref, s0_ref, a_ref, so_ref):
    gain = gain_ref[...]

    def seq_body(i, carry):
        def get_state(h):
            return s0_ref[i, h]

        def set_state(h, val):
            so_ref[i, h] = val

        a_ref[i] = _gla_chunk(q_ref[i], k_ref[i], v_ref[i], la_ref[i], r_ref[i], gain, 0, get_state, set_state)
        return carry

    lax.fori_loop(0, sb, seq_body, 0)


def _gla_sample(q, k, v, r, la, gain, s0):
    nb, pad, _ = q.shape
    sb = math.gcd(nb, 8)

    def spec(width):
        return pl.BlockSpec((sb, pad, width), lambda i: (i, 0, 0))

    state_spec = pl.BlockSpec((sb, GLA_HEADS, GLA_DK, GLA_DV), lambda i: (i, 0, 0, 0))
    return pl.pallas_call(
        functools.partial(_gla_sample_kernel, sb),
        grid=(nb // sb,),
        in_specs=[spec(GLA_QK), spec(GLA_QK), spec(GLA_V), spec(GLA_V), spec(GLA_QK),
                  pl.BlockSpec((1, GLA_V), lambda i: (0, 0)), state_spec],
        out_specs=[spec(GLA_V), state_spec],
        out_shape=[jax.ShapeDtypeStruct((nb, pad, GLA_V), F32),
                   jax.ShapeDtypeStruct((nb, GLA_HEADS, GLA_DK, GLA_DV), F32)],
        compiler_params=_params("arbitrary"),
        name="gla_sample",
    )(q, k, v, r, la, gain, s0)


def _t5_bucket(dist):
    max_exact = NUM_BUCKETS // 2
    d_f = jnp.maximum(dist, 1).astype(F32)
    large = max_exact + (jnp.log(d_f / max_exact) / math.log(MAX_DISTANCE / max_exact)
                         * (NUM_BUCKETS - max_exact)).astype(jnp.int32)
    large = jnp.minimum(large, NUM_BUCKETS - 1)
    return jnp.where(dist < max_exact, dist, large)


def _group_bias(rel_bias):
    rows = []
    for g, (window, dil) in enumerate(DILATED_GROUPS):
        buckets = _t5_bucket(jnp.arange(window // dil + 1) * dil)
        rows.append(rel_bias[buckets][:, g * HEADS_PER_GROUP:(g + 1) * HEADS_PER_GROUP].T)
    return jnp.stack(rows)


def _lookup(bias_gh, idx, valid):
    vals = jnp.take(bias_gh, jnp.asarray(np.where(valid, idx, 0)), axis=-1)
    return jnp.where(jnp.asarray(valid), vals, NEG_INF)


def _prompt_bias_tables(gb):
    qi = np.arange(QUERY_BLOCK)[:, None]
    k1 = np.arange(QUERY_BLOCK)[None, :]
    k2 = np.arange(2 * QUERY_BLOCK)[None, :]
    i1 = qi - k1
    i2 = qi + QUERY_BLOCK - k2
    b1 = _lookup(gb, i1, (i1 >= 0) & (i1 < KEYS_PER_QUERY))
    b2 = _lookup(gb, i2, (i2 >= 0) & (i2 < KEYS_PER_QUERY))
    return b1, b2


def _sample_bias_tables(gb, steps, past_rows):
    t_of_row = np.tile(np.arange(steps), HEADS_PER_GROUP)[:, None]
    h_of_row = np.repeat(np.arange(HEADS_PER_GROUP), steps)
    past, new = [], []
    for g, (_, dil) in enumerate(DILATED_GROUPS):
        p = past_rows[g]
        gbh = gb[g][h_of_row]
        delta = p + t_of_row - np.arange(p)[None, :]
        j = delta // dil
        valid = (delta % dil == 0) & (j >= 0) & (j < KEYS_PER_QUERY)
        past.append(jnp.take_along_axis(gbh, jnp.asarray(np.where(valid, j, 0)), axis=1))
        past[-1] = jnp.where(jnp.asarray(valid), past[-1], NEG_INF)
        dn = t_of_row - np.arange(SUBLANES)[None, :]
        jn = dn // dil
        vn = (dn >= 0) & (dn % dil == 0) & (jn < KEYS_PER_QUERY) & (np.arange(SUBLANES)[None, :] < steps)
        nv = jnp.take_along_axis(gbh, jnp.asarray(np.where(vn, jn, 0)), axis=1)
        new.append(jnp.where(jnp.asarray(vn), nv, NEG_INF))
    return past, jnp.stack(new)


def _ffn_weights(w_in, conv_w, conv_b, w_down):
    def cols(w):
        return jnp.transpose(w.reshape(D_MODEL, N_FF_CHUNKS, FF_CHUNK), (1, 0, 2)).astype(BF16)
    return (cols(w_in[:, :D_FF]), cols(w_in[:, D_FF:]),
            w_down.reshape(N_FF_CHUNKS, FF_CHUNK, D_MODEL).astype(BF16),
            jnp.transpose(conv_w.reshape(CONV_WIDTH, N_FF_CHUNKS, FF_CHUNK), (1, 0, 2)),
            conv_b.reshape(N_FF_CHUNKS, 1, FF_CHUNK))


def _conv_tail_prompt(cs):
    batch = cs.shape[0]
    tail = cs[:, :, SUBLANES - (CONV_WIDTH - 1):, :]
    return jnp.transpose(tail, (0, 2, 1, 3)).reshape(batch, CONV_WIDTH - 1, D_FF)


def _conv_past_sample(state):
    nb = state.shape[0]
    s = state.reshape(nb, CONV_WIDTH - 1, N_FF_CHUNKS, FF_CHUNK)
    return jnp.transpose(s, (2, 1, 0, 3)).reshape(N_FF_CHUNKS, (CONV_WIDTH - 1) * nb, FF_CHUNK)


def _conv_tail_sample(cs, nb):
    s = cs.reshape(N_FF_CHUNKS, CONV_WIDTH - 1, nb, FF_CHUNK)
    return jnp.transpose(s, (2, 1, 0, 3)).reshape(nb, CONV_WIDTH - 1, D_FF)


def kernel(x_prompt, x_sample, state_pool, cache_win_g1, cache_win_g2, cache_win_g3, state_gla, state_ffn_conv,
           c_prompt, c_sample, w_ada, b_ada, norm_gain, final_gain, rel_bias, pool_w, pool_scale,
           attn_w_in, attn_w_out, gla_w_in, gla_w_gate_up, gla_b_gate, gla_norm_gain, gla_w_out,
           ffn_w_in, ffn_conv_w, ffn_conv_b, ffn_w_down):
    batch, seq, _ = x_prompt.shape
    nb, steps, _ = x_sample.shape
    caches = (cache_win_g1, cache_win_g2, cache_win_g3)

    mods = _modulation(jnp.concatenate([c_prompt, c_sample], axis=0), w_ada, b_ada)
    mod_p = mods[:, :batch].reshape(DEPTH, batch, N_MOD, D_MODEL)
    mod_s = jnp.transpose(mods[:, batch:].reshape(DEPTH, nb, N_MOD, D_MODEL), (0, 2, 1, 3))
    fgain = final_gain.reshape(1, D_MODEL)

    xp = x_prompt.reshape(batch * seq, D_MODEL)
    xs = jnp.transpose(x_sample, (1, 0, 2)).reshape(steps * nb, D_MODEL)

    pool_p, pool_s, gla_p, gla_s, conv_p, conv_s = [], [], [], [], [], []
    win_p, win_s = None, None

    for i in range(DEPTH):
        kind, j = i % 3, i // 3
        last = i == DEPTH - 1
        ffn = _ffn_weights(ffn_w_in[i], ffn_conv_w[i], ffn_conv_b[i], ffn_w_down[i])
        conv_past = _conv_past_sample(state_ffn_conv[i])
        gains = norm_gain[i]
        if kind == 0:
            mix_w = (pool_w[j].astype(BF16), pool_scale[j].reshape(1, D_MODEL))
            xp, cs, pst = _layer_prompt("pool", last, xp, mod_p[i], gains, fgain, mix_w, ffn)
            pool_p.append(pst[:, POOL_CARRY_ROWS - POOL_STATE_ROWS:])
            past = jnp.transpose(state_pool[j], (1, 0, 2))
            xs, css, psts = _layer_sample("pool", last, xs, mod_s[i], gains, fgain, mix_w + (past,), ffn, conv_past)
            pool_s.append(jnp.transpose(psts, (1, 0, 2)))
        elif kind == 1:
            w = attn_w_in[j]
            w3 = jnp.stack([jnp.concatenate([w[:, s * ATTN_INNER + g * GROUP_WIDTH:
                                                s * ATTN_INNER + (g + 1) * GROUP_WIDTH] for s in range(3)], axis=1)
                            for g in range(N_GROUPS)]).astype(BF16)
            gb = _group_bias(rel_bias)
            b1, b2 = _prompt_bias_tables(gb)
            wo = attn_w_out[j].astype(BF16)
            qkvp, kvn = _qkv_prompt(xp.reshape(batch, seq, D_MODEL), mod_p[i], gains, w3)
            o_all = _attn_prompt(qkvp, b1, b2)
            win_p = [kvn[:, g, seq - min(window, seq):].reshape(1, batch, min(window, seq), 2, HEADS_PER_GROUP, HEAD_DIM)
                     for g, (window, _) in enumerate(DILATED_GROUPS)]
            xp, cs = _layer_prompt("proj", last, xp, mod_p[i], gains, fgain,
                                   (o_all.reshape(batch * seq, ATTN_INNER), wo), ffn)
            qkv_s = _proj_sample(xs, mod_s[i], gains, w.astype(BF16))
            q5 = jnp.transpose(qkv_s.reshape(steps, nb, 3, N_GROUPS, GROUP_WIDTH), (2, 1, 3, 0, 4))
            q16 = jnp.tile(q5[0], (1, 1, HEADS_PER_GROUP, 1))
            pad = ((0, 0), (0, 0), (0, SUBLANES - steps), (0, 0))
            kn, vn = jnp.pad(q5[1], pad), jnp.pad(q5[2], pad)
            past_rows = [c.shape[2] for c in caches]
            cache2d = [c[j].reshape(nb, p, 2 * GROUP_WIDTH) for c, p in zip(caches, past_rows)]
            bias_past, bias_new = _sample_bias_tables(gb, steps, past_rows)
            o16 = _attn_sample(steps, q16, kn, vn, cache2d, bias_past, bias_new)
            o6 = o16.reshape(nb, HEADS_PER_GROUP, steps, N_GROUPS, HEADS_PER_GROUP, HEAD_DIM)
            o_sel = jnp.stack([o6[:, h, :, :, h, :] for h in range(HEADS_PER_GROUP)], axis=3)
            a_s = jnp.transpose(o_sel.reshape(nb, steps, ATTN_INNER), (1, 0, 2)).reshape(steps * nb, ATTN_INNER)
            win_s = []
            for g in range(N_GROUPS):
                new_rows = jnp.concatenate([q5[1][:, g], q5[2][:, g]], axis=-1)
                keep = min(DILATED_GROUPS[g][0], past_rows[g] + steps)
                full = jnp.concatenate([cache2d[g], new_rows], axis=1)
                win_s.append(full[:, past_rows[g] + steps - keep:].reshape(1, nb, keep, 2, HEADS_PER_GROUP, HEAD_DIM))
            xs, css = _layer_sample("proj", last, xs, mod_s[i], gains, fgain, (a_s.astype(BF16), wo), ffn, conv_past)
        else:
            w = gla_w_in[j]
            n_main = 2 * GLA_QK + 2 * GLA_V
            weights = (w[:, :n_main].astype(BF16),
                       jnp.pad(w[:, n_main:], ((0, 0), (0, LANES - GATE_RANK))).astype(BF16),
                       jnp.pad(gla_w_gate_up[j], ((0, LANES - GATE_RANK), (0, 0))).astype(BF16),
                       gla_b_gate[j].reshape(1, GLA_QK))
            gain = gla_norm_gain[j].reshape(1, GLA_V)
            wo = gla_w_out[j].astype(BF16)
            q, k, v, r, la = _gla_proj(xp, mod_p[i], gains, weights, None, PROMPT_ROW_TILE)
            shp = lambda a: a.reshape(batch, seq, a.shape[-1])
            a_p, s_p = _gla_prompt(shp(q), shp(k), shp(v), shp(r), shp(la), gain)
            gla_p.append(s_p)
            xp, cs = _layer_prompt("proj", last, xp, mod_p[i], gains, fgain,
                                   (a_p.reshape(batch * seq, GLA_V), wo), ffn)
            outs = _gla_proj(xs, mod_s[i], gains, weights, nb, steps * nb)

            def per_seq(a):
                a = jnp.transpose(a.reshape(steps, nb, a.shape[-1]), (1, 0, 2))
                return jnp.pad(a, ((0, 0), (0, SAMPLE_DEC_PAD - steps), (0, 0)))

            qs, ks, vs, rs, las = (per_seq(a) for a in outs)
            a16, s_s = _gla_sample(qs, ks, vs, rs, las, gain, state_gla[j])
            gla_s.append(s_s)
            a_s = jnp.transpose(a16[:, :steps], (1, 0, 2)).reshape(steps * nb, GLA_V).astype(BF16)
            xs, css = _layer_sample("proj", last, xs, mod_s[i], gains, fgain, (a_s, wo), ffn, conv_past)
        conv_p.append(_conv_tail_prompt(cs))
        conv_s.append(_conv_tail_sample(css, nb))

    y_prompt = xp.reshape(batch, seq, D_MODEL)
    y_sample = jnp.transpose(xs.reshape(steps, nb, D_MODEL), (1, 0, 2))
    return (y_prompt, y_sample, jnp.stack(pool_p), jnp.stack(pool_s),
            win_p[0], win_s[0], win_p[1], win_s[1], win_p[2], win_s[2],
            jnp.stack(gla_p), jnp.stack(gla_s), jnp.stack(conv_p), jnp.stack(conv_s))
```

```python
import functools
import math

import numpy as np
import jax
import jax.numpy as jnp
from jax import lax
from jax.experimental import pallas as pl
from jax.experimental.pallas import tpu as pltpu

F32 = jnp.float32
BF16 = jnp.bfloat16

D_MODEL = 1024
DEPTH = 4
N_MOD = 6
EPS = 1e-6
NEG_INF = -1e30
POOL_WINDOWS = (2, 4, 8, 16)
POOL_GROUP_DIM = D_MODEL // len(POOL_WINDOWS)
POOL_STATE_ROWS = max(POOL_WINDOWS) - 1
POOL_CARRY_ROWS = 16
DILATED_GROUPS = ((128, 1), (512, 4), (2048, 16))
N_GROUPS = len(DILATED_GROUPS)
HEADS_PER_GROUP = 4
HEAD_DIM = 64
GROUP_WIDTH = HEADS_PER_GROUP * HEAD_DIM
ATTN_INNER = N_GROUPS * GROUP_WIDTH
KEYS_PER_QUERY = 129
QUERY_BLOCK = 128
NUM_BUCKETS = 32
MAX_DISTANCE = 2048
GLA_HEADS = 4
GLA_DK = 128
GLA_DV = 256
GLA_QK = GLA_HEADS * GLA_DK
GLA_V = GLA_HEADS * GLA_DV
GATE_RANK = 16
GATE_TAU = 16.0
GLA_CHUNK = 64
D_FF = 2816
CONV_WIDTH = 3

LANES = 128
SUBLANES = 8
FF_CHUNK = 256
N_FF_CHUNKS = D_FF // FF_CHUNK
VMEM_LIMIT_BYTES = 56 * 1024 * 1024
PROMPT_ROW_TILE = 512
GLA_TIME_TILE = 512
SAMPLE_DEC_PAD = 16


def _params(*semantics):
    return pltpu.CompilerParams(dimension_semantics=semantics, vmem_limit_bytes=VMEM_LIMIT_BYTES)


def _resident(shape):
    nd = len(shape)
    return pl.BlockSpec(shape, lambda *_: (0,) * nd, pipeline_mode=pl.Buffered(1))


def _dot(a, b):
    return jnp.dot(a, b, preferred_element_type=F32)


def _dot_nt(a, b):
    return lax.dot_general(a, b, (((1,), (1,)), ((), ())), preferred_element_type=F32)


def _dot_tn(a, b):
    return lax.dot_general(a, b, (((0,), (0,)), ((), ())), preferred_element_type=F32)


def _rms(x):
    return x * lax.rsqrt(jnp.mean(x * x, axis=-1, keepdims=True) + EPS)


def _bcast_rows(v, y, nb):
    if nb is None:
        return v * y
    rows, width = y.shape
    return (y.reshape(rows // nb, nb, width) * v[None]).reshape(rows, width)


def _norm_mod(x, gain, shift, scale, nb):
    y = _rms(x) * gain
    if nb is None:
        return y * (1.0 + scale) + shift
    rows, width = y.shape
    y3 = y.reshape(rows // nb, nb, width)
    return (y3 * (1.0 + scale)[None] + shift[None]).reshape(rows, width)


def _gelu(x):
    return 0.5 * x * (1.0 + lax.erf(x * (1.0 / math.sqrt(2.0))))


def _silu(x):
    return x * jax.nn.sigmoid(x)


def _split_bf16(a):
    hi = a.astype(BF16)
    lo = (a - hi.astype(F32)).astype(BF16)
    return hi, lo


def _mod_kernel(c_ref, w_ref, b_ref, o_ref):
    a_hi, a_lo = _split_bf16(_silu(c_ref[...]))
    w_hi, w_lo = _split_bf16(w_ref[0])
    o_ref[0] = _dot(a_hi, w_hi) + _dot(a_lo, w_hi) + _dot(a_hi, w_lo) + b_ref[0]


def _modulation(c_all, w_ada, b_ada):
    rows = c_all.shape[0]
    width = N_MOD * D_MODEL
    tn = 1536
    return pl.pallas_call(
        _mod_kernel,
        grid=(DEPTH, width // tn),
        in_specs=[pl.BlockSpec((rows, D_MODEL), lambda l, n: (0, 0)),
                  pl.BlockSpec((1, D_MODEL, tn), lambda l, n: (l, 0, n)),
                  pl.BlockSpec((1, 1, tn), lambda l, n: (l, 0, n))],
        out_specs=pl.BlockSpec((1, rows, tn), lambda l, n: (l, 0, n)),
        out_shape=jax.ShapeDtypeStruct((DEPTH, rows, width), F32),
        compiler_params=_params("parallel", "parallel"),
        name="adaln_mod",
    )(c_all, w_ada, b_ada.reshape(DEPTH, 1, width))


def _ffn_chunk_math(g, g_m1, g_m2, u, cw, cb):
    gc = cw[2:3] * g + cw[1:2] * g_m1 + cw[0:1] * g_m2 + cb
    return (_gelu(gc) * u).astype(BF16)


def _layer_prompt_kernel(mixer, last, tm, tiles_per_seq, *refs):
    refs = list(refs)
    x_ref, mod_ref, gains_ref, fg_ref = refs[:4]
    refs = refs[4:]
    if mixer == "pool":
        pw_ref, ps_ref = refs[:2]
    else:
        a_ref, wp_ref = refs[:2]
    wg_ref, wu_ref, wd_ref, cw_ref, cb_ref = refs[2:7]
    refs = refs[7:]
    if mixer == "pool":
        y_ref, cs_ref, pst_ref, h2_ref, acc_ref, gext_ref, cc_ref, hext_ref = refs
    else:
        y_ref, cs_ref, h2_ref, acc_ref, gext_ref, cc_ref = refs

    i = pl.program_id(0)
    tile_in_seq = i % tiles_per_seq

    @pl.when(tile_in_seq == 0)
    def _():
        cc_ref[...] = jnp.zeros_like(cc_ref)
        if mixer == "pool":
            hext_ref[0:POOL_CARRY_ROWS, :] = jnp.zeros((POOL_CARRY_ROWS, D_MODEL), F32)

    x = x_ref[...]
    m = mod_ref[0]
    gains = gains_ref[...]

    if mixer == "pool":
        h = _norm_mod(x, gains[0:1], m[0:1], m[1:2], None)
        hext_ref[POOL_CARRY_ROWS:, :] = h
        pos = tile_in_seq * tm + lax.broadcasted_iota(jnp.int32, (tm, 1), 0)
        parts = []
        for g, w in enumerate(POOL_WINDOWS):
            cols = slice(g * POOL_GROUP_DIM, (g + 1) * POOL_GROUP_DIM)
            s = hext_ref[pl.ds(POOL_CARRY_ROWS, tm), cols]
            for k in range(1, w):
                s = s + hext_ref[pl.ds(POOL_CARRY_ROWS - k, tm), cols]
            inv_count = 1.0 / jnp.minimum(pos + 1, w).astype(F32)
            d = s * inv_count - h[:, cols]
            parts.append(_dot(d.astype(BF16), pw_ref[g]))
        mix = jnp.concatenate(parts, axis=-1) * ps_ref[...]
        tail = hext_ref[pl.ds(tm, POOL_CARRY_ROWS), :]
        hext_ref[0:POOL_CARRY_ROWS, :] = tail
        pst_ref[0] = tail
    else:
        mix = _dot(a_ref[...], wp_ref[...])

    x1 = x + m[2:3] * mix
    y_ref[...] = x1
    h2_ref[...] = _norm_mod(x1, gains[1:2], m[3:4], m[4:5], None).astype(BF16)
    acc_ref[...] = jnp.zeros_like(acc_ref)

    def chunk(j, carry):
        h2 = h2_ref[...]
        g = _dot(h2, wg_ref[j])
        u = _dot(h2, wu_ref[j])
        gext_ref[0:SUBLANES, :] = cc_ref[j]
        gext_ref[SUBLANES:, :] = g
        a = _ffn_chunk_math(g, gext_ref[pl.ds(SUBLANES - 1, tm), :], gext_ref[pl.ds(SUBLANES - 2, tm), :],
                            u, cw_ref[j], cb_ref[j])
        acc_ref[...] += _dot(a, wd_ref[j])
        tail = g[tm - SUBLANES:tm, :]
        cc_ref[j] = tail
        cs_ref[0, j] = tail
        return carry

    lax.fori_loop(0, N_FF_CHUNKS, chunk, 0)
    xo = y_ref[...] + m[5:6] * acc_ref[...]
    if last:
        xo = _rms(xo) * fg_ref[...]
    y_ref[...] = xo


def _layer_prompt(mixer, last, x, mod, gains, fgain, mix_args, ffn):
    n = x.shape[0]
    batch = mod.shape[0]
    seq = n // batch
    tm = PROMPT_ROW_TILE
    tps = seq // tm
    in_specs = [pl.BlockSpec((tm, D_MODEL), lambda i: (i, 0)),
                pl.BlockSpec((1, N_MOD, D_MODEL), lambda i: (i // tps, 0, 0)),
                _resident((2, D_MODEL)), _resident((1, D_MODEL))]
    if mixer == "pool":
        pw, ps = mix_args
        in_specs += [_resident(pw.shape), _resident(ps.shape)]
    else:
        a, wp = mix_args
        in_specs += [pl.BlockSpec((tm, a.shape[1]), lambda i: (i, 0)), _resident(wp.shape)]
    in_specs += [_resident(w.shape) for w in ffn]
    out_shape = [jax.ShapeDtypeStruct((n, D_MODEL), F32),
                 jax.ShapeDtypeStruct((batch, N_FF_CHUNKS, SUBLANES, FF_CHUNK), F32)]
    out_specs = [pl.BlockSpec((tm, D_MODEL), lambda i: (i, 0)),
                 pl.BlockSpec((1, N_FF_CHUNKS, SUBLANES, FF_CHUNK), lambda i: (i // tps, 0, 0, 0))]
    scratch = [pltpu.VMEM((tm, D_MODEL), BF16), pltpu.VMEM((tm, D_MODEL), F32),
               pltpu.VMEM((tm + SUBLANES, FF_CHUNK), F32), pltpu.VMEM((N_FF_CHUNKS, SUBLANES, FF_CHUNK), F32)]
    if mixer == "pool":
        out_shape.append(jax.ShapeDtypeStruct((batch, POOL_CARRY_ROWS, D_MODEL), F32))
        out_specs.append(pl.BlockSpec((1, POOL_CARRY_ROWS, D_MODEL), lambda i: (i // tps, 0, 0)))
        scratch.append(pltpu.VMEM((tm + POOL_CARRY_ROWS, D_MODEL), F32))
    return pl.pallas_call(
        functools.partial(_layer_prompt_kernel, mixer, last, tm, tps),
        grid=(n // tm,), in_specs=in_specs, out_specs=out_specs, out_shape=out_shape,
        scratch_shapes=scratch, compiler_params=_params("arbitrary"),
        name=f"layer_prompt_{mixer}",
    )(x, mod, gains, fgain, *mix_args, *ffn)


def _layer_sample_kernel(mixer, last, nb, steps, *refs):
    refs = list(refs)
    x_ref, mod_ref, gains_ref, fg_ref = refs[:4]
    refs = refs[4:]
    if mixer == "pool":
        pw_ref, ps_ref, ppast_ref = refs[:3]
        refs = refs[3:]
    else:
        a_ref, wp_ref = refs[:2]
        refs = refs[2:]
    wg_ref, wu_ref, wd_ref, cw_ref, cb_ref, cpast_ref = refs[:6]
    refs = refs[6:]
    if mixer == "pool":
        y_ref, cs_ref, pst_ref, h2_ref, acc_ref = refs
    else:
        y_ref, cs_ref, h2_ref, acc_ref = refs
    rows = steps * nb

    x = x_ref[...]
    gains = gains_ref[...]
    if mixer == "pool":
        h = _norm_mod(x, gains[0:1], mod_ref[0], mod_ref[1], nb)
        new = [h[t * nb:(t + 1) * nb, :] for t in range(steps)]

        def u_rows(p, cols):
            if p < POOL_STATE_ROWS:
                return ppast_ref[p, :, cols]
            return new[p - POOL_STATE_ROWS][:, cols]

        parts = []
        for g, w in enumerate(POOL_WINDOWS):
            cols = slice(g * POOL_GROUP_DIM, (g + 1) * POOL_GROUP_DIM)
            ds = []
            for t in range(steps):
                s = u_rows(POOL_STATE_ROWS + t, cols)
                for k in range(1, w):
                    s = s + u_rows(POOL_STATE_ROWS + t - k, cols)
                ds.append(s * (1.0 / w) - new[t][:, cols])
            parts.append(_dot(jnp.concatenate(ds, axis=0).astype(BF16), pw_ref[g]))
        mix = jnp.concatenate(parts, axis=-1) * ps_ref[...]
        full = slice(0, D_MODEL)
        for p in range(POOL_STATE_ROWS):
            pst_ref[p] = u_rows(p + steps, full)
    else:
        mix = _dot(a_ref[...], wp_ref[...])

    x1 = x + _bcast_rows(mod_ref[2], mix, nb)
    y_ref[...] = x1
    h2_ref[...] = _norm_mod(x1, gains[1:2], mod_ref[3], mod_ref[4], nb).astype(BF16)
    acc_ref[...] = jnp.zeros_like(acc_ref)
    past_rows = (CONV_WIDTH - 1) * nb

    def chunk(j, carry):
        h2 = h2_ref[...]
        g = _dot(h2, wg_ref[j])
        u = _dot(h2, wu_ref[j])
        gall = jnp.concatenate([cpast_ref[j], g], axis=0)
        a = _ffn_chunk_math(g, gall[nb:nb + rows, :], gall[0:rows, :], u, cw_ref[j], cb_ref[j])
        acc_ref[...] += _dot(a, wd_ref[j])
        cs_ref[j] = gall[rows:rows + past_rows, :]
        return carry

    lax.fori_loop(0, N_FF_CHUNKS, chunk, 0)
    xo = y_ref[...] + _bcast_rows(mod_ref[5], acc_ref[...], nb)
    if last:
        xo = _rms(xo) * fg_ref[...]
    y_ref[...] = xo


def _layer_sample(mixer, last, x, mod, gains, fgain, mix_args, ffn, conv_past):
    rows = x.shape[0]
    nb = mod.shape[1]
    steps = rows // nb
    args = [x, mod, gains, fgain, *mix_args, *ffn, conv_past]
    out_shape = [jax.ShapeDtypeStruct((rows, D_MODEL), F32),
                 jax.ShapeDtypeStruct(conv_past.shape, F32)]
    if mixer == "pool":
        out_shape.append(jax.ShapeDtypeStruct((POOL_STATE_ROWS, nb, D_MODEL), F32))
    return pl.pallas_call(
        functools.partial(_layer_sample_kernel, mixer, last, nb, steps),
        grid=(1,),
        in_specs=[_resident(a.shape) for a in args],
        out_specs=[pl.BlockSpec(s.shape, functools.partial(lambda nd, i: (0,) * nd, len(s.shape))) for s in out_shape],
        out_shape=out_shape,
        scratch_shapes=[pltpu.VMEM((rows, D_MODEL), BF16), pltpu.VMEM((rows, D_MODEL), F32)],
        compiler_params=_params("arbitrary"),
        name=f"layer_sample_{mixer}",
    )(*args)


def _qkv_prompt_kernel(seq, x_ref, mod_ref, gains_ref, w_ref, qkvp_ref, kvn_ref, h_ref, slab_ref):
    g = pl.program_id(1)

    @pl.when(g == 0)
    def _():
        m = mod_ref[0]
        h_ref[...] = _norm_mod(x_ref[0], gains_ref[0:1, :], m[0:1], m[1:2], None).astype(BF16)

    h = h_ref[...]
    for c in range(3):
        r = _dot(h, w_ref[0, :, c * GROUP_WIDTH:(c + 1) * GROUP_WIDTH])
        if c == 0:
            r = r * (HEAD_DIM ** -0.5)
        else:
            kvn_ref[0, 0, :, (c - 1) * GROUP_WIDTH:c * GROUP_WIDTH] = r
        slab_ref[2 * c] = r[:, 0:LANES]
        slab_ref[2 * c + 1] = r[:, LANES:2 * LANES]

    for gi, (_, dil) in enumerate(DILATED_GROUPS):
        @pl.when(g == gi)
        def _(dil=dil):
            per_class = seq // dil
            for s in range(6):
                for rho in range(dil):
                    if dil == 1:
                        v = slab_ref[s]
                    else:
                        v = slab_ref[s, pl.ds(rho, per_class, stride=dil), :]
                    qkvp_ref[0, 0, rho * per_class:(rho + 1) * per_class, s * LANES:(s + 1) * LANES] = v.astype(BF16)


def _qkv_prompt(x3, mod, gains, w3):
    batch, seq, _ = x3.shape
    width = 3 * GROUP_WIDTH
    return pl.pallas_call(
        functools.partial(_qkv_prompt_kernel, seq),
        grid=(batch, N_GROUPS),
        in_specs=[pl.BlockSpec((1, seq, D_MODEL), lambda b, g: (b, 0, 0)),
                  pl.BlockSpec((1, N_MOD, D_MODEL), lambda b, g: (b, 0, 0)),
                  pl.BlockSpec((2, D_MODEL), lambda b, g: (0, 0)),
                  pl.BlockSpec((1, D_MODEL, width), lambda b, g: (g, 0, 0))],
        out_specs=[pl.BlockSpec((1, 1, seq, width), lambda b, g: (b, g, 0, 0)),
                   pl.BlockSpec((1, 1, seq, 2 * GROUP_WIDTH), lambda b, g: (b, g, 0, 0))],
        out_shape=[jax.ShapeDtypeStruct((batch, N_GROUPS, seq, width), BF16),
                   jax.ShapeDtypeStruct((batch, N_GROUPS, seq, 2 * GROUP_WIDTH), F32)],
        scratch_shapes=[pltpu.VMEM((seq, D_MODEL), BF16), pltpu.VMEM((6, seq, LANES), F32)],
        compiler_params=_params("arbitrary", "arbitrary"),
        name="qkv_prompt",
    )(x3, mod, gains, w3)


def _head_lane_mask(rows, h):
    lane = lax.broadcasted_iota(jnp.int32, (rows, GROUP_WIDTH), 1)
    return (lane >= h * HEAD_DIM) & (lane < (h + 1) * HEAD_DIM)


def _attn_block(q, k, v, bias_ref, first):
    o = jnp.zeros((QUERY_BLOCK, GROUP_WIDTH), F32)
    lse = jnp.zeros((QUERY_BLOCK, GROUP_WIDTH), F32)
    for h in range(HEADS_PER_GROUP):
        hm = _head_lane_mask(QUERY_BLOCK, h)
        qh = jnp.where(hm, q, jnp.zeros_like(q))
        bias = bias_ref[h, :, QUERY_BLOCK:2 * QUERY_BLOCK] if first else bias_ref[h]
        s = _dot_nt(qh, k) + bias
        m = jnp.max(s, axis=-1, keepdims=True)
        p = jnp.exp(s - m)
        l = jnp.sum(p, axis=-1, keepdims=True)
        pv = _dot(p.astype(BF16), v)
        o = jnp.where(hm, pv * (1.0 / l), o)
        lse = jnp.where(hm, m + jnp.log(l), lse)
    return o, lse


def _attn_prompt_kernel(seq, qkv_ref, brow_ref, o_ref, os_ref, ls_ref, bias_ref):
    @pl.when(pl.program_id(0) == 0)
    def _():
        for g in range(N_GROUPS):
            for h in range(HEADS_PER_GROUP):
                base = jnp.broadcast_to(brow_ref[g, h], (QUERY_BLOCK, 2 * QUERY_BLOCK))
                bias_ref[g, h] = pltpu.roll(base, 0, 1, stride=1, stride_axis=0)

    qc = slice(0, GROUP_WIDTH)
    kc = slice(GROUP_WIDTH, 2 * GROUP_WIDTH)
    vc = slice(2 * GROUP_WIDTH, 3 * GROUP_WIDTH)
    for g, (_, dil) in enumerate(DILATED_GROUPS):
        per_class = seq // dil
        blocks_per_class = per_class // QUERY_BLOCK

        def do_block(rho, blk, first, g=g, dil=dil, per_class=per_class):
            r0 = pl.multiple_of(rho * per_class + blk * QUERY_BLOCK, QUERY_BLOCK)
            q = qkv_ref[0, g, pl.ds(r0, QUERY_BLOCK), qc]
            if first:
                k = qkv_ref[0, g, pl.ds(r0, QUERY_BLOCK), kc]
                v = qkv_ref[0, g, pl.ds(r0, QUERY_BLOCK), vc]
                o, lse = _attn_block(q, k, v, bias_ref.at[g], True)
            else:
                rk = pl.multiple_of(r0 - QUERY_BLOCK, QUERY_BLOCK)
                k = qkv_ref[0, g, pl.ds(rk, 2 * QUERY_BLOCK), kc]
                v = qkv_ref[0, g, pl.ds(rk, 2 * QUERY_BLOCK), vc]
                o, lse = _attn_block(q, k, v, bias_ref.at[g], False)
            start = blk * QUERY_BLOCK * dil + rho
            for s in range(2):
                cols = slice(s * LANES, (s + 1) * LANES)
                if dil == 1:
                    os_ref[g, s, pl.ds(r0, QUERY_BLOCK), :] = o[:, cols]
                    ls_ref[g, s, pl.ds(r0, QUERY_BLOCK), :] = lse[:, cols]
                else:
                    os_ref[g, s, pl.ds(start, QUERY_BLOCK, stride=dil), :] = o[:, cols]
                    ls_ref[g, s, pl.ds(start, QUERY_BLOCK, stride=dil), :] = lse[:, cols]

        def class_body(rho, carry, do_block=do_block, blocks_per_class=blocks_per_class):
            do_block(rho, 0, True)
            if blocks_per_class > 1:
                def inner(blk, c):
                    do_block(rho, blk, False)
                    return c
                lax.fori_loop(1, blocks_per_class, inner, 0)
            return carry

        lax.fori_loop(0, dil, class_body, 0)

    def merge(i, carry):
        r0 = pl.multiple_of(i * QUERY_BLOCK, QUERY_BLOCK)
        rows = pl.ds(r0, QUERY_BLOCK)
        for s in range(2):
            ls = [ls_ref[g, s, rows, :] for g in range(N_GROUPS)]
            mx = jnp.maximum(jnp.maximum(ls[0], ls[1]), ls[2])
            es = [jnp.exp(l - mx) for l in ls]
            inv = 1.0 / (es[0] + es[1] + es[2])
            for g in range(N_GROUPS):
                c0 = g * GROUP_WIDTH + s * LANES
                o_ref[0, rows, c0:c0 + LANES] = (os_ref[g, s, rows, :] * (es[g] * inv)).astype(BF16)
        return carry

    lax.fori_loop(0, seq // QUERY_BLOCK, merge, 0)


def _attn_prompt(qkvp, bias_rows):
    batch, _, seq, width = qkvp.shape
    return pl.pallas_call(
        functools.partial(_attn_prompt_kernel, seq),
        grid=(batch,),
        in_specs=[pl.BlockSpec((1, N_GROUPS, seq, width), lambda b: (b, 0, 0, 0)),
                  _resident(bias_rows.shape)],
        out_specs=pl.BlockSpec((1, seq, ATTN_INNER), lambda b: (b, 0, 0)),
        out_shape=jax.ShapeDtypeStruct((batch, seq, ATTN_INNER), BF16),
        scratch_shapes=[pltpu.VMEM((N_GROUPS, 2, seq, LANES), F32), pltpu.VMEM((N_GROUPS, 2, seq, LANES), F32),
                        pltpu.VMEM((N_GROUPS, HEADS_PER_GROUP, QUERY_BLOCK, 2 * QUERY_BLOCK), F32)],
        compiler_params=_params("arbitrary"),
        name="attn_prompt",
    )(qkvp, bias_rows)


def _proj_sample_kernel(nb, x_ref, mod_ref, gains_ref, w_ref, o_ref):
    h = _norm_mod(x_ref[...], gains_ref[0:1, :], mod_ref[0], mod_ref[1], nb).astype(BF16)
    o_ref[...] = _dot(h, w_ref[...])


def _proj_sample(x, mod, gains, w):
    rows = x.shape[0]
    nb = mod.shape[1]
    args = [x, mod, gains, w]
    return pl.pallas_call(
        functools.partial(_proj_sample_kernel, nb),
        grid=(1,),
        in_specs=[_resident(a.shape) for a in args],
        out_specs=pl.BlockSpec((rows, w.shape[1]), lambda i: (0, 0)),
        out_shape=jax.ShapeDtypeStruct((rows, w.shape[1]), F32),
        compiler_params=_params("arbitrary"),
        name="qkv_sample",
    )(*args)


def _split3_bf16(a):
    hi = a.astype(BF16)
    r1 = a - hi.astype(F32)
    mid = r1.astype(BF16)
    lo = (r1 - mid.astype(F32)).astype(BF16)
    return hi, mid, lo


def _attn_sample_kernel(steps, q_ref, kvn_ref, c1_ref, c2_ref, c3_ref, t1_ref, t2_ref, t3_ref,
                        o_ref, n1_ref, n2_ref, n3_ref):
    lane = lax.broadcasted_iota(jnp.int32, (SUBLANES, LANES), 1)
    row = lax.broadcasted_iota(jnp.int32, (SUBLANES, LANES), 0)
    sel_head = jnp.where((lane == row) & (row < steps), 1.0, 0.0).astype(BF16)
    sel_tail = jnp.where((lane == row + (LANES - steps)) & (row < steps), 1.0, 0.0).astype(BF16)
    tail_lanes = lax.broadcasted_iota(jnp.int32, (HEAD_DIM, LANES), 1) >= LANES - steps
    outs, lses = [], []
    groups = ((c1_ref, t1_ref, n1_ref), (c2_ref, t2_ref, n2_ref), (c3_ref, t3_ref, n3_ref))
    for g, (c_ref, t_ref, n_ref) in enumerate(groups):
        p_rows = c_ref.shape[-1]
        pieces = _split3_bf16(kvn_ref[0, g])
        new_head = sum(_dot_tn(x, sel_head) for x in pieces)
        new_tail = sum(_dot_tn(x, sel_tail) for x in pieces)
        o_heads, l_heads = [], []
        for h in range(HEADS_PER_GROUP):
            ki, vi = h, HEADS_PER_GROUP + h
            kt = c_ref[0, ki]
            vt = c_ref[0, vi]
            k_all = jnp.concatenate([kt.astype(BF16), new_head[ki * HEAD_DIM:(ki + 1) * HEAD_DIM].astype(BF16)], axis=1)
            v_all = jnp.concatenate([vt.astype(BF16), new_head[vi * HEAD_DIM:(vi + 1) * HEAD_DIM].astype(BF16)], axis=1)
            qh = (q_ref[0, g, h] * (HEAD_DIM ** -0.5)).astype(BF16)
            s = _dot(qh, k_all) + t_ref[h]
            m = jnp.max(s, axis=-1, keepdims=True)
            p = jnp.exp(s - m)
            l = jnp.sum(p, axis=-1, keepdims=True)
            o_heads.append(_dot_nt(p.astype(BF16), v_all) * (1.0 / l))
            l_heads.append(m + jnp.log(l))
            for idx, src in ((ki, kt), (vi, vt)):
                shifted = pltpu.roll(src, p_rows - steps, 1)
                fresh = new_tail[idx * HEAD_DIM:(idx + 1) * HEAD_DIM]
                if p_rows > LANES:
                    n_ref[0, idx, :, 0:p_rows - LANES] = shifted[:, 0:p_rows - LANES]
                n_ref[0, idx, :, p_rows - LANES:p_rows] = jnp.where(tail_lanes, fresh, shifted[:, p_rows - LANES:p_rows])
        outs.append(o_heads)
        lses.append(l_heads)
    for h in range(HEADS_PER_GROUP):
        ls = [lses[g][h] for g in range(N_GROUPS)]
        mx = jnp.maximum(jnp.maximum(ls[0], ls[1]), ls[2])
        es = [jnp.exp(l - mx) for l in ls]
        inv = 1.0 / (es[0] + es[1] + es[2])
        for g in range(N_GROUPS):
            c0 = g * GROUP_WIDTH + h * HEAD_DIM
            o_ref[0, :, c0:c0 + HEAD_DIM] = outs[g][h] * (es[g] * inv)


def _attn_sample(steps, q, kv_new, caches, tables):
    nb = q.shape[0]
    in_specs = [pl.BlockSpec((1,) + q.shape[1:], lambda b: (b, 0, 0, 0, 0)),
                pl.BlockSpec((1,) + kv_new.shape[1:], lambda b: (b, 0, 0, 0))]
    cache_specs = [pl.BlockSpec((1,) + c.shape[1:], lambda b: (b, 0, 0, 0)) for c in caches]
    in_specs += cache_specs + [_resident(t.shape) for t in tables]
    return pl.pallas_call(
        functools.partial(_attn_sample_kernel, steps),
        grid=(nb,), in_specs=in_specs,
        out_specs=[pl.BlockSpec((1, SUBLANES, ATTN_INNER), lambda b: (b, 0, 0))] + cache_specs,
        out_shape=[jax.ShapeDtypeStruct((nb, SUBLANES, ATTN_INNER), F32)]
                  + [jax.ShapeDtypeStruct(c.shape, F32) for c in caches],
        compiler_params=_params("arbitrary"),
        name="attn_sample",
    )(q, kv_new, *caches, *tables)


def _gla_proj_kernel(nb, x_ref, mod_ref, gains_ref, w_ref, wgd_ref, wgu_ref, bg_ref,
                     q_ref, k_ref, v_ref, r_ref, la_ref):
    if nb is None:
        m = mod_ref[0]
        shift, scale = m[0:1], m[1:2]
    else:
        shift, scale = mod_ref[0], mod_ref[1]
    h = _norm_mod(x_ref[...], gains_ref[0:1, :], shift, scale, nb).astype(BF16)
    q_ref[...] = _dot(h, w_ref[:, 0:GLA_QK]) * (GLA_DK ** -0.5)
    k_ref[...] = _dot(h, w_ref[:, GLA_QK:2 * GLA_QK])
    v_ref[...] = _dot(h, w_ref[:, 2 * GLA_QK:2 * GLA_QK + GLA_V]).astype(BF16)
    r_ref[...] = _dot(h, w_ref[:, 2 * GLA_QK + GLA_V:2 * GLA_QK + 2 * GLA_V])
    gd = _dot(h, wgd_ref[...])
    gate = _dot(gd.astype(BF16), wgu_ref[...]) + bg_ref[...]
    la_ref[...] = jax.nn.log_sigmoid(gate) * (1.0 / GATE_TAU)


def _gla_proj(x, mod, gains, weights, nb, tm):
    rows = x.shape[0]
    if nb is None:
        tps = rows // mod.shape[0] // tm
        mod_spec = pl.BlockSpec((1, N_MOD, D_MODEL), lambda i: (i // tps, 0, 0))
    else:
        mod_spec = _resident(mod.shape)
    widths = (GLA_QK, GLA_QK, GLA_V, GLA_V, GLA_QK)
    dtypes = (F32, F32, BF16, F32, F32)
    return pl.pallas_call(
        functools.partial(_gla_proj_kernel, nb),
        grid=(rows // tm,),
        in_specs=[pl.BlockSpec((tm, D_MODEL), lambda i: (i, 0)), mod_spec, _resident((2, D_MODEL))]
                 + [_resident(w.shape) for w in weights],
        out_specs=[pl.BlockSpec((tm, w), lambda i: (i, 0)) for w in widths],
        out_shape=[jax.ShapeDtypeStruct((rows, w), dt) for w, dt in zip(widths, dtypes)],
        compiler_params=_params("arbitrary"),
        name="gla_proj",
    )(x, mod, gains, *weights)


def _cumsum_rows(g):
    rows = g.shape[0]
    row = lax.broadcasted_iota(jnp.int32, g.shape, 0)
    b = g
    shift = 1
    while shift < rows:
        b = b + jnp.where(row >= shift, pltpu.roll(b, shift, 0), 0.0)
        shift *= 2
    return b


def _gla_chunk(q, k, v, g, r, gain, mid, get_state, set_state):
    c = q.shape[0]
    b = _cumsum_rows(g)
    b_end = b[c - 1:c, :]
    b_mid = b[mid:mid + 1, :]
    q_in = (q * jnp.exp(b)).astype(BF16)
    q_rel = (q * jnp.exp(b - b_mid)).astype(BF16)
    k_rel = (k * jnp.exp(b_mid - b)).astype(BF16)
    k_out = (k * jnp.exp(b_end - b)).astype(BF16)
    decay_end = jnp.broadcast_to(jnp.exp(b_end), (SUBLANES, GLA_QK))
    ti = lax.broadcasted_iota(jnp.int32, (c, c), 0)
    si = lax.broadcasted_iota(jnp.int32, (c, c), 1)
    outs = []
    for h in range(GLA_HEADS):
        ks = slice(h * GLA_DK, (h + 1) * GLA_DK)
        vs = slice(h * GLA_DV, (h + 1) * GLA_DV)
        state = get_state(h)
        o = _dot(q_in[:, ks], state.astype(BF16))
        att = jnp.where(si <= ti, _dot_nt(q_rel[:, ks], k_rel[:, ks]), 0.0)
        o = o + _dot(att.astype(BF16), v[:, vs])
        decay_col = decay_end[:, ks].T[:, 0:1]
        set_state(h, decay_col * state + _dot_tn(k_out[:, ks], v[:, vs]))
        outs.append(_rms(o) * gain[:, vs] * _silu(r[:, vs]))
    return jnp.concatenate(outs, axis=-1)


def _gla_prompt_kernel(tt, q_ref, k_ref, v_ref, r_ref, la_ref, gain_ref, a_ref, so_ref, s_ref):
    t = pl.program_id(1)

    @pl.when(t == 0)
    def _():
        s_ref[...] = jnp.zeros_like(s_ref)

    gain = gain_ref[...]

    def get_state(h):
        return s_ref[h]

    def set_state(h, val):
        s_ref[h] = val

    def chunk(ci, carry):
        rows = pl.ds(pl.multiple_of(ci * GLA_CHUNK, GLA_CHUNK), GLA_CHUNK)
        a = _gla_chunk(q_ref[0, rows, :], k_ref[0, rows, :], v_ref[0, rows, :], la_ref[0, rows, :],
                       r_ref[0, rows, :], gain, GLA_CHUNK // 2, get_state, set_state)
        a_ref[0, rows, :] = a.astype(BF16)
        return carry

    lax.fori_loop(0, tt // GLA_CHUNK, chunk, 0)

    @pl.when(t == pl.num_programs(1) - 1)
    def _():
        so_ref[0] = s_ref[...]


def _gla_prompt(q, k, v, r, la, gain):
    batch, seq, _ = q.shape
    tt = GLA_TIME_TILE

    def spec(width):
        return pl.BlockSpec((1, tt, width), lambda b, t: (b, t, 0))

    return pl.pallas_call(
        functools.partial(_gla_prompt_kernel, tt),
        grid=(batch, seq // tt),
        in_specs=[spec(GLA_QK), spec(GLA_QK), spec(GLA_V), spec(GLA_V), spec(GLA_QK),
                  pl.BlockSpec((1, GLA_V), lambda b, t: (0, 0))],
        out_specs=[spec(GLA_V), pl.BlockSpec((1, GLA_HEADS, GLA_DK, GLA_DV), lambda b, t: (b, 0, 0, 0))],
        out_shape=[jax.ShapeDtypeStruct((batch, seq, GLA_V), BF16),
                   jax.ShapeDtypeStruct((batch, GLA_HEADS, GLA_DK, GLA_DV), F32)],
        scratch_shapes=[pltpu.VMEM((GLA_HEADS, GLA_DK, GLA_DV), F32)],
        compiler_params=_params("arbitrary", "arbitrary"),
        name="gla_prompt",
    )(q, k, v, r, la, gain)


def _gla_sample_kernel(sb, q_ref, k_ref, v_ref, r_ref, la_ref, gain_ref, s0_ref, a_ref, so_ref):
    gain = gain_ref[...]

    def seq_body(i, carry):
        def get_state(h):
            return s0_ref[i, h]

        def set_state(h, val):
            so_ref[i, h] = val

        a_ref[i] = _gla_chunk(q_ref[i], k_ref[i], v_ref[i], la_ref[i], r_ref[i], gain, 0, get_state, set_state)
        return carry

    lax.fori_loop(0, sb, seq_body, 0)


def _gla_sample(q, k, v, r, la, gain, s0):
    nb, pad, _ = q.shape
    sb = math.gcd(nb, 8)

    def spec(width):
        return pl.BlockSpec((sb, pad, width), lambda i: (i, 0, 0))

    state_spec = pl.BlockSpec((sb, GLA_HEADS, GLA_DK, GLA_DV), lambda i: (i, 0, 0, 0))
    return pl.pallas_call(
        functools.partial(_gla_sample_kernel, sb),
        grid=(nb // sb,),
        in_specs=[spec(GLA_QK), spec(GLA_QK), spec(GLA_V), spec(GLA_V), spec(GLA_QK),
                  pl.BlockSpec((1, GLA_V), lambda i: (0, 0)), state_spec],
        out_specs=[spec(GLA_V), state_spec],
        out_shape=[jax.ShapeDtypeStruct((nb, pad, GLA_V), F32),
                   jax.ShapeDtypeStruct((nb, GLA_HEADS, GLA_DK, GLA_DV), F32)],
        compiler_params=_params("arbitrary"),
        name="gla_sample",
    )(q, k, v, r, la, gain, s0)


def _t5_bucket(dist):
    max_exact = NUM_BUCKETS // 2
    d_f = jnp.maximum(dist, 1).astype(F32)
    large = max_exact + (jnp.log(d_f / max_exact) / math.log(MAX_DISTANCE / max_exact)
                         * (NUM_BUCKETS - max_exact)).astype(jnp.int32)
    large = jnp.minimum(large, NUM_BUCKETS - 1)
    return jnp.where(dist < max_exact, dist, large)


def _group_bias(rel_bias):
    rows = []
    for g, (window, dil) in enumerate(DILATED_GROUPS):
        buckets = _t5_bucket(jnp.arange(window // dil + 1) * dil)
        rows.append(rel_bias[buckets][:, g * HEADS_PER_GROUP:(g + 1) * HEADS_PER_GROUP].T)
    return jnp.stack(rows)


def _prompt_bias_rows(gb):
    band = gb[:, :, ::-1]
    off = jnp.full(gb.shape[:2] + (2 * QUERY_BLOCK - KEYS_PER_QUERY,), NEG_INF, F32)
    return jnp.concatenate([band, off], axis=-1)[:, :, None, :]


def _sample_bias_tables(gb, steps, past_rows):
    tables = []
    t_idx = np.arange(SUBLANES)[:, None]
    c_idx = np.arange(LANES)[None, :]
    for g, (window, dil) in enumerate(DILATED_GROUPS):
        p = past_rows[g]
        assert p == window == (KEYS_PER_QUERY - 1) * dil
        b = gb[g]
        heads = b.shape[0]
        row0 = b[:, :0:-1]
        if dil > 1:
            gaps = jnp.full((heads, KEYS_PER_QUERY - 1, dil - 1), NEG_INF, F32)
            row0 = jnp.concatenate([row0[:, :, None], gaps], axis=2).reshape(heads, p)
        rows = []
        for t in range(SUBLANES):
            if t < steps:
                rows.append(jnp.concatenate([jnp.full((heads, t), NEG_INF, F32), row0[:, :p - t]], axis=1))
            else:
                rows.append(jnp.full((heads, p), NEG_INF, F32))
        past = jnp.stack(rows, axis=1)
        new = jnp.full((heads, SUBLANES, LANES), NEG_INF, F32)
        for j in range((steps - 1) // dil + 1):
            mask = (t_idx - c_idx == j * dil) & (t_idx < steps) & (c_idx < steps)
            new = jnp.where(jnp.asarray(mask)[None], b[:, j][:, None, None], new)
        tables.append(jnp.concatenate([past, new], axis=2))
    return tables


def _ffn_weights(w_in, conv_w, conv_b, w_down):
    def cols(w):
        return jnp.transpose(w.reshape(D_MODEL, N_FF_CHUNKS, FF_CHUNK), (1, 0, 2)).astype(BF16)
    return (cols(w_in[:, :D_FF]), cols(w_in[:, D_FF:]),
            w_down.reshape(N_FF_CHUNKS, FF_CHUNK, D_MODEL).astype(BF16),
            jnp.transpose(conv_w.reshape(CONV_WIDTH, N_FF_CHUNKS, FF_CHUNK), (1, 0, 2)),
            conv_b.reshape(N_FF_CHUNKS, 1, FF_CHUNK))


def _conv_tail_prompt(cs):
    batch = cs.shape[0]
    tail = cs[:, :, SUBLANES - (CONV_WIDTH - 1):, :]
    return jnp.transpose(tail, (0, 2, 1, 3)).reshape(batch, CONV_WIDTH - 1, D_FF)


def _conv_past_sample(state):
    nb = state.shape[0]
    s = state.reshape(nb, CONV_WIDTH - 1, N_FF_CHUNKS, FF_CHUNK)
    return jnp.transpose(s, (2, 1, 0, 3)).reshape(N_FF_CHUNKS, (CONV_WIDTH - 1) * nb, FF_CHUNK)


def _conv_tail_sample(cs, nb):
    s = cs.reshape(N_FF_CHUNKS, CONV_WIDTH - 1, nb, FF_CHUNK)
    return jnp.transpose(s, (2, 1, 0, 3)).reshape(nb, CONV_WIDTH - 1, D_FF)


def kernel(x_prompt, x_sample, state_pool, cache_win_g1, cache_win_g2, cache_win_g3, state_gla, state_ffn_conv,
           c_prompt, c_sample, w_ada, b_ada, norm_gain, final_gain, rel_bias, pool_w, pool_scale,
           attn_w_in, attn_w_out, gla_w_in, gla_w_gate_up, gla_b_gate, gla_norm_gain, gla_w_out,
           ffn_w_in, ffn_conv_w, ffn_conv_b, ffn_w_down):
    batch, seq, _ = x_prompt.shape
    nb, steps, _ = x_sample.shape
    caches = (cache_win_g1, cache_win_g2, cache_win_g3)

    mods = _modulation(jnp.concatenate([c_prompt, c_sample], axis=0), w_ada, b_ada)
    mod_p = mods[:, :batch].reshape(DEPTH, batch, N_MOD, D_MODEL)
    mod_s = jnp.transpose(mods[:, batch:].reshape(DEPTH, nb, N_MOD, D_MODEL), (0, 2, 1, 3))
    fgain = final_gain.reshape(1, D_MODEL)

    xp = x_prompt.reshape(batch * seq, D_MODEL)
    xs = jnp.transpose(x_sample, (1, 0, 2)).reshape(steps * nb, D_MODEL)

    pool_p, pool_s, gla_p, gla_s, conv_p, conv_s = [], [], [], [], [], []
    win_p, win_s = None, None

    for i in range(DEPTH):
        kind, j = i % 3, i // 3
        last = i == DEPTH - 1
        ffn = _ffn_weights(ffn_w_in[i], ffn_conv_w[i], ffn_conv_b[i], ffn_w_down[i])
        conv_past = _conv_past_sample(state_ffn_conv[i])
        gains = norm_gain[i]
        if kind == 0:
            mix_w = (pool_w[j].astype(BF16), pool_scale[j].reshape(1, D_MODEL))
            xp, cs, pst = _layer_prompt("pool", last, xp, mod_p[i], gains, fgain, mix_w, ffn)
            pool_p.append(pst[:, POOL_CARRY_ROWS - POOL_STATE_ROWS:])
            past = jnp.transpose(state_pool[j], (1, 0, 2))
            xs, css, psts = _layer_sample("pool", last, xs, mod_s[i], gains, fgain, mix_w + (past,), ffn, conv_past)
            pool_s.append(jnp.transpose(psts, (1, 0, 2)))
        elif kind == 1:
            w = attn_w_in[j]
            w3 = jnp.stack([jnp.concatenate([w[:, s * ATTN_INNER + g * GROUP_WIDTH:
                                                s * ATTN_INNER + (g + 1) * GROUP_WIDTH] for s in range(3)], axis=1)
                            for g in range(N_GROUPS)]).astype(BF16)
            gb = _group_bias(rel_bias)
            wo = attn_w_out[j].astype(BF16)
            qkvp, kvn = _qkv_prompt(xp.reshape(batch, seq, D_MODEL), mod_p[i], gains, w3)
            o_all = _attn_prompt(qkvp, _prompt_bias_rows(gb))
            win_p = [kvn[:, g, seq - min(window, seq):].reshape(1, batch, min(window, seq), 2, HEADS_PER_GROUP, HEAD_DIM)
                     for g, (window, _) in enumerate(DILATED_GROUPS)]
            xp, cs = _layer_prompt("proj", last, xp, mod_p[i], gains, fgain,
                                   (o_all.reshape(batch * seq, ATTN_INNER), wo), ffn)
            qkv_s = _proj_sample(xs, mod_s[i], gains, w.astype(BF16))
            q6 = qkv_s.reshape(steps, nb, 3, N_GROUPS, HEADS_PER_GROUP, HEAD_DIM)
            q_s = jnp.pad(jnp.transpose(q6[:, :, 0], (1, 2, 3, 0, 4)),
                          ((0, 0), (0, 0), (0, 0), (0, SUBLANES - steps), (0, 0)))
            kv_new = jnp.transpose(q6[:, :, 1:], (1, 3, 0, 2, 4, 5)).reshape(nb, N_GROUPS, steps, 2 * GROUP_WIDTH)
            kv_new = jnp.pad(kv_new, ((0, 0), (0, 0), (0, SUBLANES - steps), (0, 0)))
            past_rows = [c.shape[2] for c in caches]
            cache_t = [jnp.transpose(c[j], (0, 2, 3, 4, 1)).reshape(nb, 2 * HEADS_PER_GROUP, HEAD_DIM, p)
                       for c, p in zip(caches, past_rows)]
            o8, *new_caches = _attn_sample(steps, q_s, kv_new, cache_t, _sample_bias_tables(gb, steps, past_rows))
            a_s = jnp.transpose(o8[:, :steps], (1, 0, 2)).reshape(steps * nb, ATTN_INNER)
            win_s = [jnp.transpose(c.reshape(nb, 2, HEADS_PER_GROUP, HEAD_DIM, p), (0, 4, 1, 2, 3))[None]
                     for c, p in zip(new_caches, past_rows)]
            xs, css = _layer_sample("proj", last, xs, mod_s[i], gains, fgain, (a_s.astype(BF16), wo), ffn, conv_past)
        else:
            w = gla_w_in[j]
            n_main = 2 * GLA_QK + 2 * GLA_V
            weights = (w[:, :n_main].astype(BF16),
                       jnp.pad(w[:, n_main:], ((0, 0), (0, LANES - GATE_RANK))).astype(BF16),
                       jnp.pad(gla_w_gate_up[j], ((0, LANES - GATE_RANK), (0, 0))).astype(BF16),
                       gla_b_gate[j].reshape(1, GLA_QK))
            gain = gla_norm_gain[j].reshape(1, GLA_V)
            wo = gla_w_out[j].astype(BF16)
            q, k, v, r, la = _gla_proj(xp, mod_p[i], gains, weights, None, PROMPT_ROW_TILE)
            shp = lambda a: a.reshape(batch, seq, a.shape[-1])
            a_p, s_p = _gla_prompt(shp(q), shp(k), shp(v), shp(r), shp(la), gain)
            gla_p.append(s_p)
            xp, cs = _layer_prompt("proj", last, xp, mod_p[i], gains, fgain,
                                   (a_p.reshape(batch * seq, GLA_V), wo), ffn)
            outs = _gla_proj(xs, mod_s[i], gains, weights, nb, steps * nb)

            def per_seq(a):
                a = jnp.transpose(a.reshape(steps, nb, a.shape[-1]), (1, 0, 2))
                return jnp.pad(a, ((0, 0), (0, SAMPLE_DEC_PAD - steps), (0, 0)))

            qs, ks, vs, rs, las = (per_seq(a) for a in outs)
            a16, s_s = _gla_sample(qs, ks, vs, rs, las, gain, state_gla[j])
            gla_s.append(s_s)
            a_s = jnp.transpose(a16[:, :steps], (1, 0, 2)).reshape(steps * nb, GLA_V).astype(BF16)
            xs, css = _layer_sample("proj", last, xs, mod_s[i], gains, fgain, (a_s, wo), ffn, conv_past)
        conv_p.append(_conv_tail_prompt(cs))
        conv_s.append(_conv_tail_sample(css, nb))

    y_prompt = xp.reshape(batch, seq, D_MODEL)
    y_sample = jnp.transpose(xs.reshape(steps, nb, D_MODEL), (1, 0, 2))
    return (y_prompt, y_sample, jnp.stack(pool_p), jnp.stack(pool_s),
            win_p[0], win_s[0], win_p[1], win_s[1], win_p[2], win_s[2],
            jnp.stack(gla_p), jnp.stack(gla_s), jnp.stack(conv_p), jnp.stack(conv_s))
```

```python
import functools
import math

import numpy as np
import jax
import jax.numpy as jnp
from jax import lax
from jax.experimental import pallas as pl
from jax.experimental.pallas import tpu as pltpu

F32 = jnp.float32
BF16 = jnp.bfloat16

D_MODEL = 1024
DEPTH = 4
N_MOD = 6
EPS = 1e-6
NEG_INF = -1e30
POOL_WINDOWS = (2, 4, 8, 16)
POOL_GROUP_DIM = D_MODEL // len(POOL_WINDOWS)
POOL_STATE_ROWS = max(POOL_WINDOWS) - 1
POOL_CARRY_ROWS = 16
DILATED_GROUPS = ((128, 1), (512, 4), (2048, 16))
N_GROUPS = len(DILATED_GROUPS)
HEADS_PER_GROUP = 4
HEAD_DIM = 64
GROUP_WIDTH = HEADS_PER_GROUP * HEAD_DIM
ATTN_INNER = N_GROUPS * GROUP_WIDTH
KEYS_PER_QUERY = 129
QUERY_BLOCK = 128
NUM_BUCKETS = 32
MAX_DISTANCE = 2048
GLA_HEADS = 4
GLA_DK = 128
GLA_DV = 256
GLA_QK = GLA_HEADS * GLA_DK
GLA_V = GLA_HEADS * GLA_DV
GATE_RANK = 16
GATE_TAU = 16.0
GLA_CHUNK = 64
D_FF = 2816
CONV_WIDTH = 3

LANES = 128
SUBLANES = 8
FF_CHUNK = 256
N_FF_CHUNKS = D_FF // FF_CHUNK
VMEM_LIMIT_BYTES = 56 * 1024 * 1024
PROMPT_ROW_TILE = 1024
GLA_TIME_TILE = 512
ATTN_BLOCK_UNROLL = 4
GLA_CHUNK_UNROLL = 4
SAMPLE_DEC_PAD = 16


def _params(*semantics):
    return pltpu.CompilerParams(dimension_semantics=semantics, vmem_limit_bytes=VMEM_LIMIT_BYTES)


def _resident(shape):
    nd = len(shape)
    return pl.BlockSpec(shape, lambda *_: (0,) * nd, pipeline_mode=pl.Buffered(1))


def _dot(a, b):
    return jnp.dot(a, b, preferred_element_type=F32)


def _dot_nt(a, b):
    return lax.dot_general(a, b, (((1,), (1,)), ((), ())), preferred_element_type=F32)


def _dot_tn(a, b):
    return lax.dot_general(a, b, (((0,), (0,)), ((), ())), preferred_element_type=F32)


def _rms(x):
    return x * lax.rsqrt(jnp.mean(x * x, axis=-1, keepdims=True) + EPS)


def _bcast_rows(v, y, nb):
    if nb is None:
        return v * y
    rows, width = y.shape
    return (y.reshape(rows // nb, nb, width) * v[None]).reshape(rows, width)


def _norm_mod(x, gain, shift, scale, nb):
    y = _rms(x) * gain
    if nb is None:
        return y * (1.0 + scale) + shift
    rows, width = y.shape
    y3 = y.reshape(rows // nb, nb, width)
    return (y3 * (1.0 + scale)[None] + shift[None]).reshape(rows, width)


def _gelu(x):
    return 0.5 * x * (1.0 + lax.erf(x * (1.0 / math.sqrt(2.0))))


def _silu(x):
    return x * jax.nn.sigmoid(x)


def _split_bf16(a):
    hi = a.astype(BF16)
    lo = (a - hi.astype(F32)).astype(BF16)
    return hi, lo


def _mod_kernel(c_ref, w_ref, b_ref, o_ref):
    a_hi, a_lo = _split_bf16(_silu(c_ref[...]))
    w_hi, w_lo = _split_bf16(w_ref[0])
    o_ref[0] = _dot(a_hi, w_hi) + _dot(a_lo, w_hi) + _dot(a_hi, w_lo) + b_ref[0]


def _modulation(c_all, w_ada, b_ada):
    rows = c_all.shape[0]
    width = N_MOD * D_MODEL
    tn = 1536
    return pl.pallas_call(
        _mod_kernel,
        grid=(DEPTH, width // tn),
        in_specs=[pl.BlockSpec((rows, D_MODEL), lambda l, n: (0, 0)),
                  pl.BlockSpec((1, D_MODEL, tn), lambda l, n: (l, 0, n)),
                  pl.BlockSpec((1, 1, tn), lambda l, n: (l, 0, n))],
        out_specs=pl.BlockSpec((1, rows, tn), lambda l, n: (l, 0, n)),
        out_shape=jax.ShapeDtypeStruct((DEPTH, rows, width), F32),
        compiler_params=_params("parallel", "parallel"),
        name="adaln_mod",
    )(c_all, w_ada, b_ada.reshape(DEPTH, 1, width))


def _ffn_chunk_math(g, g_m1, g_m2, u, cw, cb):
    gc = cw[2:3] * g + cw[1:2] * g_m1 + cw[0:1] * g_m2 + cb
    return (_gelu(gc) * u).astype(BF16)


def _layer_prompt_kernel(mixer, last, tm, tiles_per_seq, *refs):
    refs = list(refs)
    x_ref, mod_ref, gains_ref, fg_ref = refs[:4]
    refs = refs[4:]
    if mixer == "pool":
        pw_ref, ps_ref = refs[:2]
    else:
        a_ref, wp_ref = refs[:2]
    wg_ref, wu_ref, wd_ref, cw_ref, cb_ref = refs[2:7]
    refs = refs[7:]
    if mixer == "pool":
        y_ref, cs_ref, pst_ref, h2_ref, acc_ref, gext_ref, cc_ref, hext_ref = refs
    else:
        y_ref, cs_ref, h2_ref, acc_ref, gext_ref, cc_ref = refs

    i = pl.program_id(0)
    tile_in_seq = i % tiles_per_seq

    @pl.when(tile_in_seq == 0)
    def _():
        cc_ref[...] = jnp.zeros_like(cc_ref)
        if mixer == "pool":
            hext_ref[0:POOL_CARRY_ROWS, :] = jnp.zeros((POOL_CARRY_ROWS, D_MODEL), F32)

    x = x_ref[...]
    m = mod_ref[0]
    gains = gains_ref[...]

    if mixer == "pool":
        h = _norm_mod(x, gains[0:1], m[0:1], m[1:2], None)
        hext_ref[POOL_CARRY_ROWS:, :] = h
        pos = tile_in_seq * tm + lax.broadcasted_iota(jnp.int32, (tm, 1), 0)
        parts = []
        for g, w in enumerate(POOL_WINDOWS):
            cols = slice(g * POOL_GROUP_DIM, (g + 1) * POOL_GROUP_DIM)
            s = hext_ref[pl.ds(POOL_CARRY_ROWS, tm), cols]
            for k in range(1, w):
                s = s + hext_ref[pl.ds(POOL_CARRY_ROWS - k, tm), cols]
            inv_count = 1.0 / jnp.minimum(pos + 1, w).astype(F32)
            d = s * inv_count - h[:, cols]
            parts.append(_dot(d.astype(BF16), pw_ref[g]))
        mix = jnp.concatenate(parts, axis=-1) * ps_ref[...]
        tail = hext_ref[pl.ds(tm, POOL_CARRY_ROWS), :]
        hext_ref[0:POOL_CARRY_ROWS, :] = tail
        pst_ref[0] = tail
    else:
        mix = _dot(a_ref[...], wp_ref[...])

    x1 = x + m[2:3] * mix
    y_ref[...] = x1
    h2_ref[...] = _norm_mod(x1, gains[1:2], m[3:4], m[4:5], None).astype(BF16)
    acc_ref[...] = jnp.zeros_like(acc_ref)

    for j in range(N_FF_CHUNKS):
        h2 = h2_ref[...]
        g = _dot(h2, wg_ref[j])
        u = _dot(h2, wu_ref[j])
        gx = gext_ref.at[j % 2]
        gx[0:SUBLANES, :] = cc_ref[j]
        gx[SUBLANES:, :] = g
        a = _ffn_chunk_math(g, gx[pl.ds(SUBLANES - 1, tm), :], gx[pl.ds(SUBLANES - 2, tm), :],
                            u, cw_ref[j], cb_ref[j])
        acc_ref[...] += _dot(a, wd_ref[j])
        tail = g[tm - SUBLANES:tm, :]
        cc_ref[j] = tail
        cs_ref[0, j] = tail
    xo = y_ref[...] + m[5:6] * acc_ref[...]
    if last:
        xo = _rms(xo) * fg_ref[...]
    y_ref[...] = xo


def _layer_prompt(mixer, last, x, mod, gains, fgain, mix_args, ffn):
    n = x.shape[0]
    batch = mod.shape[0]
    seq = n // batch
    tm = PROMPT_ROW_TILE
    tps = seq // tm
    in_specs = [pl.BlockSpec((tm, D_MODEL), lambda i: (i, 0)),
                pl.BlockSpec((1, N_MOD, D_MODEL), lambda i: (i // tps, 0, 0)),
                _resident((2, D_MODEL)), _resident((1, D_MODEL))]
    if mixer == "pool":
        pw, ps = mix_args
        in_specs += [_resident(pw.shape), _resident(ps.shape)]
    else:
        a, wp = mix_args
        in_specs += [pl.BlockSpec((tm, a.shape[1]), lambda i: (i, 0)), _resident(wp.shape)]
    in_specs += [_resident(w.shape) for w in ffn]
    out_shape = [jax.ShapeDtypeStruct((n, D_MODEL), F32),
                 jax.ShapeDtypeStruct((batch, N_FF_CHUNKS, SUBLANES, FF_CHUNK), F32)]
    out_specs = [pl.BlockSpec((tm, D_MODEL), lambda i: (i, 0)),
                 pl.BlockSpec((1, N_FF_CHUNKS, SUBLANES, FF_CHUNK), lambda i: (i // tps, 0, 0, 0))]
    scratch = [pltpu.VMEM((tm, D_MODEL), BF16), pltpu.VMEM((tm, D_MODEL), F32),
               pltpu.VMEM((2, tm + SUBLANES, FF_CHUNK), F32), pltpu.VMEM((N_FF_CHUNKS, SUBLANES, FF_CHUNK), F32)]
    if mixer == "pool":
        out_shape.append(jax.ShapeDtypeStruct((batch, POOL_CARRY_ROWS, D_MODEL), F32))
        out_specs.append(pl.BlockSpec((1, POOL_CARRY_ROWS, D_MODEL), lambda i: (i // tps, 0, 0)))
        scratch.append(pltpu.VMEM((tm + POOL_CARRY_ROWS, D_MODEL), F32))
    return pl.pallas_call(
        functools.partial(_layer_prompt_kernel, mixer, last, tm, tps),
        grid=(n // tm,), in_specs=in_specs, out_specs=out_specs, out_shape=out_shape,
        scratch_shapes=scratch, compiler_params=_params("arbitrary"),
        name=f"layer_prompt_{mixer}",
    )(x, mod, gains, fgain, *mix_args, *ffn)


def _layer_sample_kernel(mixer, last, nb, steps, *refs):
    refs = list(refs)
    x_ref, mod_ref, gains_ref, fg_ref = refs[:4]
    refs = refs[4:]
    if mixer == "pool":
        pw_ref, ps_ref, ppast_ref = refs[:3]
        refs = refs[3:]
    else:
        a_ref, wp_ref = refs[:2]
        refs = refs[2:]
    wg_ref, wu_ref, wd_ref, cw_ref, cb_ref, cpast_ref = refs[:6]
    refs = refs[6:]
    if mixer == "pool":
        y_ref, cs_ref, pst_ref, h2_ref, acc_ref = refs
    else:
        y_ref, cs_ref, h2_ref, acc_ref = refs
    rows = steps * nb

    x = x_ref[...]
    gains = gains_ref[...]
    if mixer == "pool":
        h = _norm_mod(x, gains[0:1], mod_ref[0], mod_ref[1], nb)
        new = [h[t * nb:(t + 1) * nb, :] for t in range(steps)]

        def u_rows(p, cols):
            if p < POOL_STATE_ROWS:
                return ppast_ref[p, :, cols]
            return new[p - POOL_STATE_ROWS][:, cols]

        parts = []
        for g, w in enumerate(POOL_WINDOWS):
            cols = slice(g * POOL_GROUP_DIM, (g + 1) * POOL_GROUP_DIM)
            ds = []
            for t in range(steps):
                s = u_rows(POOL_STATE_ROWS + t, cols)
                for k in range(1, w):
                    s = s + u_rows(POOL_STATE_ROWS + t - k, cols)
                ds.append(s * (1.0 / w) - new[t][:, cols])
            parts.append(_dot(jnp.concatenate(ds, axis=0).astype(BF16), pw_ref[g]))
        mix = jnp.concatenate(parts, axis=-1) * ps_ref[...]
        full = slice(0, D_MODEL)
        for p in range(POOL_STATE_ROWS):
            pst_ref[p] = u_rows(p + steps, full)
    else:
        mix = _dot(a_ref[...], wp_ref[...])

    x1 = x + _bcast_rows(mod_ref[2], mix, nb)
    y_ref[...] = x1
    h2_ref[...] = _norm_mod(x1, gains[1:2], mod_ref[3], mod_ref[4], nb).astype(BF16)
    acc_ref[...] = jnp.zeros_like(acc_ref)
    past_rows = (CONV_WIDTH - 1) * nb

    def chunk(j, carry):
        h2 = h2_ref[...]
        g = _dot(h2, wg_ref[j])
        u = _dot(h2, wu_ref[j])
        gall = jnp.concatenate([cpast_ref[j], g], axis=0)
        a = _ffn_chunk_math(g, gall[nb:nb + rows, :], gall[0:rows, :], u, cw_ref[j], cb_ref[j])
        acc_ref[...] += _dot(a, wd_ref[j])
        cs_ref[j] = gall[rows:rows + past_rows, :]
        return carry

    lax.fori_loop(0, N_FF_CHUNKS, chunk, 0)
    xo = y_ref[...] + _bcast_rows(mod_ref[5], acc_ref[...], nb)
    if last:
        xo = _rms(xo) * fg_ref[...]
    y_ref[...] = xo


def _layer_sample(mixer, last, x, mod, gains, fgain, mix_args, ffn, conv_past):
    rows = x.shape[0]
    nb = mod.shape[1]
    steps = rows // nb
    args = [x, mod, gains, fgain, *mix_args, *ffn, conv_past]
    out_shape = [jax.ShapeDtypeStruct((rows, D_MODEL), F32),
                 jax.ShapeDtypeStruct(conv_past.shape, F32)]
    if mixer == "pool":
        out_shape.append(jax.ShapeDtypeStruct((POOL_STATE_ROWS, nb, D_MODEL), F32))
    return pl.pallas_call(
        functools.partial(_layer_sample_kernel, mixer, last, nb, steps),
        grid=(1,),
        in_specs=[_resident(a.shape) for a in args],
        out_specs=[pl.BlockSpec(s.shape, functools.partial(lambda nd, i: (0,) * nd, len(s.shape))) for s in out_shape],
        out_shape=out_shape,
        scratch_shapes=[pltpu.VMEM((rows, D_MODEL), BF16), pltpu.VMEM((rows, D_MODEL), F32)],
        compiler_params=_params("arbitrary"),
        name=f"layer_sample_{mixer}",
    )(*args)


def _qkv_prompt_kernel(seq, x_ref, mod_ref, gains_ref, w_ref, qkvp_ref, kvn_ref, h_ref, slab_ref):
    g = pl.program_id(1)

    @pl.when(g == 0)
    def _():
        m = mod_ref[0]
        h_ref[...] = _norm_mod(x_ref[0], gains_ref[0:1, :], m[0:1], m[1:2], None).astype(BF16)

    h = h_ref[...]
    for c in range(3):
        r = _dot(h, w_ref[0, :, c * GROUP_WIDTH:(c + 1) * GROUP_WIDTH])
        if c == 0:
            r = r * (HEAD_DIM ** -0.5)
        else:
            kvn_ref[0, 0, :, (c - 1) * GROUP_WIDTH:c * GROUP_WIDTH] = r
        slab_ref[2 * c] = r[:, 0:LANES]
        slab_ref[2 * c + 1] = r[:, LANES:2 * LANES]

    qkvp_ref[0, 0, 0:QUERY_BLOCK, :] = jnp.zeros((QUERY_BLOCK, 3 * GROUP_WIDTH), BF16)
    for gi, (_, dil) in enumerate(DILATED_GROUPS):
        @pl.when(g == gi)
        def _(dil=dil):
            per_class = seq // dil
            for s in range(6):
                for rho in range(dil):
                    if dil == 1:
                        v = slab_ref[s]
                    else:
                        v = slab_ref[s, pl.ds(rho, per_class, stride=dil), :]
                    r0 = QUERY_BLOCK + rho * per_class
                    qkvp_ref[0, 0, r0:r0 + per_class, s * LANES:(s + 1) * LANES] = v.astype(BF16)


def _qkv_prompt(x3, mod, gains, w3):
    batch, seq, _ = x3.shape
    width = 3 * GROUP_WIDTH
    return pl.pallas_call(
        functools.partial(_qkv_prompt_kernel, seq),
        grid=(batch, N_GROUPS),
        in_specs=[pl.BlockSpec((1, seq, D_MODEL), lambda b, g: (b, 0, 0)),
                  pl.BlockSpec((1, N_MOD, D_MODEL), lambda b, g: (b, 0, 0)),
                  pl.BlockSpec((2, D_MODEL), lambda b, g: (0, 0)),
                  pl.BlockSpec((1, D_MODEL, width), lambda b, g: (g, 0, 0))],
        out_specs=[pl.BlockSpec((1, 1, seq + QUERY_BLOCK, width), lambda b, g: (b, g, 0, 0)),
                   pl.BlockSpec((1, 1, seq, 2 * GROUP_WIDTH), lambda b, g: (b, g, 0, 0))],
        out_shape=[jax.ShapeDtypeStruct((batch, N_GROUPS, seq + QUERY_BLOCK, width), BF16),
                   jax.ShapeDtypeStruct((batch, N_GROUPS, seq, 2 * GROUP_WIDTH), F32)],
        scratch_shapes=[pltpu.VMEM((seq, D_MODEL), BF16), pltpu.VMEM((6, seq, LANES), F32)],
        compiler_params=_params("arbitrary", "arbitrary"),
        name="qkv_prompt",
    )(x3, mod, gains, w3)


def _head_lane_mask(rows, h):
    lane = lax.broadcasted_iota(jnp.int32, (rows, GROUP_WIDTH), 1)
    return (lane >= h * HEAD_DIM) & (lane < (h + 1) * HEAD_DIM)


def _attn_block(q, k, v, bias_ref, cols):
    masks = [_head_lane_mask(QUERY_BLOCK, h) for h in range(HEADS_PER_GROUP)]
    qs = jnp.concatenate([jnp.where(hm, q, jnp.zeros_like(q)) for hm in masks], axis=0)
    s = _dot_nt(qs, k) + bias_ref[:, cols]
    m = jnp.max(s, axis=-1, keepdims=True)
    p = jnp.exp(s - m)
    l = jnp.sum(p, axis=-1, keepdims=True)
    pv = _dot(p.astype(BF16), v) * (1.0 / l)
    lse_rows = jnp.broadcast_to(m + jnp.log(l), pv.shape)
    o = pv[0:QUERY_BLOCK]
    lse = lse_rows[0:QUERY_BLOCK]
    for h in range(1, HEADS_PER_GROUP):
        rows = slice(h * QUERY_BLOCK, (h + 1) * QUERY_BLOCK)
        o = jnp.where(masks[h], pv[rows], o)
        lse = jnp.where(masks[h], lse_rows[rows], lse)
    return o, lse


def _attn_prompt_kernel(seq, qkv_ref, brow_ref, o_ref, os_ref, ls_ref, bias_ref):
    @pl.when(pl.program_id(0) == 0)
    def _():
        left = lax.broadcasted_iota(jnp.int32, (QUERY_BLOCK, 2 * QUERY_BLOCK), 1) < QUERY_BLOCK
        for g in range(N_GROUPS):
            for h in range(HEADS_PER_GROUP):
                base = jnp.broadcast_to(brow_ref[g, h], (QUERY_BLOCK, 2 * QUERY_BLOCK))
                band = pltpu.roll(base, 0, 1, stride=1, stride_axis=0)
                rows = slice(h * QUERY_BLOCK, (h + 1) * QUERY_BLOCK)
                bias_ref[g, 0, rows, :] = band
                bias_ref[g, 1, rows, :] = jnp.where(left, NEG_INF, band)

    qc = slice(0, GROUP_WIDTH)
    kc = slice(GROUP_WIDTH, 2 * GROUP_WIDTH)
    vc = slice(2 * GROUP_WIDTH, 3 * GROUP_WIDTH)
    n_blocks = seq // QUERY_BLOCK
    for g, (_, dil) in enumerate(DILATED_GROUPS):
        blocks_per_class = n_blocks // dil

        def do_block(blk, carry, g=g, dil=dil, blocks_per_class=blocks_per_class):
            r0 = pl.multiple_of(blk * QUERY_BLOCK, QUERY_BLOCK)
            cur = pl.ds(r0 + QUERY_BLOCK, QUERY_BLOCK)
            q = qkv_ref[0, g, cur, qc]
            rho = blk // blocks_per_class
            in_class = blk % blocks_per_class
            if blocks_per_class == 1:
                o, lse = _attn_block(q, qkv_ref[0, g, cur, kc], qkv_ref[0, g, cur, vc],
                                     bias_ref.at[g, 0], slice(QUERY_BLOCK, 2 * QUERY_BLOCK))
            else:
                both = pl.ds(r0, 2 * QUERY_BLOCK)
                first = (in_class == 0).astype(jnp.int32)
                o, lse = _attn_block(q, qkv_ref[0, g, both, kc], qkv_ref[0, g, both, vc],
                                     bias_ref.at[g, first], slice(0, 2 * QUERY_BLOCK))
            start = in_class * (QUERY_BLOCK * dil) + rho
            for s in range(2):
                cols = slice(s * LANES, (s + 1) * LANES)
                if dil == 1:
                    os_ref[g, s, pl.ds(r0, QUERY_BLOCK), :] = o[:, cols]
                    ls_ref[g, s, pl.ds(r0, QUERY_BLOCK), :] = lse[:, cols]
                else:
                    os_ref[g, s, pl.ds(start, QUERY_BLOCK, stride=dil), :] = o[:, cols]
                    ls_ref[g, s, pl.ds(start, QUERY_BLOCK, stride=dil), :] = lse[:, cols]
            return carry

        lax.fori_loop(0, n_blocks, do_block, 0, unroll=ATTN_BLOCK_UNROLL)

    def merge(i, carry):
        r0 = pl.multiple_of(i * QUERY_BLOCK, QUERY_BLOCK)
        rows = pl.ds(r0, QUERY_BLOCK)
        for s in range(2):
            ls = [ls_ref[g, s, rows, :] for g in range(N_GROUPS)]
            mx = jnp.maximum(jnp.maximum(ls[0], ls[1]), ls[2])
            es = [jnp.exp(l - mx) for l in ls]
            inv = 1.0 / (es[0] + es[1] + es[2])
            for g in range(N_GROUPS):
                c0 = g * GROUP_WIDTH + s * LANES
                o_ref[0, rows, c0:c0 + LANES] = (os_ref[g, s, rows, :] * (es[g] * inv)).astype(BF16)
        return carry

    lax.fori_loop(0, seq // QUERY_BLOCK, merge, 0)


def _attn_prompt(qkvp, bias_rows):
    batch, _, padded, width = qkvp.shape
    seq = padded - QUERY_BLOCK
    return pl.pallas_call(
        functools.partial(_attn_prompt_kernel, seq),
        grid=(batch,),
        in_specs=[pl.BlockSpec((1, N_GROUPS, padded, width), lambda b: (b, 0, 0, 0)),
                  _resident(bias_rows.shape)],
        out_specs=pl.BlockSpec((1, seq, ATTN_INNER), lambda b: (b, 0, 0)),
        out_shape=jax.ShapeDtypeStruct((batch, seq, ATTN_INNER), BF16),
        scratch_shapes=[pltpu.VMEM((N_GROUPS, 2, seq, LANES), F32), pltpu.VMEM((N_GROUPS, 2, seq, LANES), F32),
                        pltpu.VMEM((N_GROUPS, 2, HEADS_PER_GROUP * QUERY_BLOCK, 2 * QUERY_BLOCK), F32)],
        compiler_params=_params("arbitrary"),
        name="attn_prompt",
    )(qkvp, bias_rows)


def _proj_sample_kernel(nb, x_ref, mod_ref, gains_ref, w_ref, o_ref):
    h = _norm_mod(x_ref[...], gains_ref[0:1, :], mod_ref[0], mod_ref[1], nb).astype(BF16)
    o_ref[...] = _dot(h, w_ref[...])


def _proj_sample(x, mod, gains, w):
    rows = x.shape[0]
    nb = mod.shape[1]
    args = [x, mod, gains, w]
    return pl.pallas_call(
        functools.partial(_proj_sample_kernel, nb),
        grid=(1,),
        in_specs=[_resident(a.shape) for a in args],
        out_specs=pl.BlockSpec((rows, w.shape[1]), lambda i: (0, 0)),
        out_shape=jax.ShapeDtypeStruct((rows, w.shape[1]), F32),
        compiler_params=_params("arbitrary"),
        name="qkv_sample",
    )(*args)


def _split3_bf16(a):
    hi = a.astype(BF16)
    r1 = a - hi.astype(F32)
    mid = r1.astype(BF16)
    lo = (r1 - mid.astype(F32)).astype(BF16)
    return hi, mid, lo


def _attn_sample_kernel(steps, q_ref, kvn_ref, c1_ref, c2_ref, c3_ref, t1_ref, t2_ref, t3_ref,
                        o_ref, n1_ref, n2_ref, n3_ref):
    lane = lax.broadcasted_iota(jnp.int32, (SUBLANES, LANES), 1)
    row = lax.broadcasted_iota(jnp.int32, (SUBLANES, LANES), 0)
    sel_head = jnp.where((lane == row) & (row < steps), 1.0, 0.0).astype(BF16)
    sel_tail = jnp.where((lane == row + (LANES - steps)) & (row < steps), 1.0, 0.0).astype(BF16)
    tail_lanes = lax.broadcasted_iota(jnp.int32, (HEAD_DIM, LANES), 1) >= LANES - steps
    outs, lses = [], []
    groups = ((c1_ref, t1_ref, n1_ref), (c2_ref, t2_ref, n2_ref), (c3_ref, t3_ref, n3_ref))
    for g, (c_ref, t_ref, n_ref) in enumerate(groups):
        p_rows = c_ref.shape[-1]
        pieces = _split3_bf16(kvn_ref[0, g])
        new_head = sum(_dot_tn(x, sel_head) for x in pieces)
        new_tail = sum(_dot_tn(x, sel_tail) for x in pieces)
        o_heads, l_heads = [], []
        for h in range(HEADS_PER_GROUP):
            ki, vi = h, HEADS_PER_GROUP + h
            kt = c_ref[0, ki]
            vt = c_ref[0, vi]
            k_all = jnp.concatenate([kt.astype(BF16), new_head[ki * HEAD_DIM:(ki + 1) * HEAD_DIM].astype(BF16)], axis=1)
            v_all = jnp.concatenate([vt.astype(BF16), new_head[vi * HEAD_DIM:(vi + 1) * HEAD_DIM].astype(BF16)], axis=1)
            qh = (q_ref[0, g, h] * (HEAD_DIM ** -0.5)).astype(BF16)
            s = _dot(qh, k_all) + t_ref[h]
            m = jnp.max(s, axis=-1, keepdims=True)
            p = jnp.exp(s - m)
            l = jnp.sum(p, axis=-1, keepdims=True)
            o_heads.append(_dot_nt(p.astype(BF16), v_all) * (1.0 / l))
            l_heads.append(m + jnp.log(l))
            for idx, src in ((ki, kt), (vi, vt)):
                shifted = pltpu.roll(src, p_rows - steps, 1)
                fresh = new_tail[idx * HEAD_DIM:(idx + 1) * HEAD_DIM]
                if p_rows > LANES:
                    n_ref[0, idx, :, 0:p_rows - LANES] = shifted[:, 0:p_rows - LANES]
                n_ref[0, idx, :, p_rows - LANES:p_rows] = jnp.where(tail_lanes, fresh, shifted[:, p_rows - LANES:p_rows])
        outs.append(o_heads)
        lses.append(l_heads)
    for h in range(HEADS_PER_GROUP):
        ls = [lses[g][h] for g in range(N_GROUPS)]
        mx = jnp.maximum(jnp.maximum(ls[0], ls[1]), ls[2])
        es = [jnp.exp(l - mx) for l in ls]
        inv = 1.0 / (es[0] + es[1] + es[2])
        for g in range(N_GROUPS):
            c0 = g * GROUP_WIDTH + h * HEAD_DIM
            o_ref[0, :, c0:c0 + HEAD_DIM] = outs[g][h] * (es[g] * inv)


def _attn_sample(steps, q, kv_new, caches, tables):
    nb = q.shape[0]
    in_specs = [pl.BlockSpec((1,) + q.shape[1:], lambda b: (b, 0, 0, 0, 0)),
                pl.BlockSpec((1,) + kv_new.shape[1:], lambda b: (b, 0, 0, 0))]
    cache_specs = [pl.BlockSpec((1,) + c.shape[1:], lambda b: (b, 0, 0, 0)) for c in caches]
    in_specs += cache_specs + [_resident(t.shape) for t in tables]
    return pl.pallas_call(
        functools.partial(_attn_sample_kernel, steps),
        grid=(nb,), in_specs=in_specs,
        out_specs=[pl.BlockSpec((1, SUBLANES, ATTN_INNER), lambda b: (b, 0, 0))] + cache_specs,
        out_shape=[jax.ShapeDtypeStruct((nb, SUBLANES, ATTN_INNER), F32)]
                  + [jax.ShapeDtypeStruct(c.shape, F32) for c in caches],
        compiler_params=_params("arbitrary"),
        name="attn_sample",
    )(q, kv_new, *caches, *tables)


def _gla_proj_kernel(nb, x_ref, mod_ref, gains_ref, w_ref, wgd_ref, wgu_ref, bg_ref,
                     q_ref, k_ref, v_ref, r_ref, la_ref):
    if nb is None:
        m = mod_ref[0]
        shift, scale = m[0:1], m[1:2]
    else:
        shift, scale = mod_ref[0], mod_ref[1]
    h = _norm_mod(x_ref[...], gains_ref[0:1, :], shift, scale, nb).astype(BF16)
    q_ref[...] = _dot(h, w_ref[:, 0:GLA_QK]) * (GLA_DK ** -0.5)
    k_ref[...] = _dot(h, w_ref[:, GLA_QK:2 * GLA_QK])
    v_ref[...] = _dot(h, w_ref[:, 2 * GLA_QK:2 * GLA_QK + GLA_V]).astype(BF16)
    r_ref[...] = _dot(h, w_ref[:, 2 * GLA_QK + GLA_V:2 * GLA_QK + 2 * GLA_V])
    gd = _dot(h, wgd_ref[...])
    gate = _dot(gd.astype(BF16), wgu_ref[...]) + bg_ref[...]
    la_ref[...] = jax.nn.log_sigmoid(gate) * (1.0 / GATE_TAU)


def _gla_proj(x, mod, gains, weights, nb, tm):
    rows = x.shape[0]
    if nb is None:
        tps = rows // mod.shape[0] // tm
        mod_spec = pl.BlockSpec((1, N_MOD, D_MODEL), lambda i: (i // tps, 0, 0))
    else:
        mod_spec = _resident(mod.shape)
    widths = (GLA_QK, GLA_QK, GLA_V, GLA_V, GLA_QK)
    dtypes = (F32, F32, BF16, F32, F32)
    return pl.pallas_call(
        functools.partial(_gla_proj_kernel, nb),
        grid=(rows // tm,),
        in_specs=[pl.BlockSpec((tm, D_MODEL), lambda i: (i, 0)), mod_spec, _resident((2, D_MODEL))]
                 + [_resident(w.shape) for w in weights],
        out_specs=[pl.BlockSpec((tm, w), lambda i: (i, 0)) for w in widths],
        out_shape=[jax.ShapeDtypeStruct((rows, w), dt) for w, dt in zip(widths, dtypes)],
        compiler_params=_params("arbitrary"),
        name="gla_proj",
    )(x, mod, gains, *weights)


def _cumsum_rows(g):
    rows = g.shape[0]
    row = lax.broadcasted_iota(jnp.int32, g.shape, 0)
    b = g
    shift = 1
    while shift < rows:
        b = b + jnp.where(row >= shift, pltpu.roll(b, shift, 0), 0.0)
        shift *= 2
    return b


def _gla_chunk(q, k, v, g, r, gain, mid, get_state, set_state):
    c = q.shape[0]
    b = _cumsum_rows(g)
    b_end = b[c - 1:c, :]
    b_mid = b[mid:mid + 1, :]
    q_in = (q * jnp.exp(b)).astype(BF16)
    q_rel = (q * jnp.exp(b - b_mid)).astype(BF16)
    k_rel = (k * jnp.exp(b_mid - b)).astype(BF16)
    k_out = (k * jnp.exp(b_end - b)).astype(BF16)
    decay_end = jnp.broadcast_to(jnp.exp(b_end), (SUBLANES, GLA_QK))
    ti = lax.broadcasted_iota(jnp.int32, (c, c), 0)
    si = lax.broadcasted_iota(jnp.int32, (c, c), 1)
    outs = []
    for h in range(GLA_HEADS):
        ks = slice(h * GLA_DK, (h + 1) * GLA_DK)
        vs = slice(h * GLA_DV, (h + 1) * GLA_DV)
        state = get_state(h)
        o = _dot(q_in[:, ks], state.astype(BF16))
        att = jnp.where(si <= ti, _dot_nt(q_rel[:, ks], k_rel[:, ks]), 0.0)
        o = o + _dot(att.astype(BF16), v[:, vs])
        decay_col = decay_end[:, ks].T[:, 0:1]
        set_state(h, decay_col * state + _dot_tn(k_out[:, ks], v[:, vs]))
        outs.append(_rms(o) * gain[:, vs] * _silu(r[:, vs]))
    return jnp.concatenate(outs, axis=-1)


def _gla_prompt_kernel(tt, q_ref, k_ref, v_ref, r_ref, la_ref, gain_ref, a_ref, so_ref, s_ref):
    t = pl.program_id(1)

    @pl.when(t == 0)
    def _():
        s_ref[...] = jnp.zeros_like(s_ref)

    gain = gain_ref[...]

    def get_state(h):
        return s_ref[h]

    def set_state(h, val):
        s_ref[h] = val

    def chunk(ci, carry):
        rows = pl.ds(pl.multiple_of(ci * GLA_CHUNK, GLA_CHUNK), GLA_CHUNK)
        a = _gla_chunk(q_ref[0, rows, :], k_ref[0, rows, :], v_ref[0, rows, :], la_ref[0, rows, :],
                       r_ref[0, rows, :], gain, GLA_CHUNK // 2, get_state, set_state)
        a_ref[0, rows, :] = a.astype(BF16)
        return carry

    lax.fori_loop(0, tt // GLA_CHUNK, chunk, 0, unroll=GLA_CHUNK_UNROLL)

    @pl.when(t == pl.num_programs(1) - 1)
    def _():
        so_ref[0] = s_ref[...]


def _gla_prompt(q, k, v, r, la, gain):
    batch, seq, _ = q.shape
    tt = GLA_TIME_TILE

    def spec(width):
        return pl.BlockSpec((1, tt, width), lambda b, t: (b, t, 0))

    return pl.pallas_call(
        functools.partial(_gla_prompt_kernel, tt),
        grid=(batch, seq // tt),
        in_specs=[spec(GLA_QK), spec(GLA_QK), spec(GLA_V), spec(GLA_V), spec(GLA_QK),
                  pl.BlockSpec((1, GLA_V), lambda b, t: (0, 0))],
        out_specs=[spec(GLA_V), pl.BlockSpec((1, GLA_HEADS, GLA_DK, GLA_DV), lambda b, t: (b, 0, 0, 0))],
        out_shape=[jax.ShapeDtypeStruct((batch, seq, GLA_V), BF16),
                   jax.ShapeDtypeStruct((batch, GLA_HEADS, GLA_DK, GLA_DV), F32)],
        scratch_shapes=[pltpu.VMEM((GLA_HEADS, GLA_DK, GLA_DV), F32)],
        compiler_params=_params("arbitrary", "arbitrary"),
        name="gla_prompt",
    )(q, k, v, r, la, gain)


def _gla_sample_kernel(sb, q_ref, k_ref, v_ref, r_ref, la_ref, gain_ref, s0_ref, a_ref, so_ref):
    gain = gain_ref[...]

    def seq_body(i, carry):
        def get_state(h):
            return s0_ref[i, h]

        def set_state(h, val):
            so_ref[i, h] = val

        a_ref[i] = _gla_chunk(q_ref[i], k_ref[i], v_ref[i], la_ref[i], r_ref[i], gain, 0, get_state, set_state)
        return carry

    lax.fori_loop(0, sb, seq_body, 0)


def _gla_sample(q, k, v, r, la, gain, s0):
    nb, pad, _ = q.shape
    sb = math.gcd(nb, 8)

    def spec(width):
        return pl.BlockSpec((sb, pad, width), lambda i: (i, 0, 0))

    state_spec = pl.BlockSpec((sb, GLA_HEADS, GLA_DK, GLA_DV), lambda i: (i, 0, 0, 0))
    return pl.pallas_call(
        functools.partial(_gla_sample_kernel, sb),
        grid=(nb // sb,),
        in_specs=[spec(GLA_QK), spec(GLA_QK), spec(GLA_V), spec(GLA_V), spec(GLA_QK),
                  pl.BlockSpec((1, GLA_V), lambda i: (0, 0)), state_spec],
        out_specs=[spec(GLA_V), state_spec],
        out_shape=[jax.ShapeDtypeStruct((nb, pad, GLA_V), F32),
                   jax.ShapeDtypeStruct((nb, GLA_HEADS, GLA_DK, GLA_DV), F32)],
        compiler_params=_params("arbitrary"),
        name="gla_sample",
    )(q, k, v, r, la, gain, s0)


def _t5_bucket(dist):
    max_exact = NUM_BUCKETS // 2
    d_f = jnp.maximum(dist, 1).astype(F32)
    large = max_exact + (jnp.log(d_f / max_exact) / math.log(MAX_DISTANCE / max_exact)
                         * (NUM_BUCKETS - max_exact)).astype(jnp.int32)
    large = jnp.minimum(large, NUM_BUCKETS - 1)
    return jnp.where(dist < max_exact, dist, large)


def _group_bias(rel_bias):
    rows = []
    for g, (window, dil) in enumerate(DILATED_GROUPS):
        buckets = _t5_bucket(jnp.arange(window // dil + 1) * dil)
        rows.append(rel_bias[buckets][:, g * HEADS_PER_GROUP:(g + 1) * HEADS_PER_GROUP].T)
    return jnp.stack(rows)


def _prompt_bias_rows(gb):
    band = gb[:, :, ::-1]
    off = jnp.full(gb.shape[:2] + (2 * QUERY_BLOCK - KEYS_PER_QUERY,), NEG_INF, F32)
    return jnp.concatenate([band, off], axis=-1)[:, :, None, :]


def _sample_bias_tables(gb, steps, past_rows):
    tables = []
    t_idx = np.arange(SUBLANES)[:, None]
    c_idx = np.arange(LANES)[None, :]
    for g, (window, dil) in enumerate(DILATED_GROUPS):
        p = past_rows[g]
        assert p == window == (KEYS_PER_QUERY - 1) * dil
        b = gb[g]
        heads = b.shape[0]
        row0 = b[:, :0:-1]
        if dil > 1:
            gaps = jnp.full((heads, KEYS_PER_QUERY - 1, dil - 1), NEG_INF, F32)
            row0 = jnp.concatenate([row0[:, :, None], gaps], axis=2).reshape(heads, p)
        rows = []
        for t in range(SUBLANES):
            if t < steps:
                rows.append(jnp.concatenate([jnp.full((heads, t), NEG_INF, F32), row0[:, :p - t]], axis=1))
            else:
                rows.append(jnp.full((heads, p), NEG_INF, F32))
        past = jnp.stack(rows, axis=1)
        new = jnp.full((heads, SUBLANES, LANES), NEG_INF, F32)
        for j in range((steps - 1) // dil + 1):
            mask = (t_idx - c_idx == j * dil) & (t_idx < steps) & (c_idx < steps)
            new = jnp.where(jnp.asarray(mask)[None], b[:, j][:, None, None], new)
        tables.append(jnp.concatenate([past, new], axis=2))
    return tables


def _ffn_weights(w_in, conv_w, conv_b, w_down):
    def cols(w):
        return jnp.transpose(w.reshape(D_MODEL, N_FF_CHUNKS, FF_CHUNK), (1, 0, 2)).astype(BF16)
    return (cols(w_in[:, :D_FF]), cols(w_in[:, D_FF:]),
            w_down.reshape(N_FF_CHUNKS, FF_CHUNK, D_MODEL).astype(BF16),
            jnp.transpose(conv_w.reshape(CONV_WIDTH, N_FF_CHUNKS, FF_CHUNK), (1, 0, 2)),
            conv_b.reshape(N_FF_CHUNKS, 1, FF_CHUNK))


def _conv_tail_prompt(cs):
    batch = cs.shape[0]
    tail = cs[:, :, SUBLANES - (CONV_WIDTH - 1):, :]
    return jnp.transpose(tail, (0, 2, 1, 3)).reshape(batch, CONV_WIDTH - 1, D_FF)


def _conv_past_sample(state):
    nb = state.shape[0]
    s = state.reshape(nb, CONV_WIDTH - 1, N_FF_CHUNKS, FF_CHUNK)
    return jnp.transpose(s, (2, 1, 0, 3)).reshape(N_FF_CHUNKS, (CONV_WIDTH - 1) * nb, FF_CHUNK)


def _conv_tail_sample(cs, nb):
    s = cs.reshape(N_FF_CHUNKS, CONV_WIDTH - 1, nb, FF_CHUNK)
    return jnp.transpose(s, (2, 1, 0, 3)).reshape(nb, CONV_WIDTH - 1, D_FF)


def kernel(x_prompt, x_sample, state_pool, cache_win_g1, cache_win_g2, cache_win_g3, state_gla, state_ffn_conv,
           c_prompt, c_sample, w_ada, b_ada, norm_gain, final_gain, rel_bias, pool_w, pool_scale,
           attn_w_in, attn_w_out, gla_w_in, gla_w_gate_up, gla_b_gate, gla_norm_gain, gla_w_out,
           ffn_w_in, ffn_conv_w, ffn_conv_b, ffn_w_down):
    batch, seq, _ = x_prompt.shape
    nb, steps, _ = x_sample.shape
    caches = (cache_win_g1, cache_win_g2, cache_win_g3)

    mods = _modulation(jnp.concatenate([c_prompt, c_sample], axis=0), w_ada, b_ada)
    mod_p = mods[:, :batch].reshape(DEPTH, batch, N_MOD, D_MODEL)
    mod_s = jnp.transpose(mods[:, batch:].reshape(DEPTH, nb, N_MOD, D_MODEL), (0, 2, 1, 3))
    fgain = final_gain.reshape(1, D_MODEL)

    xp = x_prompt.reshape(batch * seq, D_MODEL)
    xs = jnp.transpose(x_sample, (1, 0, 2)).reshape(steps * nb, D_MODEL)

    pool_p, pool_s, gla_p, gla_s, conv_p, conv_s = [], [], [], [], [], []
    win_p, win_s = None, None

    for i in range(DEPTH):
        kind, j = i % 3, i // 3
        last = i == DEPTH - 1
        ffn = _ffn_weights(ffn_w_in[i], ffn_conv_w[i], ffn_conv_b[i], ffn_w_down[i])
        conv_past = _conv_past_sample(state_ffn_conv[i])
        gains = norm_gain[i]
        if kind == 0:
            mix_w = (pool_w[j].astype(BF16), pool_scale[j].reshape(1, D_MODEL))
            xp, cs, pst = _layer_prompt("pool", last, xp, mod_p[i], gains, fgain, mix_w, ffn)
            pool_p.append(pst[:, POOL_CARRY_ROWS - POOL_STATE_ROWS:])
            past = jnp.transpose(state_pool[j], (1, 0, 2))
            xs, css, psts = _layer_sample("pool", last, xs, mod_s[i], gains, fgain, mix_w + (past,), ffn, conv_past)
            pool_s.append(jnp.transpose(psts, (1, 0, 2)))
        elif kind == 1:
            w = attn_w_in[j]
            w3 = jnp.stack([jnp.concatenate([w[:, s * ATTN_INNER + g * GROUP_WIDTH:
                                                s * ATTN_INNER + (g + 1) * GROUP_WIDTH] for s in range(3)], axis=1)
                            for g in range(N_GROUPS)]).astype(BF16)
            gb = _group_bias(rel_bias)
            wo = attn_w_out[j].astype(BF16)
            qkvp, kvn = _qkv_prompt(xp.reshape(batch, seq, D_MODEL), mod_p[i], gains, w3)
            o_all = _attn_prompt(qkvp, _prompt_bias_rows(gb))
            win_p = [kvn[:, g, seq - min(window, seq):].reshape(1, batch, min(window, seq), 2, HEADS_PER_GROUP, HEAD_DIM)
                     for g, (window, _) in enumerate(DILATED_GROUPS)]
            xp, cs = _layer_prompt("proj", last, xp, mod_p[i], gains, fgain,
                                   (o_all.reshape(batch * seq, ATTN_INNER), wo), ffn)
            qkv_s = _proj_sample(xs, mod_s[i], gains, w.astype(BF16))
            q6 = qkv_s.reshape(steps, nb, 3, N_GROUPS, HEADS_PER_GROUP, HEAD_DIM)
            q_s = jnp.pad(jnp.transpose(q6[:, :, 0], (1, 2, 3, 0, 4)),
                          ((0, 0), (0, 0), (0, 0), (0, SUBLANES - steps), (0, 0)))
            kv_new = jnp.transpose(q6[:, :, 1:], (1, 3, 0, 2, 4, 5)).reshape(nb, N_GROUPS, steps, 2 * GROUP_WIDTH)
            kv_new = jnp.pad(kv_new, ((0, 0), (0, 0), (0, SUBLANES - steps), (0, 0)))
            past_rows = [c.shape[2] for c in caches]
            cache_t = [jnp.transpose(c[j], (0, 2, 3, 4, 1)).reshape(nb, 2 * HEADS_PER_GROUP, HEAD_DIM, p)
                       for c, p in zip(caches, past_rows)]
            o8, *new_caches = _attn_sample(steps, q_s, kv_new, cache_t, _sample_bias_tables(gb, steps, past_rows))
            a_s = jnp.transpose(o8[:, :steps], (1, 0, 2)).reshape(steps * nb, ATTN_INNER)
            win_s = [jnp.transpose(c.reshape(nb, 2, HEADS_PER_GROUP, HEAD_DIM, p), (0, 4, 1, 2, 3))[None]
                     for c, p in zip(new_caches, past_rows)]
            xs, css = _layer_sample("proj", last, xs, mod_s[i], gains, fgain, (a_s.astype(BF16), wo), ffn, conv_past)
        else:
            w = gla_w_in[j]
            n_main = 2 * GLA_QK + 2 * GLA_V
            weights = (w[:, :n_main].astype(BF16),
                       jnp.pad(w[:, n_main:], ((0, 0), (0, LANES - GATE_RANK))).astype(BF16),
                       jnp.pad(gla_w_gate_up[j], ((0, LANES - GATE_RANK), (0, 0))).astype(BF16),
                       gla_b_gate[j].reshape(1, GLA_QK))
            gain = gla_norm_gain[j].reshape(1, GLA_V)
            wo = gla_w_out[j].astype(BF16)
            q, k, v, r, la = _gla_proj(xp, mod_p[i], gains, weights, None, PROMPT_ROW_TILE)
            shp = lambda a: a.reshape(batch, seq, a.shape[-1])
            a_p, s_p = _gla_prompt(shp(q), shp(k), shp(v), shp(r), shp(la), gain)
            gla_p.append(s_p)
            xp, cs = _layer_prompt("proj", last, xp, mod_p[i], gains, fgain,
                                   (a_p.reshape(batch * seq, GLA_V), wo), ffn)
            outs = _gla_proj(xs, mod_s[i], gains, weights, nb, steps * nb)

            def per_seq(a):
                a = jnp.transpose(a.reshape(steps, nb, a.shape[-1]), (1, 0, 2))
                return jnp.pad(a, ((0, 0), (0, SAMPLE_DEC_PAD - steps), (0, 0)))

            qs, ks, vs, rs, las = (per_seq(a) for a in outs)
            a16, s_s = _gla_sample(qs, ks, vs, rs, las, gain, state_gla[j])
            gla_s.append(s_s)
            a_s = jnp.transpose(a16[:, :steps], (1, 0, 2)).reshape(steps * nb, GLA_V).astype(BF16)
            xs, css = _layer_sample("proj", last, xs, mod_s[i], gains, fgain, (a_s, wo), ffn, conv_past)
        conv_p.append(_conv_tail_prompt(cs))
        conv_s.append(_conv_tail_sample(css, nb))

    y_prompt = xp.reshape(batch, seq, D_MODEL)
    y_sample = jnp.transpose(xs.reshape(steps, nb, D_MODEL), (1, 0, 2))
    return (y_prompt, y_sample, jnp.stack(pool_p), jnp.stack(pool_s),
            win_p[0], win_s[0], win_p[1], win_s[1], win_p[2], win_s[2],
            jnp.stack(gla_p), jnp.stack(gla_s), jnp.stack(conv_p), jnp.stack(conv_s))
```

```python
import functools
import math

import numpy as np
import jax
import jax.numpy as jnp
from jax import lax
from jax.experimental import pallas as pl
from jax.experimental.pallas import tpu as pltpu

F32 = jnp.float32
BF16 = jnp.bfloat16

D_MODEL = 1024
DEPTH = 4
N_MOD = 6
EPS = 1e-6
NEG_INF = -1e30
POOL_WINDOWS = (2, 4, 8, 16)
POOL_GROUP_DIM = D_MODEL // len(POOL_WINDOWS)
POOL_STATE_ROWS = max(POOL_WINDOWS) - 1
POOL_CARRY_ROWS = 16
DILATED_GROUPS = ((128, 1), (512, 4), (2048, 16))
N_GROUPS = len(DILATED_GROUPS)
HEADS_PER_GROUP = 4
HEAD_DIM = 64
GROUP_WIDTH = HEADS_PER_GROUP * HEAD_DIM
ATTN_INNER = N_GROUPS * GROUP_WIDTH
KEYS_PER_QUERY = 129
QUERY_BLOCK = 128
NUM_BUCKETS = 32
MAX_DISTANCE = 2048
GLA_HEADS = 4
GLA_DK = 128
GLA_DV = 256
GLA_QK = GLA_HEADS * GLA_DK
GLA_V = GLA_HEADS * GLA_DV
GATE_RANK = 16
GATE_TAU = 16.0
GLA_CHUNK = 64
D_FF = 2816
CONV_WIDTH = 3

LANES = 128
SUBLANES = 8
FF_CHUNK = 256
N_FF_CHUNKS = D_FF // FF_CHUNK
VMEM_LIMIT_BYTES = 56 * 1024 * 1024
PROMPT_ROW_TILE = 1024
GLA_TIME_TILE = 512
ATTN_BLOCK_UNROLL = 4
GLA_CHUNK_UNROLL = 4
SAMPLE_DEC_PAD = 16


def _params(*semantics):
    return pltpu.CompilerParams(dimension_semantics=semantics, vmem_limit_bytes=VMEM_LIMIT_BYTES)


def _resident(shape):
    nd = len(shape)
    return pl.BlockSpec(shape, lambda *_: (0,) * nd, pipeline_mode=pl.Buffered(1))


def _dot(a, b):
    return jnp.dot(a, b, preferred_element_type=F32)


def _dot_nt(a, b):
    return lax.dot_general(a, b, (((1,), (1,)), ((), ())), preferred_element_type=F32)


def _dot_tn(a, b):
    return lax.dot_general(a, b, (((0,), (0,)), ((), ())), preferred_element_type=F32)


def _rms(x):
    return x * lax.rsqrt(jnp.mean(x * x, axis=-1, keepdims=True) + EPS)


def _bcast_rows(v, y, nb):
    if nb is None:
        return v * y
    rows, width = y.shape
    return (y.reshape(rows // nb, nb, width) * v[None]).reshape(rows, width)


def _norm_mod(x, gain, shift, scale, nb):
    y = _rms(x) * gain
    if nb is None:
        return y * (1.0 + scale) + shift
    rows, width = y.shape
    y3 = y.reshape(rows // nb, nb, width)
    return (y3 * (1.0 + scale)[None] + shift[None]).reshape(rows, width)


def _gelu(x):
    return 0.5 * x * (1.0 + lax.erf(x * (1.0 / math.sqrt(2.0))))


def _silu(x):
    return x * jax.nn.sigmoid(x)


def _split_bf16(a):
    hi = a.astype(BF16)
    lo = (a - hi.astype(F32)).astype(BF16)
    return hi, lo


def _mod_kernel(c_ref, w_ref, b_ref, o_ref):
    a_hi, a_lo = _split_bf16(_silu(c_ref[...]))
    w_hi, w_lo = _split_bf16(w_ref[0])
    o_ref[0] = _dot(a_hi, w_hi) + _dot(a_lo, w_hi) + _dot(a_hi, w_lo) + b_ref[0]


def _modulation(c_all, w_ada, b_ada):
    rows = c_all.shape[0]
    width = N_MOD * D_MODEL
    tn = 1536
    return pl.pallas_call(
        _mod_kernel,
        grid=(DEPTH, width // tn),
        in_specs=[pl.BlockSpec((rows, D_MODEL), lambda l, n: (0, 0)),
                  pl.BlockSpec((1, D_MODEL, tn), lambda l, n: (l, 0, n)),
                  pl.BlockSpec((1, 1, tn), lambda l, n: (l, 0, n))],
        out_specs=pl.BlockSpec((1, rows, tn), lambda l, n: (l, 0, n)),
        out_shape=jax.ShapeDtypeStruct((DEPTH, rows, width), F32),
        compiler_params=_params("parallel", "parallel"),
        name="adaln_mod",
    )(c_all, w_ada, b_ada.reshape(DEPTH, 1, width))


def _ffn_chunk_math(g, g_m1, g_m2, u, cw, cb):
    gc = cw[2:3] * g + cw[1:2] * g_m1 + cw[0:1] * g_m2 + cb
    return (_gelu(gc) * u).astype(BF16)


def _ffn_chunk_weights(win_ref, wd_ref, cw_ref, cb_ref, j):
    cols = slice(j * FF_CHUNK, (j + 1) * FF_CHUNK)
    ucols = slice(D_FF + j * FF_CHUNK, D_FF + (j + 1) * FF_CHUNK)
    return win_ref[0, :, cols], win_ref[0, :, ucols], wd_ref[0, cols, :], cw_ref[0, :, cols], cb_ref[0, :, cols]


def _layer_resident(array, layer):
    nd = array.ndim
    return pl.BlockSpec((1,) + array.shape[1:], lambda *_: (layer,) + (0,) * (nd - 1),
                        pipeline_mode=pl.Buffered(1))


def _layer_prompt_kernel(mixer, last, tm, tiles_per_seq, *refs):
    refs = list(refs)
    x_ref, mod_ref, gains_ref, fg_ref = refs[:4]
    refs = refs[4:]
    if mixer == "pool":
        pw_ref, ps_ref = refs[:2]
    else:
        a_ref, wp_ref = refs[:2]
    win_ref, wd_ref, cw_ref, cb_ref = refs[2:6]
    refs = refs[6:]
    if mixer == "pool":
        y_ref, cs_ref, pst_ref, h2_ref, acc_ref, gext_ref, cc_ref, hext_ref = refs
    else:
        y_ref, cs_ref, h2_ref, acc_ref, gext_ref, cc_ref = refs

    i = pl.program_id(0)
    tile_in_seq = i % tiles_per_seq

    @pl.when(tile_in_seq == 0)
    def _():
        cc_ref[...] = jnp.zeros_like(cc_ref)
        if mixer == "pool":
            hext_ref[0:POOL_CARRY_ROWS, :] = jnp.zeros((POOL_CARRY_ROWS, D_MODEL), F32)

    x = x_ref[...]
    m = mod_ref[0]
    gains = gains_ref[...]

    if mixer == "pool":
        h = _norm_mod(x, gains[0:1], m[0:1], m[1:2], None)
        hext_ref[POOL_CARRY_ROWS:, :] = h
        pos = tile_in_seq * tm + lax.broadcasted_iota(jnp.int32, (tm, 1), 0)
        parts = []
        for g, w in enumerate(POOL_WINDOWS):
            cols = slice(g * POOL_GROUP_DIM, (g + 1) * POOL_GROUP_DIM)
            s = hext_ref[:, cols]
            span = 1
            while span < w:
                s = s + pltpu.roll(s, span, 0)
                span *= 2
            inv_count = 1.0 / jnp.minimum(pos + 1, w).astype(F32)
            d = s[POOL_CARRY_ROWS:, :] * inv_count - h[:, cols]
            parts.append(_dot(d.astype(BF16), pw_ref[g]))
        mix = jnp.concatenate(parts, axis=-1) * ps_ref[...]
        tail = hext_ref[pl.ds(tm, POOL_CARRY_ROWS), :]
        hext_ref[0:POOL_CARRY_ROWS, :] = tail
        pst_ref[0] = tail
    else:
        mix = _dot(a_ref[...], wp_ref[...])

    x1 = x + m[2:3] * mix
    y_ref[...] = x1
    h2_ref[...] = _norm_mod(x1, gains[1:2], m[3:4], m[4:5], None).astype(BF16)
    acc_ref[...] = jnp.zeros_like(acc_ref)

    for j in range(N_FF_CHUNKS):
        h2 = h2_ref[...]
        wg, wu, wd, cw, cb = _ffn_chunk_weights(win_ref, wd_ref, cw_ref, cb_ref, j)
        g = _dot(h2, wg)
        u = _dot(h2, wu)
        gx = gext_ref.at[j % 2]
        gx[0:SUBLANES, :] = cc_ref[j]
        gx[SUBLANES:, :] = g
        a = _ffn_chunk_math(g, gx[pl.ds(SUBLANES - 1, tm), :], gx[pl.ds(SUBLANES - 2, tm), :], u, cw, cb)
        acc_ref[...] += _dot(a, wd)
        tail = g[tm - SUBLANES:tm, :]
        cc_ref[j] = tail
        cs_ref[0, j] = tail
    xo = y_ref[...] + m[5:6] * acc_ref[...]
    if last:
        xo = _rms(xo) * fg_ref[...]
    y_ref[...] = xo


def _layer_prompt(mixer, last, x, mod, gains, fgain, mix_args, ffn):
    n = x.shape[0]
    batch = mod.shape[0]
    seq = n // batch
    tm = PROMPT_ROW_TILE
    tps = seq // tm
    in_specs = [pl.BlockSpec((tm, D_MODEL), lambda i: (i, 0)),
                pl.BlockSpec((1, N_MOD, D_MODEL), lambda i: (i // tps, 0, 0)),
                _resident((2, D_MODEL)), _resident((1, D_MODEL))]
    if mixer == "pool":
        pw, ps = mix_args
        in_specs += [_resident(pw.shape), _resident(ps.shape)]
    else:
        a, wp = mix_args
        in_specs += [pl.BlockSpec((tm, a.shape[1]), lambda i: (i, 0)), _resident(wp.shape)]
    layer, ffn_w = ffn
    in_specs += [_layer_resident(w, layer) for w in ffn_w]
    out_shape = [jax.ShapeDtypeStruct((n, D_MODEL), F32),
                 jax.ShapeDtypeStruct((batch, N_FF_CHUNKS, SUBLANES, FF_CHUNK), F32)]
    out_specs = [pl.BlockSpec((tm, D_MODEL), lambda i: (i, 0)),
                 pl.BlockSpec((1, N_FF_CHUNKS, SUBLANES, FF_CHUNK), lambda i: (i // tps, 0, 0, 0))]
    scratch = [pltpu.VMEM((tm, D_MODEL), BF16), pltpu.VMEM((tm, D_MODEL), F32),
               pltpu.VMEM((2, tm + SUBLANES, FF_CHUNK), F32), pltpu.VMEM((N_FF_CHUNKS, SUBLANES, FF_CHUNK), F32)]
    if mixer == "pool":
        out_shape.append(jax.ShapeDtypeStruct((batch, POOL_CARRY_ROWS, D_MODEL), F32))
        out_specs.append(pl.BlockSpec((1, POOL_CARRY_ROWS, D_MODEL), lambda i: (i // tps, 0, 0)))
        scratch.append(pltpu.VMEM((tm + POOL_CARRY_ROWS, D_MODEL), F32))
    return pl.pallas_call(
        functools.partial(_layer_prompt_kernel, mixer, last, tm, tps),
        grid=(n // tm,), in_specs=in_specs, out_specs=out_specs, out_shape=out_shape,
        scratch_shapes=scratch, compiler_params=_params("arbitrary"),
        name=f"layer_prompt_{mixer}",
    )(x, mod, gains, fgain, *mix_args, *ffn_w)


def _layer_sample_kernel(mixer, last, nb, steps, *refs):
    refs = list(refs)
    x_ref, mod_ref, gains_ref, fg_ref = refs[:4]
    refs = refs[4:]
    if mixer == "pool":
        pw_ref, ps_ref, ppast_ref = refs[:3]
        refs = refs[3:]
    else:
        a_ref, wp_ref = refs[:2]
        refs = refs[2:]
    win_ref, wd_ref, cw_ref, cb_ref, cpast_ref = refs[:5]
    refs = refs[5:]
    if mixer == "pool":
        y_ref, cs_ref, pst_ref, h2_ref, acc_ref = refs
    else:
        y_ref, cs_ref, h2_ref, acc_ref = refs
    rows = steps * nb

    x = x_ref[...]
    gains = gains_ref[...]
    if mixer == "pool":
        h = _norm_mod(x, gains[0:1], mod_ref[0], mod_ref[1], nb)
        new = [h[t * nb:(t + 1) * nb, :] for t in range(steps)]

        def u_rows(p, cols):
            if p < POOL_STATE_ROWS:
                return ppast_ref[p, :, cols]
            return new[p - POOL_STATE_ROWS][:, cols]

        parts = []
        for g, w in enumerate(POOL_WINDOWS):
            cols = slice(g * POOL_GROUP_DIM, (g + 1) * POOL_GROUP_DIM)
            ds = []
            for t in range(steps):
                s = u_rows(POOL_STATE_ROWS + t, cols)
                for k in range(1, w):
                    s = s + u_rows(POOL_STATE_ROWS + t - k, cols)
                ds.append(s * (1.0 / w) - new[t][:, cols])
            parts.append(_dot(jnp.concatenate(ds, axis=0).astype(BF16), pw_ref[g]))
        mix = jnp.concatenate(parts, axis=-1) * ps_ref[...]
        full = slice(0, D_MODEL)
        for p in range(POOL_STATE_ROWS):
            pst_ref[p] = u_rows(p + steps, full)
    else:
        mix = _dot(a_ref[...], wp_ref[...])

    x1 = x + _bcast_rows(mod_ref[2], mix, nb)
    y_ref[...] = x1
    h2_ref[...] = _norm_mod(x1, gains[1:2], mod_ref[3], mod_ref[4], nb).astype(BF16)
    acc_ref[...] = jnp.zeros_like(acc_ref)
    past_rows = (CONV_WIDTH - 1) * nb

    for j in range(N_FF_CHUNKS):
        h2 = h2_ref[...]
        wg, wu, wd, cw, cb = _ffn_chunk_weights(win_ref, wd_ref, cw_ref, cb_ref, j)
        g = _dot(h2, wg)
        u = _dot(h2, wu)
        gall = jnp.concatenate([cpast_ref[j], g], axis=0)
        a = _ffn_chunk_math(g, gall[nb:nb + rows, :], gall[0:rows, :], u, cw, cb)
        acc_ref[...] += _dot(a, wd)
        cs_ref[j] = gall[rows:rows + past_rows, :]
    xo = y_ref[...] + _bcast_rows(mod_ref[5], acc_ref[...], nb)
    if last:
        xo = _rms(xo) * fg_ref[...]
    y_ref[...] = xo


def _layer_sample(mixer, last, x, mod, gains, fgain, mix_args, ffn, conv_past):
    rows = x.shape[0]
    nb = mod.shape[1]
    steps = rows // nb
    layer, ffn_w = ffn
    head = [x, mod, gains, fgain, *mix_args]
    args = [*head, *ffn_w, conv_past]
    out_shape = [jax.ShapeDtypeStruct((rows, D_MODEL), F32),
                 jax.ShapeDtypeStruct(conv_past.shape, F32)]
    if mixer == "pool":
        out_shape.append(jax.ShapeDtypeStruct((POOL_STATE_ROWS, nb, D_MODEL), F32))
    return pl.pallas_call(
        functools.partial(_layer_sample_kernel, mixer, last, nb, steps),
        grid=(1,),
        in_specs=[_resident(a.shape) for a in head] + [_layer_resident(w, layer) for w in ffn_w]
                 + [_resident(conv_past.shape)],
        out_specs=[pl.BlockSpec(s.shape, functools.partial(lambda nd, i: (0,) * nd, len(s.shape))) for s in out_shape],
        out_shape=out_shape,
        scratch_shapes=[pltpu.VMEM((rows, D_MODEL), BF16), pltpu.VMEM((rows, D_MODEL), F32)],
        compiler_params=_params("arbitrary"),
        name=f"layer_sample_{mixer}",
    )(*args)


def _qkv_prompt_kernel(seq, x_ref, mod_ref, gains_ref, w_ref, qkvp_ref, kvn_ref, h_ref, slab_ref):
    g = pl.program_id(1)

    @pl.when(g == 0)
    def _():
        m = mod_ref[0]
        h_ref[...] = _norm_mod(x_ref[0], gains_ref[0:1, :], m[0:1], m[1:2], None).astype(BF16)

    h = h_ref[...]
    for c in range(3):
        r = _dot(h, w_ref[0, :, c * GROUP_WIDTH:(c + 1) * GROUP_WIDTH])
        if c == 0:
            r = r * (HEAD_DIM ** -0.5)
        else:
            kvn_ref[0, 0, :, (c - 1) * GROUP_WIDTH:c * GROUP_WIDTH] = r
        slab_ref[2 * c] = r[:, 0:LANES]
        slab_ref[2 * c + 1] = r[:, LANES:2 * LANES]

    qkvp_ref[0, 0, 0:QUERY_BLOCK, :] = jnp.zeros((QUERY_BLOCK, 3 * GROUP_WIDTH), BF16)
    for gi, (_, dil) in enumerate(DILATED_GROUPS):
        @pl.when(g == gi)
        def _(dil=dil):
            per_class = seq // dil
            for s in range(6):
                for rho in range(dil):
                    if dil == 1:
                        v = slab_ref[s]
                    else:
                        v = slab_ref[s, pl.ds(rho, per_class, stride=dil), :]
                    r0 = QUERY_BLOCK + rho * per_class
                    qkvp_ref[0, 0, r0:r0 + per_class, s * LANES:(s + 1) * LANES] = v.astype(BF16)


def _qkv_prompt(x3, mod, gains, w3):
    batch, seq, _ = x3.shape
    width = 3 * GROUP_WIDTH
    return pl.pallas_call(
        functools.partial(_qkv_prompt_kernel, seq),
        grid=(batch, N_GROUPS),
        in_specs=[pl.BlockSpec((1, seq, D_MODEL), lambda b, g: (b, 0, 0)),
                  pl.BlockSpec((1, N_MOD, D_MODEL), lambda b, g: (b, 0, 0)),
                  pl.BlockSpec((2, D_MODEL), lambda b, g: (0, 0)),
                  pl.BlockSpec((1, D_MODEL, width), lambda b, g: (g, 0, 0))],
        out_specs=[pl.BlockSpec((1, 1, seq + QUERY_BLOCK, width), lambda b, g: (b, g, 0, 0)),
                   pl.BlockSpec((1, 1, seq, 2 * GROUP_WIDTH), lambda b, g: (b, g, 0, 0))],
        out_shape=[jax.ShapeDtypeStruct((batch, N_GROUPS, seq + QUERY_BLOCK, width), BF16),
                   jax.ShapeDtypeStruct((batch, N_GROUPS, seq, 2 * GROUP_WIDTH), F32)],
        scratch_shapes=[pltpu.VMEM((seq, D_MODEL), BF16), pltpu.VMEM((6, seq, LANES), F32)],
        compiler_params=_params("arbitrary", "arbitrary"),
        name="qkv_prompt",
    )(x3, mod, gains, w3)


def _head_lane_mask(rows, h):
    lane = lax.broadcasted_iota(jnp.int32, (rows, GROUP_WIDTH), 1)
    return (lane >= h * HEAD_DIM) & (lane < (h + 1) * HEAD_DIM)


def _attn_block(q, k, v, bias_ref, cols):
    masks = [_head_lane_mask(QUERY_BLOCK, h) for h in range(HEADS_PER_GROUP)]
    qs = jnp.concatenate([jnp.where(hm, q, jnp.zeros_like(q)) for hm in masks], axis=0)
    s = _dot_nt(qs, k) + bias_ref[:, cols]
    m = jnp.max(s, axis=-1, keepdims=True)
    p = jnp.exp(s - m)
    l = jnp.sum(p, axis=-1, keepdims=True)
    pv = _dot(p.astype(BF16), v) * (1.0 / l)
    lse_rows = jnp.broadcast_to(m + jnp.log(l), pv.shape)
    o = pv[0:QUERY_BLOCK]
    lse = lse_rows[0:QUERY_BLOCK]
    for h in range(1, HEADS_PER_GROUP):
        rows = slice(h * QUERY_BLOCK, (h + 1) * QUERY_BLOCK)
        o = jnp.where(masks[h], pv[rows], o)
        lse = jnp.where(masks[h], lse_rows[rows], lse)
    return o, lse


def _attn_prompt_kernel(seq, qkv_ref, brow_ref, o_ref, os_ref, ls_ref, bias_ref):
    @pl.when(pl.program_id(0) == 0)
    def _():
        left = lax.broadcasted_iota(jnp.int32, (QUERY_BLOCK, 2 * QUERY_BLOCK), 1) < QUERY_BLOCK
        for g in range(N_GROUPS):
            for h in range(HEADS_PER_GROUP):
                base = jnp.broadcast_to(brow_ref[g, h], (QUERY_BLOCK, 2 * QUERY_BLOCK))
                band = pltpu.roll(base, 0, 1, stride=1, stride_axis=0)
                rows = slice(h * QUERY_BLOCK, (h + 1) * QUERY_BLOCK)
                bias_ref[g, 0, rows, :] = band
                bias_ref[g, 1, rows, :] = jnp.where(left, NEG_INF, band)

    qc = slice(0, GROUP_WIDTH)
    kc = slice(GROUP_WIDTH, 2 * GROUP_WIDTH)
    vc = slice(2 * GROUP_WIDTH, 3 * GROUP_WIDTH)
    n_blocks = seq // QUERY_BLOCK
    for g, (_, dil) in enumerate(DILATED_GROUPS):
        blocks_per_class = n_blocks // dil

        def do_block(blk, carry, g=g, dil=dil, blocks_per_class=blocks_per_class):
            r0 = pl.multiple_of(blk * QUERY_BLOCK, QUERY_BLOCK)
            cur = pl.ds(r0 + QUERY_BLOCK, QUERY_BLOCK)
            q = qkv_ref[0, g, cur, qc]
            rho = blk // blocks_per_class
            in_class = blk % blocks_per_class
            if blocks_per_class == 1:
                o, lse = _attn_block(q, qkv_ref[0, g, cur, kc], qkv_ref[0, g, cur, vc],
                                     bias_ref.at[g, 0], slice(QUERY_BLOCK, 2 * QUERY_BLOCK))
            else:
                both = pl.ds(r0, 2 * QUERY_BLOCK)
                first = jnp.asarray(in_class == 0, jnp.int32)
                o, lse = _attn_block(q, qkv_ref[0, g, both, kc], qkv_ref[0, g, both, vc],
                                     bias_ref.at[g, first], slice(0, 2 * QUERY_BLOCK))
            start = in_class * (QUERY_BLOCK * dil) + rho
            for s in range(2):
                cols = slice(s * LANES, (s + 1) * LANES)
                if dil == 1:
                    os_ref[g, s, pl.ds(r0, QUERY_BLOCK), :] = o[:, cols]
                    ls_ref[g, s, pl.ds(r0, QUERY_BLOCK), :] = lse[:, cols]
                else:
                    os_ref[g, s, pl.ds(start, QUERY_BLOCK, stride=dil), :] = o[:, cols]
                    ls_ref[g, s, pl.ds(start, QUERY_BLOCK, stride=dil), :] = lse[:, cols]
            return carry

        lax.fori_loop(0, n_blocks, do_block, 0, unroll=ATTN_BLOCK_UNROLL)

    def merge(i, carry):
        r0 = pl.multiple_of(i * QUERY_BLOCK, QUERY_BLOCK)
        rows = pl.ds(r0, QUERY_BLOCK)
        for s in range(2):
            ls = [ls_ref[g, s, rows, :] for g in range(N_GROUPS)]
            mx = jnp.maximum(jnp.maximum(ls[0], ls[1]), ls[2])
            es = [jnp.exp(l - mx) for l in ls]
            inv = 1.0 / (es[0] + es[1] + es[2])
            for g in range(N_GROUPS):
                c0 = g * GROUP_WIDTH + s * LANES
                o_ref[0, rows, c0:c0 + LANES] = (os_ref[g, s, rows, :] * (es[g] * inv)).astype(BF16)
        return carry

    lax.fori_loop(0, seq // QUERY_BLOCK, merge, 0)


def _attn_prompt(qkvp, bias_rows):
    batch, _, padded, width = qkvp.shape
    seq = padded - QUERY_BLOCK
    return pl.pallas_call(
        functools.partial(_attn_prompt_kernel, seq),
        grid=(batch,),
        in_specs=[pl.BlockSpec((1, N_GROUPS, padded, width), lambda b: (b, 0, 0, 0)),
                  _resident(bias_rows.shape)],
        out_specs=pl.BlockSpec((1, seq, ATTN_INNER), lambda b: (b, 0, 0)),
        out_shape=jax.ShapeDtypeStruct((batch, seq, ATTN_INNER), BF16),
        scratch_shapes=[pltpu.VMEM((N_GROUPS, 2, seq, LANES), F32), pltpu.VMEM((N_GROUPS, 2, seq, LANES), F32),
                        pltpu.VMEM((N_GROUPS, 2, HEADS_PER_GROUP * QUERY_BLOCK, 2 * QUERY_BLOCK), F32)],
        compiler_params=_params("arbitrary"),
        name="attn_prompt",
    )(qkvp, bias_rows)


def _proj_sample_kernel(nb, x_ref, mod_ref, gains_ref, w_ref, o_ref):
    h = _norm_mod(x_ref[...], gains_ref[0:1, :], mod_ref[0], mod_ref[1], nb).astype(BF16)
    o_ref[...] = _dot(h, w_ref[...])


def _proj_sample(x, mod, gains, w):
    rows = x.shape[0]
    nb = mod.shape[1]
    args = [x, mod, gains, w]
    return pl.pallas_call(
        functools.partial(_proj_sample_kernel, nb),
        grid=(1,),
        in_specs=[_resident(a.shape) for a in args],
        out_specs=pl.BlockSpec((rows, w.shape[1]), lambda i: (0, 0)),
        out_shape=jax.ShapeDtypeStruct((rows, w.shape[1]), F32),
        compiler_params=_params("arbitrary"),
        name="qkv_sample",
    )(*args)


def _split3_bf16(a):
    hi = a.astype(BF16)
    r1 = a - hi.astype(F32)
    mid = r1.astype(BF16)
    lo = (r1 - mid.astype(F32)).astype(BF16)
    return hi, mid, lo


def _attn_sample_kernel(steps, q_ref, kvn_ref, c1_ref, c2_ref, c3_ref, t1_ref, t2_ref, t3_ref, bn_ref,
                        o_ref, n1_ref, n2_ref, n3_ref):
    pairs = HEADS_PER_GROUP // 2
    outs, lses = [], []
    groups = ((c1_ref, t1_ref, n1_ref), (c2_ref, t2_ref, n2_ref), (c3_ref, t3_ref, n3_ref))
    lane = lax.broadcasted_iota(jnp.int32, (SUBLANES, LANES), 1)
    row = lax.broadcasted_iota(jnp.int32, (SUBLANES, LANES), 0)
    own_lanes = lane < HEAD_DIM
    sel_head = jnp.where((lane == row) & (row < steps), 1.0, 0.0).astype(BF16)
    sel_tail = jnp.where((lane == row + (LANES - steps)) & (row < steps), 1.0, 0.0).astype(BF16)
    for g, (c_ref, t_ref, n_ref) in enumerate(groups):
        p_rows = c_ref.shape[-1]
        pieces = _split3_bf16(kvn_ref[0, g])
        new_head = sum(_dot_tn(x, sel_head) for x in pieces)
        new_tail = sum(_dot_tn(x, sel_tail) for x in pieces)
        o_pairs, l_pairs = [], []
        for j in range(pairs):
            q = (q_ref[0, g, j] * (HEAD_DIM ** -0.5)).astype(BF16)
            kt = c_ref[0, 2 * j:2 * j + 2].reshape(LANES, p_rows).astype(BF16)
            vt = c_ref[0, HEADS_PER_GROUP + 2 * j:HEADS_PER_GROUP + 2 * j + 2].reshape(LANES, p_rows).astype(BF16)
            k_new = new_head[j * LANES:(j + 1) * LANES].astype(BF16)
            v_new = new_head[(pairs + j) * LANES:(pairs + j + 1) * LANES].astype(BF16)
            s = _dot(q, kt) + t_ref[j]
            s_new = _dot(q, k_new) + bn_ref[g, j]
            m = jnp.maximum(jnp.max(s, axis=-1, keepdims=True), jnp.max(s_new, axis=-1, keepdims=True))
            p = jnp.exp(s - m)
            p_new = jnp.exp(s_new - m)
            l = jnp.sum(p, axis=-1, keepdims=True) + jnp.sum(p_new, axis=-1, keepdims=True)
            o = _dot_nt(p.astype(BF16), vt) + _dot_nt(p_new.astype(BF16), v_new)
            o_pairs.append(o * (1.0 / l))
            l_pairs.append(m + jnp.log(l))
            for idx in (2 * j, 2 * j + 1, HEADS_PER_GROUP + 2 * j, HEADS_PER_GROUP + 2 * j + 1):
                n_ref[0, idx] = pltpu.roll(c_ref[0, idx], p_rows - steps, 1)
                n_ref[0, idx, :, p_rows - steps:p_rows] = new_tail[idx * HEAD_DIM:(idx + 1) * HEAD_DIM, LANES - steps:LANES]
        outs.append(o_pairs)
        lses.append(l_pairs)
    for j in range(pairs):
        ls = [lses[g][j] for g in range(N_GROUPS)]
        mx = jnp.maximum(jnp.maximum(ls[0], ls[1]), ls[2])
        es = [jnp.exp(l - mx) for l in ls]
        inv = 1.0 / (es[0] + es[1] + es[2])
        for g in range(N_GROUPS):
            og = outs[g][j] * (es[g] * inv)
            c0 = g * GROUP_WIDTH + j * LANES
            o_ref[0, :, c0:c0 + LANES] = jnp.where(own_lanes, og[0:SUBLANES], og[SUBLANES:2 * SUBLANES])


def _attn_sample(steps, q, kv_new, caches, tables, bias_new):
    nb = q.shape[0]
    in_specs = [pl.BlockSpec((1,) + q.shape[1:], lambda b: (b, 0, 0, 0, 0)),
                pl.BlockSpec((1,) + kv_new.shape[1:], lambda b: (b, 0, 0, 0))]
    cache_specs = [pl.BlockSpec((1,) + c.shape[1:], lambda b: (b, 0, 0, 0)) for c in caches]
    in_specs += cache_specs + [_resident(t.shape) for t in tables] + [_resident(bias_new.shape)]
    return pl.pallas_call(
        functools.partial(_attn_sample_kernel, steps),
        grid=(nb,), in_specs=in_specs,
        out_specs=[pl.BlockSpec((1, SUBLANES, ATTN_INNER), lambda b: (b, 0, 0))] + cache_specs,
        out_shape=[jax.ShapeDtypeStruct((nb, SUBLANES, ATTN_INNER), F32)]
                  + [jax.ShapeDtypeStruct(c.shape, F32) for c in caches],
        compiler_params=_params("arbitrary"),
        name="attn_sample",
    )(q, kv_new, *caches, *tables, bias_new)


def _gla_proj_kernel(nb, x_ref, mod_ref, gains_ref, w_ref, wgd_ref, wgu_ref, bg_ref,
                     q_ref, k_ref, v_ref, r_ref, la_ref):
    if nb is None:
        m = mod_ref[0]
        shift, scale = m[0:1], m[1:2]
    else:
        shift, scale = mod_ref[0], mod_ref[1]
    h = _norm_mod(x_ref[...], gains_ref[0:1, :], shift, scale, nb).astype(BF16)
    q_ref[...] = _dot(h, w_ref[:, 0:GLA_QK]) * (GLA_DK ** -0.5)
    k_ref[...] = _dot(h, w_ref[:, GLA_QK:2 * GLA_QK])
    v_ref[...] = _dot(h, w_ref[:, 2 * GLA_QK:2 * GLA_QK + GLA_V]).astype(BF16)
    r_ref[...] = _dot(h, w_ref[:, 2 * GLA_QK + GLA_V:2 * GLA_QK + 2 * GLA_V])
    gd = _dot(h, wgd_ref[...])
    gate = _dot(gd.astype(BF16), wgu_ref[...]) + bg_ref[...]
    la_ref[...] = jax.nn.log_sigmoid(gate) * (1.0 / GATE_TAU)


def _gla_proj(x, mod, gains, weights, nb, tm):
    rows = x.shape[0]
    if nb is None:
        tps = rows // mod.shape[0] // tm
        mod_spec = pl.BlockSpec((1, N_MOD, D_MODEL), lambda i: (i // tps, 0, 0))
    else:
        mod_spec = _resident(mod.shape)
    widths = (GLA_QK, GLA_QK, GLA_V, GLA_V, GLA_QK)
    dtypes = (F32, F32, BF16, F32, F32)
    return pl.pallas_call(
        functools.partial(_gla_proj_kernel, nb),
        grid=(rows // tm,),
        in_specs=[pl.BlockSpec((tm, D_MODEL), lambda i: (i, 0)), mod_spec, _resident((2, D_MODEL))]
                 + [_resident(w.shape) for w in weights],
        out_specs=[pl.BlockSpec((tm, w), lambda i: (i, 0)) for w in widths],
        out_shape=[jax.ShapeDtypeStruct((rows, w), dt) for w, dt in zip(widths, dtypes)],
        compiler_params=_params("arbitrary"),
        name="gla_proj",
    )(x, mod, gains, *weights)


def _cumsum_rows(g):
    rows = g.shape[0]
    row = lax.broadcasted_iota(jnp.int32, g.shape, 0)
    b = g
    shift = 1
    while shift < rows:
        b = b + jnp.where(row >= shift, pltpu.roll(b, shift, 0), 0.0)
        shift *= 2
    return b


def _gla_chunk(q, k, v, g, r, gain, mid, get_state, set_state):
    c = q.shape[0]
    b = _cumsum_rows(g)
    b_end = b[c - 1:c, :]
    b_mid = b[mid:mid + 1, :]
    q_in = (q * jnp.exp(b)).astype(BF16)
    q_rel = (q * jnp.exp(b - b_mid)).astype(BF16)
    k_rel = (k * jnp.exp(b_mid - b)).astype(BF16)
    k_out = (k * jnp.exp(b_end - b)).astype(BF16)
    decay_end = jnp.broadcast_to(jnp.exp(b_end), (SUBLANES, GLA_QK))
    ti = lax.broadcasted_iota(jnp.int32, (c, c), 0)
    si = lax.broadcasted_iota(jnp.int32, (c, c), 1)
    outs = []
    for h in range(GLA_HEADS):
        ks = slice(h * GLA_DK, (h + 1) * GLA_DK)
        vs = slice(h * GLA_DV, (h + 1) * GLA_DV)
        state = get_state(h)
        o = _dot(q_in[:, ks], state.astype(BF16))
        att = jnp.where(si <= ti, _dot_nt(q_rel[:, ks], k_rel[:, ks]), 0.0)
        o = o + _dot(att.astype(BF16), v[:, vs])
        decay_col = decay_end[:, ks].T[:, 0:1]
        set_state(h, decay_col * state + _dot_tn(k_out[:, ks], v[:, vs]))
        outs.append(_rms(o) * gain[:, vs] * _silu(r[:, vs]))
    return jnp.concatenate(outs, axis=-1)


def _gla_prompt_kernel(tt, q_ref, k_ref, v_ref, r_ref, la_ref, gain_ref, a_ref, so_ref, s_ref):
    t = pl.program_id(1)

    @pl.when(t == 0)
    def _():
        s_ref[...] = jnp.zeros_like(s_ref)

    gain = gain_ref[...]

    def get_state(h):
        return s_ref[h]

    def set_state(h, val):
        s_ref[h] = val

    def chunk(ci, carry):
        rows = pl.ds(pl.multiple_of(ci * GLA_CHUNK, GLA_CHUNK), GLA_CHUNK)
        a = _gla_chunk(q_ref[0, rows, :], k_ref[0, rows, :], v_ref[0, rows, :], la_ref[0, rows, :],
                       r_ref[0, rows, :], gain, GLA_CHUNK // 2, get_state, set_state)
        a_ref[0, rows, :] = a.astype(BF16)
        return carry

    lax.fori_loop(0, tt // GLA_CHUNK, chunk, 0, unroll=GLA_CHUNK_UNROLL)

    @pl.when(t == pl.num_programs(1) - 1)
    def _():
        so_ref[0] = s_ref[...]


def _gla_prompt(q, k, v, r, la, gain):
    batch, seq, _ = q.shape
    tt = GLA_TIME_TILE

    def spec(width):
        return pl.BlockSpec((1, tt, width), lambda b, t: (b, t, 0))

    return pl.pallas_call(
        functools.partial(_gla_prompt_kernel, tt),
        grid=(batch, seq // tt),
        in_specs=[spec(GLA_QK), spec(GLA_QK), spec(GLA_V), spec(GLA_V), spec(GLA_QK),
                  pl.BlockSpec((1, GLA_V), lambda b, t: (0, 0))],
        out_specs=[spec(GLA_V), pl.BlockSpec((1, GLA_HEADS, GLA_DK, GLA_DV), lambda b, t: (b, 0, 0, 0))],
        out_shape=[jax.ShapeDtypeStruct((batch, seq, GLA_V), BF16),
                   jax.ShapeDtypeStruct((batch, GLA_HEADS, GLA_DK, GLA_DV), F32)],
        scratch_shapes=[pltpu.VMEM((GLA_HEADS, GLA_DK, GLA_DV), F32)],
        compiler_params=_params("arbitrary", "arbitrary"),
        name="gla_prompt",
    )(q, k, v, r, la, gain)


def _gla_sample_kernel(sb, q_ref, k_ref, v_ref, r_ref, la_ref, gain_ref, s0_ref, a_ref, so_ref):
    gain = gain_ref[...]

    def seq_body(i, carry):
        def get_state(h):
            return s0_ref[i, h]

        def set_state(h, val):
            so_ref[i, h] = val

        a_ref[i] = _gla_chunk(q_ref[i], k_ref[i], v_ref[i], la_ref[i], r_ref[i], gain, 0, get_state, set_state)
        return carry

    lax.fori_loop(0, sb, seq_body, 0)


def _gla_sample(q, k, v, r, la, gain, s0):
    nb, pad, _ = q.shape
    sb = math.gcd(nb, 8)

    def spec(width):
        return pl.BlockSpec((sb, pad, width), lambda i: (i, 0, 0))

    state_spec = pl.BlockSpec((sb, GLA_HEADS, GLA_DK, GLA_DV), lambda i: (i, 0, 0, 0))
    return pl.pallas_call(
        functools.partial(_gla_sample_kernel, sb),
        grid=(nb // sb,),
        in_specs=[spec(GLA_QK), spec(GLA_QK), spec(GLA_V), spec(GLA_V), spec(GLA_QK),
                  pl.BlockSpec((1, GLA_V), lambda i: (0, 0)), state_spec],
        out_specs=[spec(GLA_V), state_spec],
        out_shape=[jax.ShapeDtypeStruct((nb, pad, GLA_V), F32),
                   jax.ShapeDtypeStruct((nb, GLA_HEADS, GLA_DK, GLA_DV), F32)],
        compiler_params=_params("arbitrary"),
        name="gla_sample",
    )(q, k, v, r, la, gain, s0)


def _t5_bucket(dist):
    max_exact = NUM_BUCKETS // 2
    d_f = jnp.maximum(dist, 1).astype(F32)
    large = max_exact + (jnp.log(d_f / max_exact) / math.log(MAX_DISTANCE / max_exact)
                         * (NUM_BUCKETS - max_exact)).astype(jnp.int32)
    large = jnp.minimum(large, NUM_BUCKETS - 1)
    return jnp.where(dist < max_exact, dist, large)


def _group_bias(rel_bias):
    rows = []
    for g, (window, dil) in enumerate(DILATED_GROUPS):
        buckets = _t5_bucket(jnp.arange(window // dil + 1) * dil)
        rows.append(rel_bias[buckets][:, g * HEADS_PER_GROUP:(g + 1) * HEADS_PER_GROUP].T)
    return jnp.stack(rows)


def _prompt_bias_rows(gb):
    band = gb[:, :, ::-1]
    off = jnp.full(gb.shape[:2] + (2 * QUERY_BLOCK - KEYS_PER_QUERY,), NEG_INF, F32)
    return jnp.concatenate([band, off], axis=-1)[:, :, None, :]


def _sample_bias_tables(gb, steps, past_rows):
    tables, new_tables = [], []
    t_idx = np.arange(SUBLANES)[:, None]
    c_idx = np.arange(LANES)[None, :]
    for g, (window, dil) in enumerate(DILATED_GROUPS):
        p = past_rows[g]
        assert p == window == (KEYS_PER_QUERY - 1) * dil
        b = gb[g]
        heads = b.shape[0]
        row0 = b[:, :0:-1]
        if dil > 1:
            gaps = jnp.full((heads, KEYS_PER_QUERY - 1, dil - 1), NEG_INF, F32)
            row0 = jnp.concatenate([row0[:, :, None], gaps], axis=2).reshape(heads, p)
        rows = []
        for t in range(SUBLANES):
            if t < steps:
                rows.append(jnp.concatenate([jnp.full((heads, t), NEG_INF, F32), row0[:, :p - t]], axis=1))
            else:
                rows.append(jnp.full((heads, p), NEG_INF, F32))
        past = jnp.stack(rows, axis=1)
        new = jnp.full((heads, SUBLANES, LANES), NEG_INF, F32)
        for j in range((steps - 1) // dil + 1):
            mask = (t_idx - c_idx == j * dil) & (t_idx < steps) & (c_idx < steps)
            new = jnp.where(jnp.asarray(mask)[None], b[:, j][:, None, None], new)
        tables.append(past.reshape(heads // 2, 2 * SUBLANES, p))
        new_tables.append(new.reshape(heads // 2, 2 * SUBLANES, LANES))
    return tables, jnp.stack(new_tables)


def _ffn_weights(w_in, conv_w, conv_b, w_down):
    return (w_in.astype(BF16), w_down.astype(BF16), conv_w, conv_b.reshape(DEPTH, 1, D_FF))


def _conv_tail_prompt(cs):
    batch = cs.shape[0]
    tail = cs[:, :, SUBLANES - (CONV_WIDTH - 1):, :]
    return jnp.transpose(tail, (0, 2, 1, 3)).reshape(batch, CONV_WIDTH - 1, D_FF)


def _conv_past_sample(state):
    nb = state.shape[0]
    s = state.reshape(nb, CONV_WIDTH - 1, N_FF_CHUNKS, FF_CHUNK)
    return jnp.transpose(s, (2, 1, 0, 3)).reshape(N_FF_CHUNKS, (CONV_WIDTH - 1) * nb, FF_CHUNK)


def _conv_tail_sample(cs, nb):
    s = cs.reshape(N_FF_CHUNKS, CONV_WIDTH - 1, nb, FF_CHUNK)
    return jnp.transpose(s, (2, 1, 0, 3)).reshape(nb, CONV_WIDTH - 1, D_FF)


def kernel(x_prompt, x_sample, state_pool, cache_win_g1, cache_win_g2, cache_win_g3, state_gla, state_ffn_conv,
           c_prompt, c_sample, w_ada, b_ada, norm_gain, final_gain, rel_bias, pool_w, pool_scale,
           attn_w_in, attn_w_out, gla_w_in, gla_w_gate_up, gla_b_gate, gla_norm_gain, gla_w_out,
           ffn_w_in, ffn_conv_w, ffn_conv_b, ffn_w_down):
    batch, seq, _ = x_prompt.shape
    nb, steps, _ = x_sample.shape
    caches = (cache_win_g1, cache_win_g2, cache_win_g3)

    mods = _modulation(jnp.concatenate([c_prompt, c_sample], axis=0), w_ada, b_ada)
    mod_p = mods[:, :batch].reshape(DEPTH, batch, N_MOD, D_MODEL)
    mod_s = jnp.transpose(mods[:, batch:].reshape(DEPTH, nb, N_MOD, D_MODEL), (0, 2, 1, 3))
    fgain = final_gain.reshape(1, D_MODEL)

    xp = x_prompt.reshape(batch * seq, D_MODEL)
    xs = jnp.transpose(x_sample, (1, 0, 2)).reshape(steps * nb, D_MODEL)

    pool_p, pool_s, gla_p, gla_s, conv_p, conv_s = [], [], [], [], [], []
    win_p, win_s = None, None
    ffn_stack = _ffn_weights(ffn_w_in, ffn_conv_w, ffn_conv_b, ffn_w_down)

    for i in range(DEPTH):
        kind, j = i % 3, i // 3
        last = i == DEPTH - 1
        ffn = (i, ffn_stack)
        conv_past = _conv_past_sample(state_ffn_conv[i])
        gains = norm_gain[i]
        if kind == 0:
            mix_w = (pool_w[j].astype(BF16), pool_scale[j].reshape(1, D_MODEL))
            xp, cs, pst = _layer_prompt("pool", last, xp, mod_p[i], gains, fgain, mix_w, ffn)
            pool_p.append(pst[:, POOL_CARRY_ROWS - POOL_STATE_ROWS:])
            past = jnp.transpose(state_pool[j], (1, 0, 2))
            xs, css, psts = _layer_sample("pool", last, xs, mod_s[i], gains, fgain, mix_w + (past,), ffn, conv_past)
            pool_s.append(jnp.transpose(psts, (1, 0, 2)))
        elif kind == 1:
            w = attn_w_in[j]
            w3 = jnp.stack([jnp.concatenate([w[:, s * ATTN_INNER + g * GROUP_WIDTH:
                                                s * ATTN_INNER + (g + 1) * GROUP_WIDTH] for s in range(3)], axis=1)
                            for g in range(N_GROUPS)]).astype(BF16)
            gb = _group_bias(rel_bias)
            wo = attn_w_out[j].astype(BF16)
            qkvp, kvn = _qkv_prompt(xp.reshape(batch, seq, D_MODEL), mod_p[i], gains, w3)
            o_all = _attn_prompt(qkvp, _prompt_bias_rows(gb))
            win_p = [kvn[:, g, seq - min(window, seq):].reshape(1, batch, min(window, seq), 2, HEADS_PER_GROUP, HEAD_DIM)
                     for g, (window, _) in enumerate(DILATED_GROUPS)]
            xp, cs = _layer_prompt("proj", last, xp, mod_p[i], gains, fgain,
                                   (o_all.reshape(batch * seq, ATTN_INNER), wo), ffn)
            qkv_s = _proj_sample(xs, mod_s[i], gains, w.astype(BF16))
            q6 = qkv_s.reshape(steps, nb, 3, N_GROUPS, HEADS_PER_GROUP, HEAD_DIM)
            q_s = jnp.pad(jnp.transpose(q6[:, :, 0], (1, 2, 3, 0, 4)),
                          ((0, 0), (0, 0), (0, 0), (0, SUBLANES - steps), (0, 0)))
            q_s = q_s.reshape(nb, N_GROUPS, HEADS_PER_GROUP // 2, 2, SUBLANES, HEAD_DIM)
            zeros = jnp.zeros_like(q_s[:, :, :, 0])
            q_s = jnp.stack([jnp.concatenate([q_s[:, :, :, 0], zeros], axis=-1),
                             jnp.concatenate([zeros, q_s[:, :, :, 1]], axis=-1)], axis=3)
            q_s = q_s.reshape(nb, N_GROUPS, HEADS_PER_GROUP // 2, 2 * SUBLANES, LANES)
            kv_new = jnp.transpose(q6[:, :, 1:], (1, 3, 0, 2, 4, 5)).reshape(nb, N_GROUPS, steps, 2 * GROUP_WIDTH)
            kv_new = jnp.pad(kv_new, ((0, 0), (0, 0), (0, SUBLANES - steps), (0, 0)))
            past_rows = [c.shape[2] for c in caches]
            cache_t = [jnp.transpose(c[j], (0, 2, 3, 4, 1)).reshape(nb, 2 * HEADS_PER_GROUP, HEAD_DIM, p)
                       for c, p in zip(caches, past_rows)]
            bias_past, bias_new = _sample_bias_tables(gb, steps, past_rows)
            o8, *new_caches = _attn_sample(steps, q_s, kv_new, cache_t, bias_past, bias_new)
            a_s = jnp.transpose(o8[:, :steps], (1, 0, 2)).reshape(steps * nb, ATTN_INNER)
            win_s = [jnp.transpose(c.reshape(nb, 2, HEADS_PER_GROUP, HEAD_DIM, p), (0, 4, 1, 2, 3))[None]
                     for c, p in zip(new_caches, past_rows)]
            xs, css = _layer_sample("proj", last, xs, mod_s[i], gains, fgain, (a_s.astype(BF16), wo), ffn, conv_past)
        else:
            w = gla_w_in[j]
            n_main = 2 * GLA_QK + 2 * GLA_V
            weights = (w[:, :n_main].astype(BF16),
                       jnp.pad(w[:, n_main:], ((0, 0), (0, LANES - GATE_RANK))).astype(BF16),
                       jnp.pad(gla_w_gate_up[j], ((0, LANES - GATE_RANK), (0, 0))).astype(BF16),
                       gla_b_gate[j].reshape(1, GLA_QK))
            gain = gla_norm_gain[j].reshape(1, GLA_V)
            wo = gla_w_out[j].astype(BF16)
            q, k, v, r, la = _gla_proj(xp, mod_p[i], gains, weights, None, PROMPT_ROW_TILE)
            shp = lambda a: a.reshape(batch, seq, a.shape[-1])
            a_p, s_p = _gla_prompt(shp(q), shp(k), shp(v), shp(r), shp(la), gain)
            gla_p.append(s_p)
            xp, cs = _layer_prompt("proj", last, xp, mod_p[i], gains, fgain,
                                   (a_p.reshape(batch * seq, GLA_V), wo), ffn)
            outs = _gla_proj(xs, mod_s[i], gains, weights, nb, steps * nb)

            def per_seq(a):
                a = jnp.transpose(a.reshape(steps, nb, a.shape[-1]), (1, 0, 2))
                return jnp.pad(a, ((0, 0), (0, SAMPLE_DEC_PAD - steps), (0, 0)))

            qs, ks, vs, rs, las = (per_seq(a) for a in outs)
            a16, s_s = _gla_sample(qs, ks, vs, rs, las, gain, state_gla[j])
            gla_s.append(s_s)
            a_s = jnp.transpose(a16[:, :steps], (1, 0, 2)).reshape(steps * nb, GLA_V).astype(BF16)
            xs, css = _layer_sample("proj", last, xs, mod_s[i], gains, fgain, (a_s, wo), ffn, conv_past)
        conv_p.append(_conv_tail_prompt(cs))
        conv_s.append(_conv_tail_sample(css, nb))

    y_prompt = xp.reshape(batch, seq, D_MODEL)
    y_sample = jnp.transpose(xs.reshape(steps, nb, D_MODEL), (1, 0, 2))
    return (y_prompt, y_sample, jnp.stack(pool_p), jnp.stack(pool_s),
            win_p[0], win_s[0], win_p[1], win_s[1], win_p[2], win_s[2],
            jnp.stack(gla_p), jnp.stack(gla_s), jnp.stack(conv_p), jnp.stack(conv_s))
```

```python
import functools
import math

import numpy as np
import jax
import jax.numpy as jnp
from jax import lax
from jax.experimental import pallas as pl
from jax.experimental.pallas import tpu as pltpu

F32 = jnp.float32
BF16 = jnp.bfloat16

D_MODEL = 1024
DEPTH = 4
N_MOD = 6
EPS = 1e-6
NEG_INF = -1e30
POOL_WINDOWS = (2, 4, 8, 16)
POOL_GROUP_DIM = D_MODEL // len(POOL_WINDOWS)
POOL_STATE_ROWS = max(POOL_WINDOWS) - 1
POOL_CARRY_ROWS = 16
DILATED_GROUPS = ((128, 1), (512, 4), (2048, 16))
N_GROUPS = len(DILATED_GROUPS)
HEADS_PER_GROUP = 4
HEAD_DIM = 64
GROUP_WIDTH = HEADS_PER_GROUP * HEAD_DIM
ATTN_INNER = N_GROUPS * GROUP_WIDTH
KEYS_PER_QUERY = 129
QUERY_BLOCK = 128
NUM_BUCKETS = 32
MAX_DISTANCE = 2048
GLA_HEADS = 4
GLA_DK = 128
GLA_DV = 256
GLA_QK = GLA_HEADS * GLA_DK
GLA_V = GLA_HEADS * GLA_DV
GATE_RANK = 16
GATE_TAU = 16.0
GLA_CHUNK = 128
D_FF = 2816
CONV_WIDTH = 3

LANES = 128
SUBLANES = 8
FF_CHUNK = 256
N_FF_CHUNKS = D_FF // FF_CHUNK
VMEM_LIMIT_BYTES = 56 * 1024 * 1024
PROMPT_ROW_TILE = 1024
GLA_TIME_TILE = 512
ATTN_BLOCK_UNROLL = 4
GLA_CHUNK_UNROLL = 4
GLA_SAMPLE_UNROLL = 4
SAMPLE_DEC_PAD = 16


def _params(*semantics):
    return pltpu.CompilerParams(dimension_semantics=semantics, vmem_limit_bytes=VMEM_LIMIT_BYTES)


def _resident(shape):
    nd = len(shape)
    return pl.BlockSpec(shape, lambda *_: (0,) * nd, pipeline_mode=pl.Buffered(1))


def _dot(a, b):
    return jnp.dot(a, b, preferred_element_type=F32)


def _dot_nt(a, b):
    return lax.dot_general(a, b, (((1,), (1,)), ((), ())), preferred_element_type=F32)


def _dot_tn(a, b):
    return lax.dot_general(a, b, (((0,), (0,)), ((), ())), preferred_element_type=F32)


def _rms(x):
    return x * lax.rsqrt(jnp.mean(x * x, axis=-1, keepdims=True) + EPS)


def _bcast_rows(v, y, nb):
    if nb is None:
        return v * y
    rows, width = y.shape
    return (y.reshape(rows // nb, nb, width) * v[None]).reshape(rows, width)


def _norm_mod(x, gain, shift, scale, nb):
    y = _rms(x) * gain
    if nb is None:
        return y * (1.0 + scale) + shift
    rows, width = y.shape
    y3 = y.reshape(rows // nb, nb, width)
    return (y3 * (1.0 + scale)[None] + shift[None]).reshape(rows, width)


def _gelu(x):
    return 0.5 * x * (1.0 + lax.erf(x * (1.0 / math.sqrt(2.0))))


def _silu(x):
    return x * jax.nn.sigmoid(x)


def _split_bf16(a):
    hi = a.astype(BF16)
    lo = (a - hi.astype(F32)).astype(BF16)
    return hi, lo


def _mod_kernel(c_ref, w_ref, b_ref, o_ref):
    a_hi, a_lo = _split_bf16(_silu(c_ref[...]))
    w_hi, w_lo = _split_bf16(w_ref[0])
    o_ref[0] = _dot(a_hi, w_hi) + _dot(a_lo, w_hi) + _dot(a_hi, w_lo) + b_ref[0]


def _modulation(c_all, w_ada, b_ada):
    rows = c_all.shape[0]
    width = N_MOD * D_MODEL
    tn = 1536
    return pl.pallas_call(
        _mod_kernel,
        grid=(DEPTH, width // tn),
        in_specs=[pl.BlockSpec((rows, D_MODEL), lambda l, n: (0, 0)),
                  pl.BlockSpec((1, D_MODEL, tn), lambda l, n: (l, 0, n)),
                  pl.BlockSpec((1, 1, tn), lambda l, n: (l, 0, n))],
        out_specs=pl.BlockSpec((1, rows, tn), lambda l, n: (l, 0, n)),
        out_shape=jax.ShapeDtypeStruct((DEPTH, rows, width), F32),
        compiler_params=_params("parallel", "parallel"),
        name="adaln_mod",
    )(c_all, w_ada, b_ada.reshape(DEPTH, 1, width))


def _ffn_chunk_math(g, g_m1, g_m2, u, cw, cb):
    gc = cw[2:3] * g + cw[1:2] * g_m1 + cw[0:1] * g_m2 + cb
    return (_gelu(gc) * u).astype(BF16)


def _ffn_chunk_weights(win_ref, wd_ref, cw_ref, cb_ref, j):
    cols = slice(j * FF_CHUNK, (j + 1) * FF_CHUNK)
    ucols = slice(D_FF + j * FF_CHUNK, D_FF + (j + 1) * FF_CHUNK)
    return win_ref[0, :, cols], win_ref[0, :, ucols], wd_ref[0, cols, :], cw_ref[0, :, cols], cb_ref[0, :, cols]


def _layer_resident(array, layer):
    nd = array.ndim
    return pl.BlockSpec((1,) + array.shape[1:], lambda *_: (layer,) + (0,) * (nd - 1),
                        pipeline_mode=pl.Buffered(1))


def _layer_prompt_kernel(mixer, last, tm, tiles_per_seq, *refs):
    refs = list(refs)
    x_ref, mod_ref, gains_ref, fg_ref = refs[:4]
    refs = refs[4:]
    if mixer == "pool":
        pw_ref, ps_ref = refs[:2]
    else:
        a_ref, wp_ref = refs[:2]
    win_ref, wd_ref, cw_ref, cb_ref = refs[2:6]
    refs = refs[6:]
    if mixer == "pool":
        y_ref, cs_ref, pst_ref, h2_ref, act_ref, gext_ref, cc_ref, hext_ref = refs
    else:
        y_ref, cs_ref, h2_ref, act_ref, gext_ref, cc_ref = refs

    i = pl.program_id(0)
    tile_in_seq = i % tiles_per_seq

    @pl.when(tile_in_seq == 0)
    def _():
        cc_ref[...] = jnp.zeros_like(cc_ref)
        if mixer == "pool":
            hext_ref[0:POOL_CARRY_ROWS, :] = jnp.zeros((POOL_CARRY_ROWS, D_MODEL), F32)

    x = x_ref[...]
    m = mod_ref[0]
    gains = gains_ref[...]

    if mixer == "pool":
        h = _norm_mod(x, gains[0:1], m[0:1], m[1:2], None)
        hext_ref[POOL_CARRY_ROWS:, :] = h
        pos = tile_in_seq * tm + lax.broadcasted_iota(jnp.int32, (tm, 1), 0)
        parts = []
        for g, w in enumerate(POOL_WINDOWS):
            cols = slice(g * POOL_GROUP_DIM, (g + 1) * POOL_GROUP_DIM)
            s = hext_ref[:, cols]
            span = 1
            while span < w:
                s = s + pltpu.roll(s, span, 0)
                span *= 2
            inv_count = 1.0 / jnp.minimum(pos + 1, w).astype(F32)
            d = s[POOL_CARRY_ROWS:, :] * inv_count - h[:, cols]
            parts.append(_dot(d.astype(BF16), pw_ref[g]))
        mix = jnp.concatenate(parts, axis=-1) * ps_ref[...]
        tail = hext_ref[pl.ds(tm, POOL_CARRY_ROWS), :]
        hext_ref[0:POOL_CARRY_ROWS, :] = tail
        pst_ref[0] = tail
    else:
        mix = _dot(a_ref[...], wp_ref[...])

    x1 = x + m[2:3] * mix
    y_ref[...] = x1
    h2_ref[...] = _norm_mod(x1, gains[1:2], m[3:4], m[4:5], None).astype(BF16)

    def up_proj(j):
        wg, wu, _, _, _ = _ffn_chunk_weights(win_ref, wd_ref, cw_ref, cb_ref, j)
        h2 = h2_ref[...]
        return _dot(h2, wg), _dot(h2, wu)

    ahead = up_proj(0)
    for j in range(N_FF_CHUNKS):
        g, u = ahead
        if j + 1 < N_FF_CHUNKS:
            ahead = up_proj(j + 1)
        _, _, _, cw, cb = _ffn_chunk_weights(win_ref, wd_ref, cw_ref, cb_ref, j)
        gx = gext_ref.at[j % 2]
        gx[0:SUBLANES, :] = cc_ref[j]
        gx[SUBLANES:, :] = g
        act_ref[:, j * FF_CHUNK:(j + 1) * FF_CHUNK] = _ffn_chunk_math(
            g, gx[pl.ds(SUBLANES - 1, tm), :], gx[pl.ds(SUBLANES - 2, tm), :], u, cw, cb)
        tail = g[tm - SUBLANES:tm, :]
        cc_ref[j] = tail
        cs_ref[0, j] = tail
    xo = y_ref[...] + m[5:6] * _dot(act_ref[...], wd_ref[0])
    if last:
        xo = _rms(xo) * fg_ref[...]
    y_ref[...] = xo


def _layer_prompt(mixer, last, x, mod, gains, fgain, mix_args, ffn):
    n = x.shape[0]
    batch = mod.shape[0]
    seq = n // batch
    tm = PROMPT_ROW_TILE
    tps = seq // tm
    in_specs = [pl.BlockSpec((tm, D_MODEL), lambda i: (i, 0)),
                pl.BlockSpec((1, N_MOD, D_MODEL), lambda i: (i // tps, 0, 0)),
                _resident((2, D_MODEL)), _resident((1, D_MODEL))]
    if mixer == "pool":
        pw, ps = mix_args
        in_specs += [_resident(pw.shape), _resident(ps.shape)]
    else:
        a, wp = mix_args
        in_specs += [pl.BlockSpec((tm, a.shape[1]), lambda i: (i, 0)), _resident(wp.shape)]
    layer, ffn_w = ffn
    in_specs += [_layer_resident(w, layer) for w in ffn_w]
    out_shape = [jax.ShapeDtypeStruct((n, D_MODEL), F32),
                 jax.ShapeDtypeStruct((batch, N_FF_CHUNKS, SUBLANES, FF_CHUNK), F32)]
    out_specs = [pl.BlockSpec((tm, D_MODEL), lambda i: (i, 0)),
                 pl.BlockSpec((1, N_FF_CHUNKS, SUBLANES, FF_CHUNK), lambda i: (i // tps, 0, 0, 0))]
    scratch = [pltpu.VMEM((tm, D_MODEL), BF16), pltpu.VMEM((tm, D_FF), BF16),
               pltpu.VMEM((2, tm + SUBLANES, FF_CHUNK), F32), pltpu.VMEM((N_FF_CHUNKS, SUBLANES, FF_CHUNK), F32)]
    if mixer == "pool":
        out_shape.append(jax.ShapeDtypeStruct((batch, POOL_CARRY_ROWS, D_MODEL), F32))
        out_specs.append(pl.BlockSpec((1, POOL_CARRY_ROWS, D_MODEL), lambda i: (i // tps, 0, 0)))
        scratch.append(pltpu.VMEM((tm + POOL_CARRY_ROWS, D_MODEL), F32))
    return pl.pallas_call(
        functools.partial(_layer_prompt_kernel, mixer, last, tm, tps),
        grid=(n // tm,), in_specs=in_specs, out_specs=out_specs, out_shape=out_shape,
        scratch_shapes=scratch, compiler_params=_params("arbitrary"),
        name=f"layer_prompt_{mixer}",
    )(x, mod, gains, fgain, *mix_args, *ffn_w)


def _layer_sample_kernel(mixer, last, nb, steps, *refs):
    refs = list(refs)
    x_ref, mod_ref, gains_ref, fg_ref = refs[:4]
    refs = refs[4:]
    if mixer == "pool":
        pw_ref, ps_ref, ppast_ref = refs[:3]
        refs = refs[3:]
    else:
        a_ref, wp_ref = refs[:2]
        refs = refs[2:]
    win_ref, wd_ref, cw_ref, cb_ref, cpast_ref = refs[:5]
    refs = refs[5:]
    if mixer == "pool":
        y_ref, cs_ref, pst_ref, h2_ref, act_ref = refs
    else:
        y_ref, cs_ref, h2_ref, act_ref = refs
    rows = steps * nb

    x = x_ref[...]
    gains = gains_ref[...]
    if mixer == "pool":
        h = _norm_mod(x, gains[0:1], mod_ref[0], mod_ref[1], nb)
        new = [h[t * nb:(t + 1) * nb, :] for t in range(steps)]

        def u_rows(p, cols):
            if p < POOL_STATE_ROWS:
                return ppast_ref[p, :, cols]
            return new[p - POOL_STATE_ROWS][:, cols]

        parts = []
        for g, w in enumerate(POOL_WINDOWS):
            cols = slice(g * POOL_GROUP_DIM, (g + 1) * POOL_GROUP_DIM)
            ds = []
            for t in range(steps):
                s = u_rows(POOL_STATE_ROWS + t, cols)
                for k in range(1, w):
                    s = s + u_rows(POOL_STATE_ROWS + t - k, cols)
                ds.append(s * (1.0 / w) - new[t][:, cols])
            parts.append(_dot(jnp.concatenate(ds, axis=0).astype(BF16), pw_ref[g]))
        mix = jnp.concatenate(parts, axis=-1) * ps_ref[...]
        full = slice(0, D_MODEL)
        for p in range(POOL_STATE_ROWS):
            pst_ref[p] = u_rows(p + steps, full)
    else:
        mix = _dot(a_ref[...], wp_ref[...])

    x1 = x + _bcast_rows(mod_ref[2], mix, nb)
    y_ref[...] = x1
    h2_ref[...] = _norm_mod(x1, gains[1:2], mod_ref[3], mod_ref[4], nb).astype(BF16)
    past_rows = (CONV_WIDTH - 1) * nb

    for j in range(N_FF_CHUNKS):
        h2 = h2_ref[...]
        wg, wu, _, cw, cb = _ffn_chunk_weights(win_ref, wd_ref, cw_ref, cb_ref, j)
        g = _dot(h2, wg)
        u = _dot(h2, wu)
        gall = jnp.concatenate([cpast_ref[j], g], axis=0)
        act_ref[:, j * FF_CHUNK:(j + 1) * FF_CHUNK] = _ffn_chunk_math(
            g, gall[nb:nb + rows, :], gall[0:rows, :], u, cw, cb)
        cs_ref[j] = gall[rows:rows + past_rows, :]
    xo = y_ref[...] + _bcast_rows(mod_ref[5], _dot(act_ref[...], wd_ref[0]), nb)
    if last:
        xo = _rms(xo) * fg_ref[...]
    y_ref[...] = xo


def _layer_sample(mixer, last, x, mod, gains, fgain, mix_args, ffn, conv_past):
    rows = x.shape[0]
    nb = mod.shape[1]
    steps = rows // nb
    layer, ffn_w = ffn
    head = [x, mod, gains, fgain, *mix_args]
    args = [*head, *ffn_w, conv_past]
    out_shape = [jax.ShapeDtypeStruct((rows, D_MODEL), F32),
                 jax.ShapeDtypeStruct(conv_past.shape, F32)]
    if mixer == "pool":
        out_shape.append(jax.ShapeDtypeStruct((POOL_STATE_ROWS, nb, D_MODEL), F32))
    return pl.pallas_call(
        functools.partial(_layer_sample_kernel, mixer, last, nb, steps),
        grid=(1,),
        in_specs=[_resident(a.shape) for a in head] + [_layer_resident(w, layer) for w in ffn_w]
                 + [_resident(conv_past.shape)],
        out_specs=[pl.BlockSpec(s.shape, functools.partial(lambda nd, i: (0,) * nd, len(s.shape))) for s in out_shape],
        out_shape=out_shape,
        scratch_shapes=[pltpu.VMEM((rows, D_MODEL), BF16), pltpu.VMEM((rows, D_FF), BF16)],
        compiler_params=_params("arbitrary"),
        name=f"layer_sample_{mixer}",
    )(*args)


def _qkv_prompt_kernel(seq, x_ref, mod_ref, gains_ref, w_ref, qkvp_ref, kvn_ref, h_ref, slab_ref):
    g = pl.program_id(1)

    @pl.when(g == 0)
    def _():
        m = mod_ref[0]
        h_ref[...] = _norm_mod(x_ref[0], gains_ref[0:1, :], m[0:1], m[1:2], None).astype(BF16)

    h = h_ref[...]
    for c in range(3):
        r = _dot(h, w_ref[0, :, c * GROUP_WIDTH:(c + 1) * GROUP_WIDTH])
        if c == 0:
            r = r * (HEAD_DIM ** -0.5)
        else:
            kvn_ref[0, 0, :, (c - 1) * GROUP_WIDTH:c * GROUP_WIDTH] = r
        slab_ref[2 * c] = r[:, 0:LANES]
        slab_ref[2 * c + 1] = r[:, LANES:2 * LANES]

    qkvp_ref[0, 0, 0:QUERY_BLOCK, :] = jnp.zeros((QUERY_BLOCK, 3 * GROUP_WIDTH), BF16)
    for gi, (_, dil) in enumerate(DILATED_GROUPS):
        @pl.when(g == gi)
        def _(dil=dil):
            per_class = seq // dil
            for s in range(6):
                for rho in range(dil):
                    if dil == 1:
                        v = slab_ref[s]
                    else:
                        v = slab_ref[s, pl.ds(rho, per_class, stride=dil), :]
                    r0 = QUERY_BLOCK + rho * per_class
                    qkvp_ref[0, 0, r0:r0 + per_class, s * LANES:(s + 1) * LANES] = v.astype(BF16)


def _qkv_prompt(x3, mod, gains, w3):
    batch, seq, _ = x3.shape
    width = 3 * GROUP_WIDTH
    return pl.pallas_call(
        functools.partial(_qkv_prompt_kernel, seq),
        grid=(batch, N_GROUPS),
        in_specs=[pl.BlockSpec((1, seq, D_MODEL), lambda b, g: (b, 0, 0)),
                  pl.BlockSpec((1, N_MOD, D_MODEL), lambda b, g: (b, 0, 0)),
                  pl.BlockSpec((2, D_MODEL), lambda b, g: (0, 0)),
                  pl.BlockSpec((1, D_MODEL, width), lambda b, g: (g, 0, 0))],
        out_specs=[pl.BlockSpec((1, 1, seq + QUERY_BLOCK, width), lambda b, g: (b, g, 0, 0)),
                   pl.BlockSpec((1, 1, seq, 2 * GROUP_WIDTH), lambda b, g: (b, g, 0, 0))],
        out_shape=[jax.ShapeDtypeStruct((batch, N_GROUPS, seq + QUERY_BLOCK, width), BF16),
                   jax.ShapeDtypeStruct((batch, N_GROUPS, seq, 2 * GROUP_WIDTH), F32)],
        scratch_shapes=[pltpu.VMEM((seq, D_MODEL), BF16), pltpu.VMEM((6, seq, LANES), F32)],
        compiler_params=_params("arbitrary", "arbitrary"),
        name="qkv_prompt",
    )(x3, mod, gains, w3)


def _head_lane_mask(rows, h):
    lane = lax.broadcasted_iota(jnp.int32, (rows, GROUP_WIDTH), 1)
    return (lane >= h * HEAD_DIM) & (lane < (h + 1) * HEAD_DIM)


def _attn_block(q, k, v, bias_ref, cols):
    masks = [_head_lane_mask(QUERY_BLOCK, h) for h in range(HEADS_PER_GROUP)]
    qs = jnp.concatenate([jnp.where(hm, q, jnp.zeros_like(q)) for hm in masks], axis=0)
    s = _dot_nt(qs, k) + bias_ref[:, cols]
    m = jnp.max(s, axis=-1, keepdims=True)
    p = jnp.exp(s - m)
    l = jnp.sum(p, axis=-1, keepdims=True)
    pv = _dot(p.astype(BF16), v) * (1.0 / l)
    lse_rows = jnp.broadcast_to(m + jnp.log(l), pv.shape)
    o = pv[0:QUERY_BLOCK]
    lse = lse_rows[0:QUERY_BLOCK]
    for h in range(1, HEADS_PER_GROUP):
        rows = slice(h * QUERY_BLOCK, (h + 1) * QUERY_BLOCK)
        o = jnp.where(masks[h], pv[rows], o)
        lse = jnp.where(masks[h], lse_rows[rows], lse)
    return o, lse


def _attn_prompt_kernel(seq, qkv_ref, brow_ref, o_ref, os_ref, ls_ref, bias_ref):
    @pl.when(pl.program_id(0) == 0)
    def _():
        left = lax.broadcasted_iota(jnp.int32, (QUERY_BLOCK, 2 * QUERY_BLOCK), 1) < QUERY_BLOCK
        for g in range(N_GROUPS):
            for h in range(HEADS_PER_GROUP):
                base = jnp.broadcast_to(brow_ref[g, h], (QUERY_BLOCK, 2 * QUERY_BLOCK))
                band = pltpu.roll(base, 0, 1, stride=1, stride_axis=0)
                rows = slice(h * QUERY_BLOCK, (h + 1) * QUERY_BLOCK)
                bias_ref[g, 0, rows, :] = band
                bias_ref[g, 1, rows, :] = jnp.where(left, NEG_INF, band)

    qc = slice(0, GROUP_WIDTH)
    kc = slice(GROUP_WIDTH, 2 * GROUP_WIDTH)
    vc = slice(2 * GROUP_WIDTH, 3 * GROUP_WIDTH)
    n_blocks = seq // QUERY_BLOCK
    for g, (_, dil) in enumerate(DILATED_GROUPS):
        blocks_per_class = n_blocks // dil

        def do_block(blk, carry, g=g, dil=dil, blocks_per_class=blocks_per_class):
            r0 = pl.multiple_of(blk * QUERY_BLOCK, QUERY_BLOCK)
            cur = pl.ds(r0 + QUERY_BLOCK, QUERY_BLOCK)
            q = qkv_ref[0, g, cur, qc]
            rho = blk // blocks_per_class
            in_class = blk % blocks_per_class
            if blocks_per_class == 1:
                o, lse = _attn_block(q, qkv_ref[0, g, cur, kc], qkv_ref[0, g, cur, vc],
                                     bias_ref.at[g, 0], slice(QUERY_BLOCK, 2 * QUERY_BLOCK))
            else:
                both = pl.ds(r0, 2 * QUERY_BLOCK)
                first = jnp.asarray(in_class == 0, jnp.int32)
                o, lse = _attn_block(q, qkv_ref[0, g, both, kc], qkv_ref[0, g, both, vc],
                                     bias_ref.at[g, first], slice(0, 2 * QUERY_BLOCK))
            start = in_class * (QUERY_BLOCK * dil) + rho
            for s in range(2):
                cols = slice(s * LANES, (s + 1) * LANES)
                if dil == 1:
                    os_ref[g, s, pl.ds(r0, QUERY_BLOCK), :] = o[:, cols]
                    ls_ref[g, s, pl.ds(r0, QUERY_BLOCK), :] = lse[:, cols]
                else:
                    os_ref[g, s, pl.ds(start, QUERY_BLOCK, stride=dil), :] = o[:, cols]
                    ls_ref[g, s, pl.ds(start, QUERY_BLOCK, stride=dil), :] = lse[:, cols]
            return carry

        lax.fori_loop(0, n_blocks, do_block, 0, unroll=ATTN_BLOCK_UNROLL)

    def merge(i, carry):
        r0 = pl.multiple_of(i * QUERY_BLOCK, QUERY_BLOCK)
        rows = pl.ds(r0, QUERY_BLOCK)
        for s in range(2):
            ls = [ls_ref[g, s, rows, :] for g in range(N_GROUPS)]
            mx = jnp.maximum(jnp.maximum(ls[0], ls[1]), ls[2])
            es = [jnp.exp(l - mx) for l in ls]
            inv = 1.0 / (es[0] + es[1] + es[2])
            for g in range(N_GROUPS):
                c0 = g * GROUP_WIDTH + s * LANES
                o_ref[0, rows, c0:c0 + LANES] = (os_ref[g, s, rows, :] * (es[g] * inv)).astype(BF16)
        return carry

    lax.fori_loop(0, seq // QUERY_BLOCK, merge, 0)


def _attn_prompt(qkvp, bias_rows):
    batch, _, padded, width = qkvp.shape
    seq = padded - QUERY_BLOCK
    return pl.pallas_call(
        functools.partial(_attn_prompt_kernel, seq),
        grid=(batch,),
        in_specs=[pl.BlockSpec((1, N_GROUPS, padded, width), lambda b: (b, 0, 0, 0)),
                  _resident(bias_rows.shape)],
        out_specs=pl.BlockSpec((1, seq, ATTN_INNER), lambda b: (b, 0, 0)),
        out_shape=jax.ShapeDtypeStruct((batch, seq, ATTN_INNER), BF16),
        scratch_shapes=[pltpu.VMEM((N_GROUPS, 2, seq, LANES), F32), pltpu.VMEM((N_GROUPS, 2, seq, LANES), F32),
                        pltpu.VMEM((N_GROUPS, 2, HEADS_PER_GROUP * QUERY_BLOCK, 2 * QUERY_BLOCK), F32)],
        compiler_params=_params("arbitrary"),
        name="attn_prompt",
    )(qkvp, bias_rows)


def _proj_sample_kernel(nb, x_ref, mod_ref, gains_ref, w_ref, o_ref):
    h = _norm_mod(x_ref[...], gains_ref[0:1, :], mod_ref[0], mod_ref[1], nb).astype(BF16)
    o_ref[...] = _dot(h, w_ref[...])


def _proj_sample(x, mod, gains, w):
    rows = x.shape[0]
    nb = mod.shape[1]
    args = [x, mod, gains, w]
    return pl.pallas_call(
        functools.partial(_proj_sample_kernel, nb),
        grid=(1,),
        in_specs=[_resident(a.shape) for a in args],
        out_specs=pl.BlockSpec((rows, w.shape[1]), lambda i: (0, 0)),
        out_shape=jax.ShapeDtypeStruct((rows, w.shape[1]), F32),
        compiler_params=_params("arbitrary"),
        name="qkv_sample",
    )(*args)


def _split3_bf16(a):
    hi = a.astype(BF16)
    r1 = a - hi.astype(F32)
    mid = r1.astype(BF16)
    lo = (r1 - mid.astype(F32)).astype(BF16)
    return hi, mid, lo


def _attn_sample_kernel(steps, q_ref, kvn_ref, c1_ref, c2_ref, c3_ref, t1_ref, t2_ref, t3_ref, bn_ref,
                        o_ref, n1_ref, n2_ref, n3_ref):
    pairs = HEADS_PER_GROUP // 2
    outs, lses = [], []
    groups = ((c1_ref, t1_ref, n1_ref), (c2_ref, t2_ref, n2_ref), (c3_ref, t3_ref, n3_ref))
    lane = lax.broadcasted_iota(jnp.int32, (SUBLANES, LANES), 1)
    row = lax.broadcasted_iota(jnp.int32, (SUBLANES, LANES), 0)
    own_lanes = lane < HEAD_DIM
    sel_head = jnp.where((lane == row) & (row < steps), 1.0, 0.0).astype(BF16)
    sel_tail = jnp.where((lane == row + (LANES - steps)) & (row < steps), 1.0, 0.0).astype(BF16)
    for g, (c_ref, t_ref, n_ref) in enumerate(groups):
        p_rows = c_ref.shape[-1]
        pieces = _split3_bf16(kvn_ref[0, g])
        new_head = sum(_dot_tn(x, sel_head) for x in pieces)
        new_tail = sum(_dot_tn(x, sel_tail) for x in pieces)
        o_pairs, l_pairs = [], []
        for j in range(pairs):
            q = (q_ref[0, g, j] * (HEAD_DIM ** -0.5)).astype(BF16)
            kt = c_ref[0, 2 * j:2 * j + 2].reshape(LANES, p_rows).astype(BF16)
            vt = c_ref[0, HEADS_PER_GROUP + 2 * j:HEADS_PER_GROUP + 2 * j + 2].reshape(LANES, p_rows).astype(BF16)
            k_new = new_head[j * LANES:(j + 1) * LANES].astype(BF16)
            v_new = new_head[(pairs + j) * LANES:(pairs + j + 1) * LANES].astype(BF16)
            s = _dot(q, kt) + t_ref[j]
            s_new = _dot(q, k_new) + bn_ref[g, j]
            m = jnp.maximum(jnp.max(s, axis=-1, keepdims=True), jnp.max(s_new, axis=-1, keepdims=True))
            p = jnp.exp(s - m)
            p_new = jnp.exp(s_new - m)
            l = jnp.sum(p, axis=-1, keepdims=True) + jnp.sum(p_new, axis=-1, keepdims=True)
            o = _dot_nt(p.astype(BF16), vt) + _dot_nt(p_new.astype(BF16), v_new)
            o_pairs.append(o * (1.0 / l))
            l_pairs.append(m + jnp.log(l))
            for idx in (2 * j, 2 * j + 1, HEADS_PER_GROUP + 2 * j, HEADS_PER_GROUP + 2 * j + 1):
                n_ref[0, idx] = pltpu.roll(c_ref[0, idx], p_rows - steps, 1)
                n_ref[0, idx, :, p_rows - steps:p_rows] = new_tail[idx * HEAD_DIM:(idx + 1) * HEAD_DIM, LANES - steps:LANES]
        outs.append(o_pairs)
        lses.append(l_pairs)
    for j in range(pairs):
        ls = [lses[g][j] for g in range(N_GROUPS)]
        mx = jnp.maximum(jnp.maximum(ls[0], ls[1]), ls[2])
        es = [jnp.exp(l - mx) for l in ls]
        inv = 1.0 / (es[0] + es[1] + es[2])
        for g in range(N_GROUPS):
            og = outs[g][j] * (es[g] * inv)
            c0 = g * GROUP_WIDTH + j * LANES
            o_ref[0, :, c0:c0 + LANES] = jnp.where(own_lanes, og[0:SUBLANES], og[SUBLANES:2 * SUBLANES])


def _attn_sample(steps, q, kv_new, caches, tables, bias_new):
    nb = q.shape[0]
    in_specs = [pl.BlockSpec((1,) + q.shape[1:], lambda b: (b, 0, 0, 0, 0)),
                pl.BlockSpec((1,) + kv_new.shape[1:], lambda b: (b, 0, 0, 0))]
    cache_specs = [pl.BlockSpec((1,) + c.shape[1:], lambda b: (b, 0, 0, 0)) for c in caches]
    in_specs += cache_specs + [_resident(t.shape) for t in tables] + [_resident(bias_new.shape)]
    return pl.pallas_call(
        functools.partial(_attn_sample_kernel, steps),
        grid=(nb,), in_specs=in_specs,
        out_specs=[pl.BlockSpec((1, SUBLANES, ATTN_INNER), lambda b: (b, 0, 0))] + cache_specs,
        out_shape=[jax.ShapeDtypeStruct((nb, SUBLANES, ATTN_INNER), F32)]
                  + [jax.ShapeDtypeStruct(c.shape, F32) for c in caches],
        compiler_params=_params("arbitrary"),
        name="attn_sample",
    )(q, kv_new, *caches, *tables, bias_new)


def _gla_proj_kernel(nb, x_ref, mod_ref, gains_ref, w_ref, wgd_ref, wgu_ref, bg_ref,
                     q_ref, k_ref, v_ref, r_ref, la_ref):
    if nb is None:
        m = mod_ref[0]
        shift, scale = m[0:1], m[1:2]
    else:
        shift, scale = mod_ref[0], mod_ref[1]
    h = _norm_mod(x_ref[...], gains_ref[0:1, :], shift, scale, nb).astype(BF16)
    q_ref[...] = _dot(h, w_ref[:, 0:GLA_QK]) * (GLA_DK ** -0.5)
    k_ref[...] = _dot(h, w_ref[:, GLA_QK:2 * GLA_QK])
    v_ref[...] = _dot(h, w_ref[:, 2 * GLA_QK:2 * GLA_QK + GLA_V]).astype(BF16)
    r_ref[...] = _dot(h, w_ref[:, 2 * GLA_QK + GLA_V:2 * GLA_QK + 2 * GLA_V])
    gd = _dot(h, wgd_ref[...])
    gate = _dot(gd.astype(BF16), wgu_ref[...]) + bg_ref[...]
    la_ref[...] = jax.nn.log_sigmoid(gate) * (1.0 / GATE_TAU)


def _gla_proj(x, mod, gains, weights, nb, tm):
    rows = x.shape[0]
    if nb is None:
        tps = rows // mod.shape[0] // tm
        mod_spec = pl.BlockSpec((1, N_MOD, D_MODEL), lambda i: (i // tps, 0, 0))
    else:
        mod_spec = _resident(mod.shape)
    widths = (GLA_QK, GLA_QK, GLA_V, GLA_V, GLA_QK)
    dtypes = (F32, F32, BF16, F32, F32)
    return pl.pallas_call(
        functools.partial(_gla_proj_kernel, nb),
        grid=(rows // tm,),
        in_specs=[pl.BlockSpec((tm, D_MODEL), lambda i: (i, 0)), mod_spec, _resident((2, D_MODEL))]
                 + [_resident(w.shape) for w in weights],
        out_specs=[pl.BlockSpec((tm, w), lambda i: (i, 0)) for w in widths],
        out_shape=[jax.ShapeDtypeStruct((rows, w), dt) for w, dt in zip(widths, dtypes)],
        compiler_params=_params("arbitrary"),
        name="gla_proj",
    )(x, mod, gains, *weights)


def _cumsum_rows(g):
    rows = g.shape[0]
    row = lax.broadcasted_iota(jnp.int32, g.shape, 0)
    b = g
    shift = 1
    while shift < rows:
        b = b + jnp.where(row >= shift, pltpu.roll(b, shift, 0), 0.0)
        shift *= 2
    return b


def _gla_chunk(q, k, v, g, r, gain, mid, get_state, set_state):
    c = q.shape[0]
    b = _cumsum_rows(g)
    b_end = b[c - 1:c, :]
    b_mid = b[mid:mid + 1, :]
    q_in = (q * jnp.exp(b)).astype(BF16)
    q_rel = (q * jnp.exp(b - b_mid)).astype(BF16)
    k_rel = (k * jnp.exp(b_mid - b)).astype(BF16)
    k_out = (k * jnp.exp(b_end - b)).astype(BF16)
    decay_end = jnp.broadcast_to(jnp.exp(b_end), (SUBLANES, GLA_QK))
    ti = lax.broadcasted_iota(jnp.int32, (c, c), 0)
    si = lax.broadcasted_iota(jnp.int32, (c, c), 1)
    outs = []
    for h in range(GLA_HEADS):
        ks = slice(h * GLA_DK, (h + 1) * GLA_DK)
        vs = slice(h * GLA_DV, (h + 1) * GLA_DV)
        state = get_state(h)
        o = _dot(q_in[:, ks], state.astype(BF16))
        att = jnp.where(si <= ti, _dot_nt(q_rel[:, ks], k_rel[:, ks]), 0.0)
        o = o + _dot(att.astype(BF16), v[:, vs])
        decay_col = decay_end[:, ks].T[:, 0:1]
        set_state(h, decay_col * state + _dot_tn(k_out[:, ks], v[:, vs]))
        outs.append(_rms(o) * gain[:, vs] * _silu(r[:, vs]))
    return jnp.concatenate(outs, axis=-1)


def _gla_prompt_kernel(tt, q_ref, k_ref, v_ref, r_ref, la_ref, gain_ref, a_ref, so_ref, s_ref):
    t = pl.program_id(1)

    @pl.when(t == 0)
    def _():
        s_ref[...] = jnp.zeros_like(s_ref)

    gain = gain_ref[...]

    def get_state(h):
        return s_ref[h]

    def set_state(h, val):
        s_ref[h] = val

    def chunk(ci, carry):
        rows = pl.ds(pl.multiple_of(ci * GLA_CHUNK, GLA_CHUNK), GLA_CHUNK)
        a = _gla_chunk(q_ref[0, rows, :], k_ref[0, rows, :], v_ref[0, rows, :], la_ref[0, rows, :],
                       r_ref[0, rows, :], gain, GLA_CHUNK // 2, get_state, set_state)
        a_ref[0, rows, :] = a.astype(BF16)
        return carry

    lax.fori_loop(0, tt // GLA_CHUNK, chunk, 0, unroll=GLA_CHUNK_UNROLL)

    @pl.when(t == pl.num_programs(1) - 1)
    def _():
        so_ref[0] = s_ref[...]


def _gla_prompt(q, k, v, r, la, gain):
    batch, seq, _ = q.shape
    tt = GLA_TIME_TILE

    def spec(width):
        return pl.BlockSpec((1, tt, width), lambda b, t: (b, t, 0))

    return pl.pallas_call(
        functools.partial(_gla_prompt_kernel, tt),
        grid=(batch, seq // tt),
        in_specs=[spec(GLA_QK), spec(GLA_QK), spec(GLA_V), spec(GLA_V), spec(GLA_QK),
                  pl.BlockSpec((1, GLA_V), lambda b, t: (0, 0))],
        out_specs=[spec(GLA_V), pl.BlockSpec((1, GLA_HEADS, GLA_DK, GLA_DV), lambda b, t: (b, 0, 0, 0))],
        out_shape=[jax.ShapeDtypeStruct((batch, seq, GLA_V), BF16),
                   jax.ShapeDtypeStruct((batch, GLA_HEADS, GLA_DK, GLA_DV), F32)],
        scratch_shapes=[pltpu.VMEM((GLA_HEADS, GLA_DK, GLA_DV), F32)],
        compiler_params=_params("arbitrary", "arbitrary"),
        name="gla_prompt",
    )(q, k, v, r, la, gain)


def _gla_sample_kernel(sb, q_ref, k_ref, v_ref, r_ref, la_ref, gain_ref, s0_ref, a_ref, so_ref):
    gain = gain_ref[...]

    def seq_body(i, carry):
        def get_state(h):
            return s0_ref[i, h]

        def set_state(h, val):
            so_ref[i, h] = val

        a_ref[i] = _gla_chunk(q_ref[i], k_ref[i], v_ref[i], la_ref[i], r_ref[i], gain, 0, get_state, set_state)
        return carry

    lax.fori_loop(0, sb, seq_body, 0, unroll=GLA_SAMPLE_UNROLL)


def _gla_sample(q, k, v, r, la, gain, s0):
    nb, pad, _ = q.shape
    sb = math.gcd(nb, 8)

    def spec(width):
        return pl.BlockSpec((sb, pad, width), lambda i: (i, 0, 0))

    state_spec = pl.BlockSpec((sb, GLA_HEADS, GLA_DK, GLA_DV), lambda i: (i, 0, 0, 0))
    return pl.pallas_call(
        functools.partial(_gla_sample_kernel, sb),
        grid=(nb // sb,),
        in_specs=[spec(GLA_QK), spec(GLA_QK), spec(GLA_V), spec(GLA_V), spec(GLA_QK),
                  pl.BlockSpec((1, GLA_V), lambda i: (0, 0)), state_spec],
        out_specs=[spec(GLA_V), state_spec],
        out_shape=[jax.ShapeDtypeStruct((nb, pad, GLA_V), F32),
                   jax.ShapeDtypeStruct((nb, GLA_HEADS, GLA_DK, GLA_DV), F32)],
        compiler_params=_params("arbitrary"),
        name="gla_sample",
    )(q, k, v, r, la, gain, s0)


def _t5_bucket(dist):
    max_exact = NUM_BUCKETS // 2
    d_f = jnp.maximum(dist, 1).astype(F32)
    large = max_exact + (jnp.log(d_f / max_exact) / math.log(MAX_DISTANCE / max_exact)
                         * (NUM_BUCKETS - max_exact)).astype(jnp.int32)
    large = jnp.minimum(large, NUM_BUCKETS - 1)
    return jnp.where(dist < max_exact, dist, large)


def _group_bias(rel_bias):
    rows = []
    for g, (window, dil) in enumerate(DILATED_GROUPS):
        buckets = _t5_bucket(jnp.arange(window // dil + 1) * dil)
        rows.append(rel_bias[buckets][:, g * HEADS_PER_GROUP:(g + 1) * HEADS_PER_GROUP].T)
    return jnp.stack(rows)


def _prompt_bias_rows(gb):
    band = gb[:, :, ::-1]
    off = jnp.full(gb.shape[:2] + (2 * QUERY_BLOCK - KEYS_PER_QUERY,), NEG_INF, F32)
    return jnp.concatenate([band, off], axis=-1)[:, :, None, :]


def _sample_bias_tables(gb, steps, past_rows):
    tables, new_tables = [], []
    t_idx = np.arange(SUBLANES)[:, None]
    c_idx = np.arange(LANES)[None, :]
    for g, (window, dil) in enumerate(DILATED_GROUPS):
        p = past_rows[g]
        assert p == window == (KEYS_PER_QUERY - 1) * dil
        b = gb[g]
        heads = b.shape[0]
        row0 = b[:, :0:-1]
        if dil > 1:
            gaps = jnp.full((heads, KEYS_PER_QUERY - 1, dil - 1), NEG_INF, F32)
            row0 = jnp.concatenate([row0[:, :, None], gaps], axis=2).reshape(heads, p)
        rows = []
        for t in range(SUBLANES):
            if t < steps:
                rows.append(jnp.concatenate([jnp.full((heads, t), NEG_INF, F32), row0[:, :p - t]], axis=1))
            else:
                rows.append(jnp.full((heads, p), NEG_INF, F32))
        past = jnp.stack(rows, axis=1)
        new = jnp.full((heads, SUBLANES, LANES), NEG_INF, F32)
        for j in range((steps - 1) // dil + 1):
            mask = (t_idx - c_idx == j * dil) & (t_idx < steps) & (c_idx < steps)
            new = jnp.where(jnp.asarray(mask)[None], b[:, j][:, None, None], new)
        tables.append(past.reshape(heads // 2, 2 * SUBLANES, p))
        new_tables.append(new.reshape(heads // 2, 2 * SUBLANES, LANES))
    return tables, jnp.stack(new_tables)


def _ffn_weights(w_in, conv_w, conv_b, w_down):
    return (w_in.astype(BF16), w_down.astype(BF16), conv_w, conv_b.reshape(DEPTH, 1, D_FF))


def _conv_tail_prompt(cs):
    batch = cs.shape[0]
    tail = cs[:, :, SUBLANES - (CONV_WIDTH - 1):, :]
    return jnp.transpose(tail, (0, 2, 1, 3)).reshape(batch, CONV_WIDTH - 1, D_FF)


def _conv_past_sample(state):
    nb = state.shape[0]
    s = state.reshape(nb, CONV_WIDTH - 1, N_FF_CHUNKS, FF_CHUNK)
    return jnp.transpose(s, (2, 1, 0, 3)).reshape(N_FF_CHUNKS, (CONV_WIDTH - 1) * nb, FF_CHUNK)


def _conv_tail_sample(cs, nb):
    s = cs.reshape(N_FF_CHUNKS, CONV_WIDTH - 1, nb, FF_CHUNK)
    return jnp.transpose(s, (2, 1, 0, 3)).reshape(nb, CONV_WIDTH - 1, D_FF)


def kernel(x_prompt, x_sample, state_pool, cache_win_g1, cache_win_g2, cache_win_g3, state_gla, state_ffn_conv,
           c_prompt, c_sample, w_ada, b_ada, norm_gain, final_gain, rel_bias, pool_w, pool_scale,
           attn_w_in, attn_w_out, gla_w_in, gla_w_gate_up, gla_b_gate, gla_norm_gain, gla_w_out,
           ffn_w_in, ffn_conv_w, ffn_conv_b, ffn_w_down):
    batch, seq, _ = x_prompt.shape
    nb, steps, _ = x_sample.shape
    caches = (cache_win_g1, cache_win_g2, cache_win_g3)

    mods = _modulation(jnp.concatenate([c_prompt, c_sample], axis=0), w_ada, b_ada)
    mod_p = mods[:, :batch].reshape(DEPTH, batch, N_MOD, D_MODEL)
    mod_s = jnp.transpose(mods[:, batch:].reshape(DEPTH, nb, N_MOD, D_MODEL), (0, 2, 1, 3))
    fgain = final_gain.reshape(1, D_MODEL)

    xp = x_prompt.reshape(batch * seq, D_MODEL)
    xs = jnp.transpose(x_sample, (1, 0, 2)).reshape(steps * nb, D_MODEL)

    pool_p, pool_s, gla_p, gla_s, conv_p, conv_s = [], [], [], [], [], []
    win_p, win_s = None, None
    ffn_stack = _ffn_weights(ffn_w_in, ffn_conv_w, ffn_conv_b, ffn_w_down)

    for i in range(DEPTH):
        kind, j = i % 3, i // 3
        last = i == DEPTH - 1
        ffn = (i, ffn_stack)
        conv_past = _conv_past_sample(state_ffn_conv[i])
        gains = norm_gain[i]
        if kind == 0:
            mix_w = (pool_w[j].astype(BF16), pool_scale[j].reshape(1, D_MODEL))
            xp, cs, pst = _layer_prompt("pool", last, xp, mod_p[i], gains, fgain, mix_w, ffn)
            pool_p.append(pst[:, POOL_CARRY_ROWS - POOL_STATE_ROWS:])
            past = jnp.transpose(state_pool[j], (1, 0, 2))
            xs, css, psts = _layer_sample("pool", last, xs, mod_s[i], gains, fgain, mix_w + (past,), ffn, conv_past)
            pool_s.append(jnp.transpose(psts, (1, 0, 2)))
        elif kind == 1:
            w = attn_w_in[j]
            w3 = jnp.stack([jnp.concatenate([w[:, s * ATTN_INNER + g * GROUP_WIDTH:
                                                s * ATTN_INNER + (g + 1) * GROUP_WIDTH] for s in range(3)], axis=1)
                            for g in range(N_GROUPS)]).astype(BF16)
            gb = _group_bias(rel_bias)
            wo = attn_w_out[j].astype(BF16)
            qkvp, kvn = _qkv_prompt(xp.reshape(batch, seq, D_MODEL), mod_p[i], gains, w3)
            o_all = _attn_prompt(qkvp, _prompt_bias_rows(gb))
            win_p = [kvn[:, g, seq - min(window, seq):].reshape(1, batch, min(window, seq), 2, HEADS_PER_GROUP, HEAD_DIM)
                     for g, (window, _) in enumerate(DILATED_GROUPS)]
            xp, cs = _layer_prompt("proj", last, xp, mod_p[i], gains, fgain,
                                   (o_all.reshape(batch * seq, ATTN_INNER), wo), ffn)
            qkv_s = _proj_sample(xs, mod_s[i], gains, w.astype(BF16))
            q6 = qkv_s.reshape(steps, nb, 3, N_GROUPS, HEADS_PER_GROUP, HEAD_DIM)
            q_s = jnp.pad(jnp.transpose(q6[:, :, 0], (1, 2, 3, 0, 4)),
                          ((0, 0), (0, 0), (0, 0), (0, SUBLANES - steps), (0, 0)))
            q_s = q_s.reshape(nb, N_GROUPS, HEADS_PER_GROUP // 2, 2, SUBLANES, HEAD_DIM)
            zeros = jnp.zeros_like(q_s[:, :, :, 0])
            q_s = jnp.stack([jnp.concatenate([q_s[:, :, :, 0], zeros], axis=-1),
                             jnp.concatenate([zeros, q_s[:, :, :, 1]], axis=-1)], axis=3)
            q_s = q_s.reshape(nb, N_GROUPS, HEADS_PER_GROUP // 2, 2 * SUBLANES, LANES)
            kv_new = jnp.transpose(q6[:, :, 1:], (1, 3, 0, 2, 4, 5)).reshape(nb, N_GROUPS, steps, 2 * GROUP_WIDTH)
            kv_new = jnp.pad(kv_new, ((0, 0), (0, 0), (0, SUBLANES - steps), (0, 0)))
            past_rows = [c.shape[2] for c in caches]
            cache_t = [jnp.transpose(c[j], (0, 2, 3, 4, 1)).reshape(nb, 2 * HEADS_PER_GROUP, HEAD_DIM, p)
                       for c, p in zip(caches, past_rows)]
            bias_past, bias_new = _sample_bias_tables(gb, steps, past_rows)
            o8, *new_caches = _attn_sample(steps, q_s, kv_new, cache_t, bias_past, bias_new)
            a_s = jnp.transpose(o8[:, :steps], (1, 0, 2)).reshape(steps * nb, ATTN_INNER)
            win_s = [jnp.transpose(c.reshape(nb, 2, HEADS_PER_GROUP, HEAD_DIM, p), (0, 4, 1, 2, 3))[None]
                     for c, p in zip(new_caches, past_rows)]
            xs, css = _layer_sample("proj", last, xs, mod_s[i], gains, fgain, (a_s.astype(BF16), wo), ffn, conv_past)
        else:
            w = gla_w_in[j]
            n_main = 2 * GLA_QK + 2 * GLA_V
            weights = (w[:, :n_main].astype(BF16),
                       jnp.pad(w[:, n_main:], ((0, 0), (0, LANES - GATE_RANK))).astype(BF16),
                       jnp.pad(gla_w_gate_up[j], ((0, LANES - GATE_RANK), (0, 0))).astype(BF16),
                       gla_b_gate[j].reshape(1, GLA_QK))
            gain = gla_norm_gain[j].reshape(1, GLA_V)
            wo = gla_w_out[j].astype(BF16)
            q, k, v, r, la = _gla_proj(xp, mod_p[i], gains, weights, None, PROMPT_ROW_TILE)
            shp = lambda a: a.reshape(batch, seq, a.shape[-1])
            a_p, s_p = _gla_prompt(shp(q), shp(k), shp(v), shp(r), shp(la), gain)
            gla_p.append(s_p)
            xp, cs = _layer_prompt("proj", last, xp, mod_p[i], gains, fgain,
                                   (a_p.reshape(batch * seq, GLA_V), wo), ffn)
            outs = _gla_proj(xs, mod_s[i], gains, weights, nb, steps * nb)

            def per_seq(a):
                a = jnp.transpose(a.reshape(steps, nb, a.shape[-1]), (1, 0, 2))
                return jnp.pad(a, ((0, 0), (0, SAMPLE_DEC_PAD - steps), (0, 0)))

            qs, ks, vs, rs, las = (per_seq(a) for a in outs)
            a16, s_s = _gla_sample(qs, ks, vs, rs, las, gain, state_gla[j])
            gla_s.append(s_s)
            a_s = jnp.transpose(a16[:, :steps], (1, 0, 2)).reshape(steps * nb, GLA_V).astype(BF16)
            xs, css = _layer_sample("proj", last, xs, mod_s[i], gains, fgain, (a_s, wo), ffn, conv_past)
        conv_p.append(_conv_tail_prompt(cs))
        conv_s.append(_conv_tail_sample(css, nb))

    y_prompt = xp.reshape(batch, seq, D_MODEL)
    y_sample = jnp.transpose(xs.reshape(steps, nb, D_MODEL), (1, 0, 2))
    return (y_prompt, y_sample, jnp.stack(pool_p), jnp.stack(pool_s),
            win_p[0], win_s[0], win_p[1], win_s[1], win_p[2], win_s[2],
            jnp.stack(gla_p), jnp.stack(gla_s), jnp.stack(conv_p), jnp.stack(conv_s))
```

```python
import functools
import math

import numpy as np
import jax
import jax.numpy as jnp
from jax import lax
from jax.experimental import pallas as pl
from jax.experimental.pallas import tpu as pltpu

F32 = jnp.float32
BF16 = jnp.bfloat16

D_MODEL = 1024
DEPTH = 4
N_MOD = 6
EPS = 1e-6
NEG_INF = -1e30
POOL_WINDOWS = (2, 4, 8, 16)
POOL_GROUP_DIM = D_MODEL // len(POOL_WINDOWS)
POOL_STATE_ROWS = max(POOL_WINDOWS) - 1
POOL_CARRY_ROWS = 16
DILATED_GROUPS = ((128, 1), (512, 4), (2048, 16))
N_GROUPS = len(DILATED_GROUPS)
HEADS_PER_GROUP = 4
HEAD_DIM = 64
GROUP_WIDTH = HEADS_PER_GROUP * HEAD_DIM
ATTN_INNER = N_GROUPS * GROUP_WIDTH
KEYS_PER_QUERY = 129
QUERY_BLOCK = 128
NUM_BUCKETS = 32
MAX_DISTANCE = 2048
GLA_HEADS = 4
GLA_DK = 128
GLA_DV = 256
GLA_QK = GLA_HEADS * GLA_DK
GLA_V = GLA_HEADS * GLA_DV
GATE_RANK = 16
GATE_TAU = 16.0
GLA_CHUNK = 128
D_FF = 2816
CONV_WIDTH = 3

LANES = 128
SUBLANES = 8
FF_CHUNK = 256
N_FF_CHUNKS = D_FF // FF_CHUNK
VMEM_LIMIT_BYTES = 56 * 1024 * 1024
PROMPT_ROW_TILE = 1024
GLA_TIME_TILE = 512
ATTN_BLOCK_UNROLL = 16
GLA_CHUNK_UNROLL = 4
GLA_SAMPLE_UNROLL = 4
ATTN_SAMPLE_SEQS = 2
SAMPLE_DEC_PAD = 16


def _params(*semantics):
    return pltpu.CompilerParams(dimension_semantics=semantics, vmem_limit_bytes=VMEM_LIMIT_BYTES)


def _resident(shape):
    nd = len(shape)
    return pl.BlockSpec(shape, lambda *_: (0,) * nd, pipeline_mode=pl.Buffered(1))


def _dot(a, b):
    return jnp.dot(a, b, preferred_element_type=F32)


def _dot_nt(a, b):
    return lax.dot_general(a, b, (((1,), (1,)), ((), ())), preferred_element_type=F32)


def _dot_tn(a, b):
    return lax.dot_general(a, b, (((0,), (0,)), ((), ())), preferred_element_type=F32)


def _rms(x):
    return x * lax.rsqrt(jnp.mean(x * x, axis=-1, keepdims=True) + EPS)


def _bcast_rows(v, y, nb):
    if nb is None:
        return v * y
    rows, width = y.shape
    return (y.reshape(rows // nb, nb, width) * v[None]).reshape(rows, width)


def _norm_mod(x, gain, shift, scale, nb):
    y = _rms(x) * gain
    if nb is None:
        return y * (1.0 + scale) + shift
    rows, width = y.shape
    y3 = y.reshape(rows // nb, nb, width)
    return (y3 * (1.0 + scale)[None] + shift[None]).reshape(rows, width)


def _gelu(x):
    return 0.5 * x * (1.0 + lax.erf(x * (1.0 / math.sqrt(2.0))))


def _silu(x):
    return x * jax.nn.sigmoid(x)


def _split_bf16(a):
    hi = a.astype(BF16)
    lo = (a - hi.astype(F32)).astype(BF16)
    return hi, lo


def _mod_kernel(c_ref, w_ref, b_ref, o_ref):
    a_hi, a_lo = _split_bf16(_silu(c_ref[...]))
    w_hi, w_lo = _split_bf16(w_ref[0])
    o_ref[0] = _dot(a_hi, w_hi) + _dot(a_lo, w_hi) + _dot(a_hi, w_lo) + b_ref[0]


def _modulation(c_all, w_ada, b_ada):
    rows = c_all.shape[0]
    width = N_MOD * D_MODEL
    tn = 1536
    return pl.pallas_call(
        _mod_kernel,
        grid=(DEPTH, width // tn),
        in_specs=[pl.BlockSpec((rows, D_MODEL), lambda l, n: (0, 0)),
                  pl.BlockSpec((1, D_MODEL, tn), lambda l, n: (l, 0, n)),
                  pl.BlockSpec((1, 1, tn), lambda l, n: (l, 0, n))],
        out_specs=pl.BlockSpec((1, rows, tn), lambda l, n: (l, 0, n)),
        out_shape=jax.ShapeDtypeStruct((DEPTH, rows, width), F32),
        compiler_params=_params("parallel", "parallel"),
        name="adaln_mod",
    )(c_all, w_ada, b_ada.reshape(DEPTH, 1, width))


def _ffn_chunk_math(g, g_m1, g_m2, u, cw, cb):
    gc = cw[2:3] * g + cw[1:2] * g_m1 + cw[0:1] * g_m2 + cb
    return (_gelu(gc) * u).astype(BF16)


def _ffn_chunk_weights(win_ref, wd_ref, cw_ref, cb_ref, j):
    cols = slice(j * FF_CHUNK, (j + 1) * FF_CHUNK)
    ucols = slice(D_FF + j * FF_CHUNK, D_FF + (j + 1) * FF_CHUNK)
    return win_ref[0, :, cols], win_ref[0, :, ucols], wd_ref[0, cols, :], cw_ref[0, :, cols], cb_ref[0, :, cols]


def _layer_resident(array, layer):
    nd = array.ndim
    return pl.BlockSpec((1,) + array.shape[1:], lambda *_: (layer,) + (0,) * (nd - 1),
                        pipeline_mode=pl.Buffered(1))


def _layer_prompt_kernel(mixer, last, tm, tiles_per_seq, *refs):
    refs = list(refs)
    x_ref, mod_ref, gains_ref, fg_ref = refs[:4]
    refs = refs[4:]
    if mixer == "pool":
        pw_ref, ps_ref = refs[:2]
    else:
        a_ref, wp_ref = refs[:2]
    win_ref, wd_ref, cw_ref, cb_ref = refs[2:6]
    refs = refs[6:]
    if mixer == "pool":
        y_ref, cs_ref, pst_ref, h2_ref, act_ref, gext_ref, cc_ref, hext_ref = refs
    else:
        y_ref, cs_ref, h2_ref, act_ref, gext_ref, cc_ref = refs

    i = pl.program_id(0)
    tile_in_seq = i % tiles_per_seq

    @pl.when(tile_in_seq == 0)
    def _():
        cc_ref[...] = jnp.zeros_like(cc_ref)
        if mixer == "pool":
            hext_ref[0:POOL_CARRY_ROWS, :] = jnp.zeros((POOL_CARRY_ROWS, D_MODEL), F32)

    x = x_ref[...]
    m = mod_ref[0]
    gains = gains_ref[...]

    if mixer == "pool":
        h = _norm_mod(x, gains[0:1], m[0:1], m[1:2], None)
        hext_ref[POOL_CARRY_ROWS:, :] = h
        pos = tile_in_seq * tm + lax.broadcasted_iota(jnp.int32, (tm, 1), 0)
        parts = []
        for g, w in enumerate(POOL_WINDOWS):
            cols = slice(g * POOL_GROUP_DIM, (g + 1) * POOL_GROUP_DIM)
            s = hext_ref[:, cols]
            span = 1
            while span < w:
                s = s + pltpu.roll(s, span, 0)
                span *= 2
            inv_count = 1.0 / jnp.minimum(pos + 1, w).astype(F32)
            d = s[POOL_CARRY_ROWS:, :] * inv_count - h[:, cols]
            parts.append(_dot(d.astype(BF16), pw_ref[g]))
        mix = jnp.concatenate(parts, axis=-1) * ps_ref[...]
        tail = hext_ref[pl.ds(tm, POOL_CARRY_ROWS), :]
        hext_ref[0:POOL_CARRY_ROWS, :] = tail
        pst_ref[0] = tail
    else:
        mix = _dot(a_ref[...], wp_ref[...])

    x1 = x + m[2:3] * mix
    y_ref[...] = x1
    h2_ref[...] = _norm_mod(x1, gains[1:2], m[3:4], m[4:5], None).astype(BF16)

    def up_proj(j):
        wg, wu, _, _, _ = _ffn_chunk_weights(win_ref, wd_ref, cw_ref, cb_ref, j)
        h2 = h2_ref[...]
        return _dot(h2, wg), _dot(h2, wu)

    ahead = up_proj(0)
    for j in range(N_FF_CHUNKS):
        g, u = ahead
        if j + 1 < N_FF_CHUNKS:
            ahead = up_proj(j + 1)
        _, _, _, cw, cb = _ffn_chunk_weights(win_ref, wd_ref, cw_ref, cb_ref, j)
        gx = gext_ref.at[j % 2]
        gx[0:SUBLANES, :] = cc_ref[j]
        gx[SUBLANES:, :] = g
        act_ref[:, j * FF_CHUNK:(j + 1) * FF_CHUNK] = _ffn_chunk_math(
            g, gx[pl.ds(SUBLANES - 1, tm), :], gx[pl.ds(SUBLANES - 2, tm), :], u, cw, cb)
        tail = g[tm - SUBLANES:tm, :]
        cc_ref[j] = tail
        cs_ref[0, j] = tail
    xo = y_ref[...] + m[5:6] * _dot(act_ref[...], wd_ref[0])
    if last:
        xo = _rms(xo) * fg_ref[...]
    y_ref[...] = xo


def _layer_prompt(mixer, last, x, mod, gains, fgain, mix_args, ffn):
    n = x.shape[0]
    batch = mod.shape[0]
    seq = n // batch
    tm = PROMPT_ROW_TILE
    tps = seq // tm
    in_specs = [pl.BlockSpec((tm, D_MODEL), lambda i: (i, 0)),
                pl.BlockSpec((1, N_MOD, D_MODEL), lambda i: (i // tps, 0, 0)),
                _resident((2, D_MODEL)), _resident((1, D_MODEL))]
    if mixer == "pool":
        pw, ps = mix_args
        in_specs += [_resident(pw.shape), _resident(ps.shape)]
    else:
        a, wp = mix_args
        in_specs += [pl.BlockSpec((tm, a.shape[1]), lambda i: (i, 0)), _resident(wp.shape)]
    layer, ffn_w = ffn
    in_specs += [_layer_resident(w, layer) for w in ffn_w]
    out_shape = [jax.ShapeDtypeStruct((n, D_MODEL), F32),
                 jax.ShapeDtypeStruct((batch, N_FF_CHUNKS, SUBLANES, FF_CHUNK), F32)]
    out_specs = [pl.BlockSpec((tm, D_MODEL), lambda i: (i, 0)),
                 pl.BlockSpec((1, N_FF_CHUNKS, SUBLANES, FF_CHUNK), lambda i: (i // tps, 0, 0, 0))]
    scratch = [pltpu.VMEM((tm, D_MODEL), BF16), pltpu.VMEM((tm, D_FF), BF16),
               pltpu.VMEM((2, tm + SUBLANES, FF_CHUNK), F32), pltpu.VMEM((N_FF_CHUNKS, SUBLANES, FF_CHUNK), F32)]
    if mixer == "pool":
        out_shape.append(jax.ShapeDtypeStruct((batch, POOL_CARRY_ROWS, D_MODEL), F32))
        out_specs.append(pl.BlockSpec((1, POOL_CARRY_ROWS, D_MODEL), lambda i: (i // tps, 0, 0)))
        scratch.append(pltpu.VMEM((tm + POOL_CARRY_ROWS, D_MODEL), F32))
    return pl.pallas_call(
        functools.partial(_layer_prompt_kernel, mixer, last, tm, tps),
        grid=(n // tm,), in_specs=in_specs, out_specs=out_specs, out_shape=out_shape,
        scratch_shapes=scratch, compiler_params=_params("arbitrary"),
        name=f"layer_prompt_{mixer}",
    )(x, mod, gains, fgain, *mix_args, *ffn_w)


def _layer_sample_kernel(mixer, last, nb, steps, *refs):
    refs = list(refs)
    x_ref, mod_ref, gains_ref, fg_ref = refs[:4]
    refs = refs[4:]
    if mixer == "pool":
        pw_ref, ps_ref, ppast_ref = refs[:3]
        refs = refs[3:]
    else:
        a_ref, wp_ref = refs[:2]
        refs = refs[2:]
    win_ref, wd_ref, cw_ref, cb_ref, cpast_ref = refs[:5]
    refs = refs[5:]
    if mixer == "pool":
        y_ref, cs_ref, pst_ref, h2_ref, act_ref = refs
    else:
        y_ref, cs_ref, h2_ref, act_ref = refs
    rows = steps * nb

    x = x_ref[...]
    gains = gains_ref[...]
    if mixer == "pool":
        h = _norm_mod(x, gains[0:1], mod_ref[0], mod_ref[1], nb)
        new = [h[t * nb:(t + 1) * nb, :] for t in range(steps)]

        def u_rows(p, cols):
            if p < POOL_STATE_ROWS:
                return ppast_ref[p, :, cols]
            return new[p - POOL_STATE_ROWS][:, cols]

        parts = []
        for g, w in enumerate(POOL_WINDOWS):
            cols = slice(g * POOL_GROUP_DIM, (g + 1) * POOL_GROUP_DIM)
            ds = []
            for t in range(steps):
                s = u_rows(POOL_STATE_ROWS + t, cols)
                for k in range(1, w):
                    s = s + u_rows(POOL_STATE_ROWS + t - k, cols)
                ds.append(s * (1.0 / w) - new[t][:, cols])
            parts.append(_dot(jnp.concatenate(ds, axis=0).astype(BF16), pw_ref[g]))
        mix = jnp.concatenate(parts, axis=-1) * ps_ref[...]
        full = slice(0, D_MODEL)
        for p in range(POOL_STATE_ROWS):
            pst_ref[p] = u_rows(p + steps, full)
    else:
        mix = _dot(a_ref[...], wp_ref[...])

    x1 = x + _bcast_rows(mod_ref[2], mix, nb)
    y_ref[...] = x1
    h2_ref[...] = _norm_mod(x1, gains[1:2], mod_ref[3], mod_ref[4], nb).astype(BF16)
    past_rows = (CONV_WIDTH - 1) * nb

    for j in range(N_FF_CHUNKS):
        h2 = h2_ref[...]
        wg, wu, _, cw, cb = _ffn_chunk_weights(win_ref, wd_ref, cw_ref, cb_ref, j)
        g = _dot(h2, wg)
        u = _dot(h2, wu)
        gall = jnp.concatenate([cpast_ref[j], g], axis=0)
        act_ref[:, j * FF_CHUNK:(j + 1) * FF_CHUNK] = _ffn_chunk_math(
            g, gall[nb:nb + rows, :], gall[0:rows, :], u, cw, cb)
        cs_ref[j] = gall[rows:rows + past_rows, :]
    xo = y_ref[...] + _bcast_rows(mod_ref[5], _dot(act_ref[...], wd_ref[0]), nb)
    if last:
        xo = _rms(xo) * fg_ref[...]
    y_ref[...] = xo


def _layer_sample(mixer, last, x, mod, gains, fgain, mix_args, ffn, conv_past):
    rows = x.shape[0]
    nb = mod.shape[1]
    steps = rows // nb
    layer, ffn_w = ffn
    head = [x, mod, gains, fgain, *mix_args]
    args = [*head, *ffn_w, conv_past]
    out_shape = [jax.ShapeDtypeStruct((rows, D_MODEL), F32),
                 jax.ShapeDtypeStruct(conv_past.shape, F32)]
    if mixer == "pool":
        out_shape.append(jax.ShapeDtypeStruct((POOL_STATE_ROWS, nb, D_MODEL), F32))
    return pl.pallas_call(
        functools.partial(_layer_sample_kernel, mixer, last, nb, steps),
        grid=(1,),
        in_specs=[_resident(a.shape) for a in head] + [_layer_resident(w, layer) for w in ffn_w]
                 + [_resident(conv_past.shape)],
        out_specs=[pl.BlockSpec(s.shape, functools.partial(lambda nd, i: (0,) * nd, len(s.shape))) for s in out_shape],
        out_shape=out_shape,
        scratch_shapes=[pltpu.VMEM((rows, D_MODEL), BF16), pltpu.VMEM((rows, D_FF), BF16)],
        compiler_params=_params("arbitrary"),
        name=f"layer_sample_{mixer}",
    )(*args)


def _qkv_prompt_kernel(seq, x_ref, mod_ref, gains_ref, w_ref, qkvp_ref, kt1_ref, kt2_ref, kt3_ref, h_ref, slab_ref):
    g = pl.program_id(1)
    kt_refs = (kt1_ref, kt2_ref, kt3_ref)

    @pl.when(g == 0)
    def _():
        m = mod_ref[0]
        h_ref[...] = _norm_mod(x_ref[0], gains_ref[0:1, :], m[0:1], m[1:2], None).astype(BF16)

    h = h_ref[...]
    for c in range(3):
        r = _dot(h, w_ref[0, :, c * GROUP_WIDTH:(c + 1) * GROUP_WIDTH])
        if c == 0:
            r = r * (HEAD_DIM ** -0.5)
        slab_ref[2 * c] = r[:, 0:LANES]
        slab_ref[2 * c + 1] = r[:, LANES:2 * LANES]

    qkvp_ref[0, 0, 0:QUERY_BLOCK, :] = jnp.zeros((QUERY_BLOCK, 3 * GROUP_WIDTH), BF16)
    for gi, (_, dil) in enumerate(DILATED_GROUPS):
        @pl.when(g == gi)
        def _(dil=dil):
            per_class = seq // dil
            for s in range(6):
                for rho in range(dil):
                    if dil == 1:
                        v = slab_ref[s]
                    else:
                        v = slab_ref[s, pl.ds(rho, per_class, stride=dil), :]
                    r0 = QUERY_BLOCK + rho * per_class
                    qkvp_ref[0, 0, r0:r0 + per_class, s * LANES:(s + 1) * LANES] = v.astype(BF16)

    for gi, kt_ref in enumerate(kt_refs):
        @pl.when(g == gi)
        def _(kt_ref=kt_ref):
            keep = kt_ref.shape[-1]
            for s in range(2, 6):
                tile_t = slab_ref[s, seq - keep:seq, :].T
                for r in range(2):
                    kt_ref[0, 2 * (s - 2) + r] = tile_t[r * HEAD_DIM:(r + 1) * HEAD_DIM, :]


def _qkv_prompt(x3, mod, gains, w3):
    batch, seq, _ = x3.shape
    width = 3 * GROUP_WIDTH
    keeps = [min(window, seq) for window, _ in DILATED_GROUPS]
    return pl.pallas_call(
        functools.partial(_qkv_prompt_kernel, seq),
        grid=(batch, N_GROUPS),
        in_specs=[pl.BlockSpec((1, seq, D_MODEL), lambda b, g: (b, 0, 0)),
                  pl.BlockSpec((1, N_MOD, D_MODEL), lambda b, g: (b, 0, 0)),
                  pl.BlockSpec((2, D_MODEL), lambda b, g: (0, 0)),
                  pl.BlockSpec((1, D_MODEL, width), lambda b, g: (g, 0, 0))],
        out_specs=[pl.BlockSpec((1, 1, seq + QUERY_BLOCK, width), lambda b, g: (b, g, 0, 0))]
                  + [pl.BlockSpec((1, 2 * HEADS_PER_GROUP, HEAD_DIM, keep), lambda b, g: (b, 0, 0, 0)) for keep in keeps],
        out_shape=[jax.ShapeDtypeStruct((batch, N_GROUPS, seq + QUERY_BLOCK, width), BF16)]
                  + [jax.ShapeDtypeStruct((batch, 2 * HEADS_PER_GROUP, HEAD_DIM, keep), F32) for keep in keeps],
        scratch_shapes=[pltpu.VMEM((seq, D_MODEL), BF16), pltpu.VMEM((6, seq, LANES), F32)],
        compiler_params=_params("arbitrary", "arbitrary"),
        name="qkv_prompt",
    )(x3, mod, gains, w3)


def _head_lane_mask(rows, h):
    lane = lax.broadcasted_iota(jnp.int32, (rows, GROUP_WIDTH), 1)
    return (lane >= h * HEAD_DIM) & (lane < (h + 1) * HEAD_DIM)


def _attn_block(q, k, v, bias_ref, cols):
    masks = [_head_lane_mask(QUERY_BLOCK, h) for h in range(HEADS_PER_GROUP)]
    qs = jnp.concatenate([jnp.where(hm, q, jnp.zeros_like(q)) for hm in masks], axis=0)
    s = _dot_nt(qs, k) + bias_ref[:, cols]
    m = jnp.max(s, axis=-1, keepdims=True)
    p = jnp.exp(s - m)
    l = jnp.sum(p, axis=-1, keepdims=True)
    pv = _dot(p.astype(BF16), v) * (1.0 / l)
    lse_rows = jnp.broadcast_to(m + jnp.log(l), pv.shape)
    o = pv[0:QUERY_BLOCK]
    lse = lse_rows[0:QUERY_BLOCK]
    for h in range(1, HEADS_PER_GROUP):
        rows = slice(h * QUERY_BLOCK, (h + 1) * QUERY_BLOCK)
        o = jnp.where(masks[h], pv[rows], o)
        lse = jnp.where(masks[h], lse_rows[rows], lse)
    return o, lse


def _attn_prompt_kernel(seq, qkv_ref, brow_ref, o_ref, os_ref, ls_ref, bias_ref):
    @pl.when(pl.program_id(0) == 0)
    def _():
        left = lax.broadcasted_iota(jnp.int32, (QUERY_BLOCK, 2 * QUERY_BLOCK), 1) < QUERY_BLOCK
        for g in range(N_GROUPS):
            for h in range(HEADS_PER_GROUP):
                base = jnp.broadcast_to(brow_ref[g, h], (QUERY_BLOCK, 2 * QUERY_BLOCK))
                band = pltpu.roll(base, 0, 1, stride=1, stride_axis=0)
                rows = slice(h * QUERY_BLOCK, (h + 1) * QUERY_BLOCK)
                bias_ref[g, 0, rows, :] = band
                bias_ref[g, 1, rows, :] = jnp.where(left, NEG_INF, band)

    qc = slice(0, GROUP_WIDTH)
    kc = slice(GROUP_WIDTH, 2 * GROUP_WIDTH)
    vc = slice(2 * GROUP_WIDTH, 3 * GROUP_WIDTH)
    n_blocks = seq // QUERY_BLOCK
    for g, (_, dil) in enumerate(DILATED_GROUPS):
        blocks_per_class = n_blocks // dil

        def do_block(blk, carry, g=g, dil=dil, blocks_per_class=blocks_per_class):
            r0 = pl.multiple_of(blk * QUERY_BLOCK, QUERY_BLOCK)
            cur = pl.ds(r0 + QUERY_BLOCK, QUERY_BLOCK)
            q = qkv_ref[0, g, cur, qc]
            rho = blk // blocks_per_class
            in_class = blk % blocks_per_class
            if blocks_per_class == 1:
                o, lse = _attn_block(q, qkv_ref[0, g, cur, kc], qkv_ref[0, g, cur, vc],
                                     bias_ref.at[g, 0], slice(QUERY_BLOCK, 2 * QUERY_BLOCK))
            else:
                both = pl.ds(r0, 2 * QUERY_BLOCK)
                first = jnp.asarray(in_class == 0, jnp.int32)
                o, lse = _attn_block(q, qkv_ref[0, g, both, kc], qkv_ref[0, g, both, vc],
                                     bias_ref.at[g, first], slice(0, 2 * QUERY_BLOCK))
            start = in_class * (QUERY_BLOCK * dil) + rho
            for s in range(2):
                cols = slice(s * LANES, (s + 1) * LANES)
                if dil == 1:
                    os_ref[g, s, pl.ds(r0, QUERY_BLOCK), :] = o[:, cols]
                    ls_ref[g, s, pl.ds(r0, QUERY_BLOCK), :] = lse[:, cols]
                else:
                    os_ref[g, s, pl.ds(start, QUERY_BLOCK, stride=dil), :] = o[:, cols]
                    ls_ref[g, s, pl.ds(start, QUERY_BLOCK, stride=dil), :] = lse[:, cols]
            return carry

        lax.fori_loop(0, n_blocks, do_block, 0, unroll=ATTN_BLOCK_UNROLL)

    def merge(i, carry):
        r0 = pl.multiple_of(i * QUERY_BLOCK, QUERY_BLOCK)
        rows = pl.ds(r0, QUERY_BLOCK)
        for s in range(2):
            ls = [ls_ref[g, s, rows, :] for g in range(N_GROUPS)]
            mx = jnp.maximum(jnp.maximum(ls[0], ls[1]), ls[2])
            es = [jnp.exp(l - mx) for l in ls]
            inv = 1.0 / (es[0] + es[1] + es[2])
            for g in range(N_GROUPS):
                c0 = g * GROUP_WIDTH + s * LANES
                o_ref[0, rows, c0:c0 + LANES] = (os_ref[g, s, rows, :] * (es[g] * inv)).astype(BF16)
        return carry

    lax.fori_loop(0, seq // QUERY_BLOCK, merge, 0)


def _attn_prompt(qkvp, bias_rows):
    batch, _, padded, width = qkvp.shape
    seq = padded - QUERY_BLOCK
    return pl.pallas_call(
        functools.partial(_attn_prompt_kernel, seq),
        grid=(batch,),
        in_specs=[pl.BlockSpec((1, N_GROUPS, padded, width), lambda b: (b, 0, 0, 0)),
                  _resident(bias_rows.shape)],
        out_specs=pl.BlockSpec((1, seq, ATTN_INNER), lambda b: (b, 0, 0)),
        out_shape=jax.ShapeDtypeStruct((batch, seq, ATTN_INNER), BF16),
        scratch_shapes=[pltpu.VMEM((N_GROUPS, 2, seq, LANES), F32), pltpu.VMEM((N_GROUPS, 2, seq, LANES), F32),
                        pltpu.VMEM((N_GROUPS, 2, HEADS_PER_GROUP * QUERY_BLOCK, 2 * QUERY_BLOCK), F32)],
        compiler_params=_params("arbitrary"),
        name="attn_prompt",
    )(qkvp, bias_rows)


def _proj_sample_kernel(nb, x_ref, mod_ref, gains_ref, w_ref, o_ref):
    h = _norm_mod(x_ref[...], gains_ref[0:1, :], mod_ref[0], mod_ref[1], nb).astype(BF16)
    o_ref[...] = _dot(h, w_ref[...])


def _proj_sample(x, mod, gains, w):
    rows = x.shape[0]
    nb = mod.shape[1]
    args = [x, mod, gains, w]
    return pl.pallas_call(
        functools.partial(_proj_sample_kernel, nb),
        grid=(1,),
        in_specs=[_resident(a.shape) for a in args],
        out_specs=pl.BlockSpec((rows, w.shape[1]), lambda i: (0, 0)),
        out_shape=jax.ShapeDtypeStruct((rows, w.shape[1]), F32),
        compiler_params=_params("arbitrary"),
        name="qkv_sample",
    )(*args)


def _split3_bf16(a):
    hi = a.astype(BF16)
    r1 = a - hi.astype(F32)
    mid = r1.astype(BF16)
    lo = (r1 - mid.astype(F32)).astype(BF16)
    return hi, mid, lo


def _attn_sample_kernel(steps, q_ref, kvn_ref, c1_ref, c2_ref, c3_ref, t1_ref, t2_ref, t3_ref, bn_ref,
                        o_ref, n1_ref, n2_ref, n3_ref):
    for i in range(q_ref.shape[0]):
        _attn_sample_one(steps, i, q_ref, kvn_ref, ((c1_ref, t1_ref, n1_ref), (c2_ref, t2_ref, n2_ref),
                                                  (c3_ref, t3_ref, n3_ref)), bn_ref, o_ref)


def _attn_sample_one(steps, i, q_ref, kvn_ref, groups, bn_ref, o_ref):
    pairs = HEADS_PER_GROUP // 2
    outs, lses = [], []
    lane = lax.broadcasted_iota(jnp.int32, (SUBLANES, LANES), 1)
    row = lax.broadcasted_iota(jnp.int32, (SUBLANES, LANES), 0)
    own_lanes = lane < HEAD_DIM
    sel_head = jnp.where((lane == row) & (row < steps), 1.0, 0.0).astype(BF16)
    sel_tail = jnp.where((lane == row + (LANES - steps)) & (row < steps), 1.0, 0.0).astype(BF16)
    for g, (c_ref, t_ref, n_ref) in enumerate(groups):
        p_rows = c_ref.shape[-1]
        pieces = _split3_bf16(kvn_ref[i, g])
        new_head = sum(_dot_tn(x, sel_head) for x in pieces)
        new_tail = sum(_dot_tn(x, sel_tail) for x in pieces)
        o_pairs, l_pairs = [], []
        for j in range(pairs):
            q = (q_ref[i, g, j] * (HEAD_DIM ** -0.5)).astype(BF16)
            kt = c_ref[i, 2 * j:2 * j + 2].reshape(LANES, p_rows).astype(BF16)
            vt = c_ref[i, HEADS_PER_GROUP + 2 * j:HEADS_PER_GROUP + 2 * j + 2].reshape(LANES, p_rows).astype(BF16)
            k_new = new_head[j * LANES:(j + 1) * LANES].astype(BF16)
            v_new = new_head[(pairs + j) * LANES:(pairs + j + 1) * LANES].astype(BF16)
            s = _dot(q, kt) + t_ref[j]
            s_new = _dot(q, k_new) + bn_ref[g, j]
            m = jnp.maximum(jnp.max(s, axis=-1, keepdims=True), jnp.max(s_new, axis=-1, keepdims=True))
            p = jnp.exp(s - m)
            p_new = jnp.exp(s_new - m)
            l = jnp.sum(p, axis=-1, keepdims=True) + jnp.sum(p_new, axis=-1, keepdims=True)
            o = _dot_nt(p.astype(BF16), vt) + _dot_nt(p_new.astype(BF16), v_new)
            o_pairs.append(o * (1.0 / l))
            l_pairs.append(m + jnp.log(l))
            for idx in (2 * j, 2 * j + 1, HEADS_PER_GROUP + 2 * j, HEADS_PER_GROUP + 2 * j + 1):
                n_ref[i, idx] = pltpu.roll(c_ref[i, idx], p_rows - steps, 1)
                n_ref[i, idx, :, p_rows - steps:p_rows] = new_tail[idx * HEAD_DIM:(idx + 1) * HEAD_DIM, LANES - steps:LANES]
        outs.append(o_pairs)
        lses.append(l_pairs)
    for j in range(pairs):
        ls = [lses[g][j] for g in range(N_GROUPS)]
        mx = jnp.maximum(jnp.maximum(ls[0], ls[1]), ls[2])
        es = [jnp.exp(l - mx) for l in ls]
        inv = 1.0 / (es[0] + es[1] + es[2])
        for g in range(N_GROUPS):
            og = outs[g][j] * (es[g] * inv)
            c0 = g * GROUP_WIDTH + j * LANES
            o_ref[i, :, c0:c0 + LANES] = jnp.where(own_lanes, og[0:SUBLANES], og[SUBLANES:2 * SUBLANES])


def _attn_sample(steps, q, kv_new, caches, tables, bias_new):
    nb = q.shape[0]
    sb = math.gcd(nb, ATTN_SAMPLE_SEQS)
    in_specs = [pl.BlockSpec((sb,) + q.shape[1:], lambda b: (b, 0, 0, 0, 0)),
                pl.BlockSpec((sb,) + kv_new.shape[1:], lambda b: (b, 0, 0, 0))]
    cache_specs = [pl.BlockSpec((sb,) + c.shape[1:], lambda b: (b, 0, 0, 0)) for c in caches]
    in_specs += cache_specs + [_resident(t.shape) for t in tables] + [_resident(bias_new.shape)]
    return pl.pallas_call(
        functools.partial(_attn_sample_kernel, steps),
        grid=(nb // sb,), in_specs=in_specs,
        out_specs=[pl.BlockSpec((sb, SUBLANES, ATTN_INNER), lambda b: (b, 0, 0))] + cache_specs,
        out_shape=[jax.ShapeDtypeStruct((nb, SUBLANES, ATTN_INNER), F32)]
                  + [jax.ShapeDtypeStruct(c.shape, F32) for c in caches],
        compiler_params=_params("arbitrary"),
        name="attn_sample",
    )(q, kv_new, *caches, *tables, bias_new)


def _gla_proj_kernel(nb, x_ref, mod_ref, gains_ref, w_ref, wgd_ref, wgu_ref, bg_ref,
                     q_ref, k_ref, v_ref, r_ref, la_ref):
    if nb is None:
        m = mod_ref[0]
        shift, scale = m[0:1], m[1:2]
    else:
        shift, scale = mod_ref[0], mod_ref[1]
    h = _norm_mod(x_ref[...], gains_ref[0:1, :], shift, scale, nb).astype(BF16)
    q_ref[...] = _dot(h, w_ref[:, 0:GLA_QK]) * (GLA_DK ** -0.5)
    k_ref[...] = _dot(h, w_ref[:, GLA_QK:2 * GLA_QK])
    v_ref[...] = _dot(h, w_ref[:, 2 * GLA_QK:2 * GLA_QK + GLA_V]).astype(BF16)
    r_ref[...] = _dot(h, w_ref[:, 2 * GLA_QK + GLA_V:2 * GLA_QK + 2 * GLA_V])
    gd = _dot(h, wgd_ref[...])
    gate = _dot(gd.astype(BF16), wgu_ref[...]) + bg_ref[...]
    la_ref[...] = jax.nn.log_sigmoid(gate) * (1.0 / GATE_TAU)


def _gla_proj(x, mod, gains, weights, nb, tm):
    rows = x.shape[0]
    if nb is None:
        tps = rows // mod.shape[0] // tm
        mod_spec = pl.BlockSpec((1, N_MOD, D_MODEL), lambda i: (i // tps, 0, 0))
    else:
        mod_spec = _resident(mod.shape)
    widths = (GLA_QK, GLA_QK, GLA_V, GLA_V, GLA_QK)
    dtypes = (F32, F32, BF16, F32, F32)
    return pl.pallas_call(
        functools.partial(_gla_proj_kernel, nb),
        grid=(rows // tm,),
        in_specs=[pl.BlockSpec((tm, D_MODEL), lambda i: (i, 0)), mod_spec, _resident((2, D_MODEL))]
                 + [_resident(w.shape) for w in weights],
        out_specs=[pl.BlockSpec((tm, w), lambda i: (i, 0)) for w in widths],
        out_shape=[jax.ShapeDtypeStruct((rows, w), dt) for w, dt in zip(widths, dtypes)],
        compiler_params=_params("arbitrary"),
        name="gla_proj",
    )(x, mod, gains, *weights)


def _cumsum_rows(g):
    rows = g.shape[0]
    row = lax.broadcasted_iota(jnp.int32, g.shape, 0)
    b = g
    shift = 1
    while shift < rows:
        b = b + jnp.where(row >= shift, pltpu.roll(b, shift, 0), 0.0)
        shift *= 2
    return b


def _gla_chunk(q, k, v, g, r, gain, mid, get_state, set_state):
    c = q.shape[0]
    b = _cumsum_rows(g)
    b_end = b[c - 1:c, :]
    b_mid = b[mid:mid + 1, :]
    q_in = (q * jnp.exp(b)).astype(BF16)
    q_rel = (q * jnp.exp(b - b_mid)).astype(BF16)
    k_rel = (k * jnp.exp(b_mid - b)).astype(BF16)
    k_out = (k * jnp.exp(b_end - b)).astype(BF16)
    decay_end = jnp.broadcast_to(jnp.exp(b_end), (SUBLANES, GLA_QK))
    ti = lax.broadcasted_iota(jnp.int32, (c, c), 0)
    si = lax.broadcasted_iota(jnp.int32, (c, c), 1)
    outs = []
    for h in range(GLA_HEADS):
        ks = slice(h * GLA_DK, (h + 1) * GLA_DK)
        vs = slice(h * GLA_DV, (h + 1) * GLA_DV)
        state = get_state(h)
        o = _dot(q_in[:, ks], state.astype(BF16))
        att = jnp.where(si <= ti, _dot_nt(q_rel[:, ks], k_rel[:, ks]), 0.0)
        o = o + _dot(att.astype(BF16), v[:, vs])
        decay_col = decay_end[:, ks].T[:, 0:1]
        set_state(h, decay_col * state + _dot_tn(k_out[:, ks], v[:, vs]))
        outs.append(_rms(o) * gain[:, vs] * _silu(r[:, vs]))
    return jnp.concatenate(outs, axis=-1)


def _gla_prompt_kernel(tt, q_ref, k_ref, v_ref, r_ref, la_ref, gain_ref, a_ref, so_ref, s_ref):
    t = pl.program_id(1)

    @pl.when(t == 0)
    def _():
        s_ref[...] = jnp.zeros_like(s_ref)

    gain = gain_ref[...]

    def get_state(h):
        return s_ref[h]

    def set_state(h, val):
        s_ref[h] = val

    def chunk(ci, carry):
        rows = pl.ds(pl.multiple_of(ci * GLA_CHUNK, GLA_CHUNK), GLA_CHUNK)
        a = _gla_chunk(q_ref[0, rows, :], k_ref[0, rows, :], v_ref[0, rows, :], la_ref[0, rows, :],
                       r_ref[0, rows, :], gain, GLA_CHUNK // 2, get_state, set_state)
        a_ref[0, rows, :] = a.astype(BF16)
        return carry

    lax.fori_loop(0, tt // GLA_CHUNK, chunk, 0, unroll=GLA_CHUNK_UNROLL)

    @pl.when(t == pl.num_programs(1) - 1)
    def _():
        so_ref[0] = s_ref[...]


def _gla_prompt(q, k, v, r, la, gain):
    batch, seq, _ = q.shape
    tt = GLA_TIME_TILE

    def spec(width):
        return pl.BlockSpec((1, tt, width), lambda b, t: (b, t, 0))

    return pl.pallas_call(
        functools.partial(_gla_prompt_kernel, tt),
        grid=(batch, seq // tt),
        in_specs=[spec(GLA_QK), spec(GLA_QK), spec(GLA_V), spec(GLA_V), spec(GLA_QK),
                  pl.BlockSpec((1, GLA_V), lambda b, t: (0, 0))],
        out_specs=[spec(GLA_V), pl.BlockSpec((1, GLA_HEADS, GLA_DK, GLA_DV), lambda b, t: (b, 0, 0, 0))],
        out_shape=[jax.ShapeDtypeStruct((batch, seq, GLA_V), BF16),
                   jax.ShapeDtypeStruct((batch, GLA_HEADS, GLA_DK, GLA_DV), F32)],
        scratch_shapes=[pltpu.VMEM((GLA_HEADS, GLA_DK, GLA_DV), F32)],
        compiler_params=_params("arbitrary", "arbitrary"),
        name="gla_prompt",
    )(q, k, v, r, la, gain)


def _gla_sample_kernel(sb, q_ref, k_ref, v_ref, r_ref, la_ref, gain_ref, s0_ref, a_ref, so_ref):
    gain = gain_ref[...]

    def seq_body(i, carry):
        def get_state(h):
            return s0_ref[i, h]

        def set_state(h, val):
            so_ref[i, h] = val

        a_ref[i] = _gla_chunk(q_ref[i], k_ref[i], v_ref[i], la_ref[i], r_ref[i], gain, 0, get_state, set_state)
        return carry

    lax.fori_loop(0, sb, seq_body, 0, unroll=GLA_SAMPLE_UNROLL)


def _gla_sample(q, k, v, r, la, gain, s0):
    nb, pad, _ = q.shape
    sb = math.gcd(nb, 8)

    def spec(width):
        return pl.BlockSpec((sb, pad, width), lambda i: (i, 0, 0))

    state_spec = pl.BlockSpec((sb, GLA_HEADS, GLA_DK, GLA_DV), lambda i: (i, 0, 0, 0))
    return pl.pallas_call(
        functools.partial(_gla_sample_kernel, sb),
        grid=(nb // sb,),
        in_specs=[spec(GLA_QK), spec(GLA_QK), spec(GLA_V), spec(GLA_V), spec(GLA_QK),
                  pl.BlockSpec((1, GLA_V), lambda i: (0, 0)), state_spec],
        out_specs=[spec(GLA_V), state_spec],
        out_shape=[jax.ShapeDtypeStruct((nb, pad, GLA_V), F32),
                   jax.ShapeDtypeStruct((nb, GLA_HEADS, GLA_DK, GLA_DV), F32)],
        compiler_params=_params("arbitrary"),
        name="gla_sample",
    )(q, k, v, r, la, gain, s0)


def _t5_bucket(dist):
    max_exact = NUM_BUCKETS // 2
    d_f = jnp.maximum(dist, 1).astype(F32)
    large = max_exact + (jnp.log(d_f / max_exact) / math.log(MAX_DISTANCE / max_exact)
                         * (NUM_BUCKETS - max_exact)).astype(jnp.int32)
    large = jnp.minimum(large, NUM_BUCKETS - 1)
    return jnp.where(dist < max_exact, dist, large)


def _group_bias(rel_bias):
    rows = []
    for g, (window, dil) in enumerate(DILATED_GROUPS):
        buckets = _t5_bucket(jnp.arange(window // dil + 1) * dil)
        rows.append(rel_bias[buckets][:, g * HEADS_PER_GROUP:(g + 1) * HEADS_PER_GROUP].T)
    return jnp.stack(rows)


def _prompt_bias_rows(gb):
    band = gb[:, :, ::-1]
    off = jnp.full(gb.shape[:2] + (2 * QUERY_BLOCK - KEYS_PER_QUERY,), NEG_INF, F32)
    return jnp.concatenate([band, off], axis=-1)[:, :, None, :]


def _sample_bias_tables(gb, steps, past_rows):
    tables, new_tables = [], []
    t_idx = np.arange(SUBLANES)[:, None]
    c_idx = np.arange(LANES)[None, :]
    for g, (window, dil) in enumerate(DILATED_GROUPS):
        p = past_rows[g]
        assert p == window == (KEYS_PER_QUERY - 1) * dil
        b = gb[g]
        heads = b.shape[0]
        row0 = b[:, :0:-1]
        if dil > 1:
            gaps = jnp.full((heads, KEYS_PER_QUERY - 1, dil - 1), NEG_INF, F32)
            row0 = jnp.concatenate([row0[:, :, None], gaps], axis=2).reshape(heads, p)
        rows = []
        for t in range(SUBLANES):
            if t < steps:
                rows.append(jnp.concatenate([jnp.full((heads, t), NEG_INF, F32), row0[:, :p - t]], axis=1))
            else:
                rows.append(jnp.full((heads, p), NEG_INF, F32))
        past = jnp.stack(rows, axis=1)
        new = jnp.full((heads, SUBLANES, LANES), NEG_INF, F32)
        for j in range((steps - 1) // dil + 1):
            mask = (t_idx - c_idx == j * dil) & (t_idx < steps) & (c_idx < steps)
            new = jnp.where(jnp.asarray(mask)[None], b[:, j][:, None, None], new)
        tables.append(past.reshape(heads // 2, 2 * SUBLANES, p))
        new_tables.append(new.reshape(heads // 2, 2 * SUBLANES, LANES))
    return tables, jnp.stack(new_tables)


def _ffn_weights(w_in, conv_w, conv_b, w_down):
    return (w_in.astype(BF16), w_down.astype(BF16), conv_w, conv_b.reshape(DEPTH, 1, D_FF))


def _conv_tail_prompt(cs):
    batch = cs.shape[0]
    tail = cs[:, :, SUBLANES - (CONV_WIDTH - 1):, :]
    return jnp.transpose(tail, (0, 2, 1, 3)).reshape(batch, CONV_WIDTH - 1, D_FF)


def _conv_past_sample(state):
    nb = state.shape[0]
    s = state.reshape(nb, CONV_WIDTH - 1, N_FF_CHUNKS, FF_CHUNK)
    return jnp.transpose(s, (2, 1, 0, 3)).reshape(N_FF_CHUNKS, (CONV_WIDTH - 1) * nb, FF_CHUNK)


def _conv_tail_sample(cs, nb):
    s = cs.reshape(N_FF_CHUNKS, CONV_WIDTH - 1, nb, FF_CHUNK)
    return jnp.transpose(s, (2, 1, 0, 3)).reshape(nb, CONV_WIDTH - 1, D_FF)


def kernel(x_prompt, x_sample, state_pool, cache_win_g1, cache_win_g2, cache_win_g3, state_gla, state_ffn_conv,
           c_prompt, c_sample, w_ada, b_ada, norm_gain, final_gain, rel_bias, pool_w, pool_scale,
           attn_w_in, attn_w_out, gla_w_in, gla_w_gate_up, gla_b_gate, gla_norm_gain, gla_w_out,
           ffn_w_in, ffn_conv_w, ffn_conv_b, ffn_w_down):
    batch, seq, _ = x_prompt.shape
    nb, steps, _ = x_sample.shape
    caches = (cache_win_g1, cache_win_g2, cache_win_g3)

    mods = _modulation(jnp.concatenate([c_prompt, c_sample], axis=0), w_ada, b_ada)
    mod_p = mods[:, :batch].reshape(DEPTH, batch, N_MOD, D_MODEL)
    mod_s = jnp.transpose(mods[:, batch:].reshape(DEPTH, nb, N_MOD, D_MODEL), (0, 2, 1, 3))
    fgain = final_gain.reshape(1, D_MODEL)

    xp = x_prompt.reshape(batch * seq, D_MODEL)
    xs = jnp.transpose(x_sample, (1, 0, 2)).reshape(steps * nb, D_MODEL)

    pool_p, pool_s, gla_p, gla_s, conv_p, conv_s = [], [], [], [], [], []
    win_p, win_s = None, None
    ffn_stack = _ffn_weights(ffn_w_in, ffn_conv_w, ffn_conv_b, ffn_w_down)

    for i in range(DEPTH):
        kind, j = i % 3, i // 3
        last = i == DEPTH - 1
        ffn = (i, ffn_stack)
        conv_past = _conv_past_sample(state_ffn_conv[i])
        gains = norm_gain[i]
        if kind == 0:
            mix_w = (pool_w[j].astype(BF16), pool_scale[j].reshape(1, D_MODEL))
            xp, cs, pst = _layer_prompt("pool", last, xp, mod_p[i], gains, fgain, mix_w, ffn)
            pool_p.append(pst[:, POOL_CARRY_ROWS - POOL_STATE_ROWS:])
            past = jnp.transpose(state_pool[j], (1, 0, 2))
            xs, css, psts = _layer_sample("pool", last, xs, mod_s[i], gains, fgain, mix_w + (past,), ffn, conv_past)
            pool_s.append(jnp.transpose(psts, (1, 0, 2)))
        elif kind == 1:
            w = attn_w_in[j]
            w3 = jnp.stack([jnp.concatenate([w[:, s * ATTN_INNER + g * GROUP_WIDTH:
                                                s * ATTN_INNER + (g + 1) * GROUP_WIDTH] for s in range(3)], axis=1)
                            for g in range(N_GROUPS)]).astype(BF16)
            gb = _group_bias(rel_bias)
            wo = attn_w_out[j].astype(BF16)
            qkvp, *kv_t = _qkv_prompt(xp.reshape(batch, seq, D_MODEL), mod_p[i], gains, w3)
            o_all = _attn_prompt(qkvp, _prompt_bias_rows(gb))
            win_p = [jnp.transpose(t.reshape(batch, 2, HEADS_PER_GROUP, HEAD_DIM, t.shape[-1]), (0, 4, 1, 2, 3))[None]
                     for t in kv_t]
            xp, cs = _layer_prompt("proj", last, xp, mod_p[i], gains, fgain,
                                   (o_all.reshape(batch * seq, ATTN_INNER), wo), ffn)
            qkv_s = _proj_sample(xs, mod_s[i], gains, w.astype(BF16))
            q6 = qkv_s.reshape(steps, nb, 3, N_GROUPS, HEADS_PER_GROUP, HEAD_DIM)
            q_s = jnp.pad(jnp.transpose(q6[:, :, 0], (1, 2, 3, 0, 4)),
                          ((0, 0), (0, 0), (0, 0), (0, SUBLANES - steps), (0, 0)))
            q_s = q_s.reshape(nb, N_GROUPS, HEADS_PER_GROUP // 2, 2, SUBLANES, HEAD_DIM)
            zeros = jnp.zeros_like(q_s[:, :, :, 0])
            q_s = jnp.stack([jnp.concatenate([q_s[:, :, :, 0], zeros], axis=-1),
                             jnp.concatenate([zeros, q_s[:, :, :, 1]], axis=-1)], axis=3)
            q_s = q_s.reshape(nb, N_GROUPS, HEADS_PER_GROUP // 2, 2 * SUBLANES, LANES)
            kv_new = jnp.transpose(q6[:, :, 1:], (1, 3, 0, 2, 4, 5)).reshape(nb, N_GROUPS, steps, 2 * GROUP_WIDTH)
            kv_new = jnp.pad(kv_new, ((0, 0), (0, 0), (0, SUBLANES - steps), (0, 0)))
            past_rows = [c.shape[2] for c in caches]
            cache_t = [jnp.transpose(c[j], (0, 2, 3, 4, 1)).reshape(nb, 2 * HEADS_PER_GROUP, HEAD_DIM, p)
                       for c, p in zip(caches, past_rows)]
            bias_past, bias_new = _sample_bias_tables(gb, steps, past_rows)
            o8, *new_caches = _attn_sample(steps, q_s, kv_new, cache_t, bias_past, bias_new)
            a_s = jnp.transpose(o8[:, :steps], (1, 0, 2)).reshape(steps * nb, ATTN_INNER)
            win_s = [jnp.transpose(c.reshape(nb, 2, HEADS_PER_GROUP, HEAD_DIM, p), (0, 4, 1, 2, 3))[None]
                     for c, p in zip(new_caches, past_rows)]
            xs, css = _layer_sample("proj", last, xs, mod_s[i], gains, fgain, (a_s.astype(BF16), wo), ffn, conv_past)
        else:
            w = gla_w_in[j]
            n_main = 2 * GLA_QK + 2 * GLA_V
            weights = (w[:, :n_main].astype(BF16),
                       jnp.pad(w[:, n_main:], ((0, 0), (0, LANES - GATE_RANK))).astype(BF16),
                       jnp.pad(gla_w_gate_up[j], ((0, LANES - GATE_RANK), (0, 0))).astype(BF16),
                       gla_b_gate[j].reshape(1, GLA_QK))
            gain = gla_norm_gain[j].reshape(1, GLA_V)
            wo = gla_w_out[j].astype(BF16)
            q, k, v, r, la = _gla_proj(xp, mod_p[i], gains, weights, None, PROMPT_ROW_TILE)
            shp = lambda a: a.reshape(batch, seq, a.shape[-1])
            a_p, s_p = _gla_prompt(shp(q), shp(k), shp(v), shp(r), shp(la), gain)
            gla_p.append(s_p)
            xp, cs = _layer_prompt("proj", last, xp, mod_p[i], gains, fgain,
                                   (a_p.reshape(batch * seq, GLA_V), wo), ffn)
            outs = _gla_proj(xs, mod_s[i], gains, weights, nb, steps * nb)

            def per_seq(a):
                a = jnp.transpose(a.reshape(steps, nb, a.shape[-1]), (1, 0, 2))
                return jnp.pad(a, ((0, 0), (0, SAMPLE_DEC_PAD - steps), (0, 0)))

            qs, ks, vs, rs, las = (per_seq(a) for a in outs)
            a16, s_s = _gla_sample(qs, ks, vs, rs, las, gain, state_gla[j])
            gla_s.append(s_s)
            a_s = jnp.transpose(a16[:, :steps], (1, 0, 2)).reshape(steps * nb, GLA_V).astype(BF16)
            xs, css = _layer_sample("proj", last, xs, mod_s[i], gains, fgain, (a_s, wo), ffn, conv_past)
        conv_p.append(_conv_tail_prompt(cs))
        conv_s.append(_conv_tail_sample(css, nb))

    y_prompt = xp.reshape(batch, seq, D_MODEL)
    y_sample = jnp.transpose(xs.reshape(steps, nb, D_MODEL), (1, 0, 2))
    return (y_prompt, y_sample, jnp.stack(pool_p), jnp.stack(pool_s),
            win_p[0], win_s[0], win_p[1], win_s[1], win_p[2], win_s[2],
            jnp.stack(gla_p), jnp.stack(gla_s), jnp.stack(conv_p), jnp.stack(conv_s))
```

```python
import functools
import math

import numpy as np
import jax
import jax.numpy as jnp
from jax import lax
from jax.experimental import pallas as pl
from jax.experimental.pallas import tpu as pltpu

F32 = jnp.float32
BF16 = jnp.bfloat16

D_MODEL = 1024
DEPTH = 4
N_MOD = 6
EPS = 1e-6
NEG_INF = -1e30
POOL_WINDOWS = (2, 4, 8, 16)
POOL_GROUP_DIM = D_MODEL // len(POOL_WINDOWS)
POOL_STATE_ROWS = max(POOL_WINDOWS) - 1
POOL_CARRY_ROWS = 16
DILATED_GROUPS = ((128, 1), (512, 4), (2048, 16))
N_GROUPS = len(DILATED_GROUPS)
HEADS_PER_GROUP = 4
HEAD_DIM = 64
GROUP_WIDTH = HEADS_PER_GROUP * HEAD_DIM
ATTN_INNER = N_GROUPS * GROUP_WIDTH
KEYS_PER_QUERY = 129
QUERY_BLOCK = 128
NUM_BUCKETS = 32
MAX_DISTANCE = 2048
GLA_HEADS = 4
GLA_DK = 128
GLA_DV = 256
GLA_QK = GLA_HEADS * GLA_DK
GLA_V = GLA_HEADS * GLA_DV
GATE_RANK = 16
GATE_TAU = 16.0
GLA_CHUNK = 128
D_FF = 2816
CONV_WIDTH = 3

LANES = 128
SUBLANES = 8
FF_CHUNK = 256
N_FF_CHUNKS = D_FF // FF_CHUNK
VMEM_LIMIT_BYTES = 56 * 1024 * 1024
PROMPT_ROW_TILE = 1024
GLA_TIME_TILE = 512
ATTN_BLOCK_UNROLL = 16
GLA_CHUNK_UNROLL = 4
GLA_SAMPLE_UNROLL = 8
ATTN_SAMPLE_SEQS = 2
SAMPLE_DEC_PAD = 16


def _params(*semantics):
    return pltpu.CompilerParams(dimension_semantics=semantics, vmem_limit_bytes=VMEM_LIMIT_BYTES)


def _resident(shape):
    nd = len(shape)
    return pl.BlockSpec(shape, lambda *_: (0,) * nd, pipeline_mode=pl.Buffered(1))


def _dot(a, b):
    return jnp.dot(a, b, preferred_element_type=F32)


def _dot_nt(a, b):
    return lax.dot_general(a, b, (((1,), (1,)), ((), ())), preferred_element_type=F32)


def _dot_tn(a, b):
    return lax.dot_general(a, b, (((0,), (0,)), ((), ())), preferred_element_type=F32)


def _rms(x):
    return x * lax.rsqrt(jnp.mean(x * x, axis=-1, keepdims=True) + EPS)


def _bcast_rows(v, y, nb):
    if nb is None:
        return v * y
    rows, width = y.shape
    return (y.reshape(rows // nb, nb, width) * v[None]).reshape(rows, width)


def _norm_mod(x, gain, shift, scale, nb):
    y = _rms(x) * gain
    if nb is None:
        return y * (1.0 + scale) + shift
    rows, width = y.shape
    y3 = y.reshape(rows // nb, nb, width)
    return (y3 * (1.0 + scale)[None] + shift[None]).reshape(rows, width)


def _prompt_mod(mod_ref, b):
    return [mod_ref[k, pl.ds(b, 1), :] for k in range(N_MOD)]


def _gelu(x):
    return 0.5 * x * (1.0 + lax.erf(x * (1.0 / math.sqrt(2.0))))


def _silu(x):
    return x * jax.nn.sigmoid(x)


def _split_bf16(a):
    hi = a.astype(BF16)
    lo = (a - hi.astype(F32)).astype(BF16)
    return hi, lo


def _mod_kernel(cp_ref, cs_ref, w_ref, b_ref, op_ref, os_ref):
    batch = op_ref.shape[2]
    pad = cp_ref.shape[0]
    rows = pad + cs_ref.shape[0]
    w_hi, w_lo = _split_bf16(w_ref[0])
    a_hi, a_lo = _split_bf16(_silu(jnp.concatenate([cp_ref[...], cs_ref[...]], axis=0)))
    both = _dot(jnp.concatenate([a_hi, a_lo], axis=0), w_hi)
    out = both[0:rows] + both[rows:2 * rows] + _dot(a_hi, w_lo) + b_ref[0]
    op_ref[0, 0] = out[0:batch]
    os_ref[0, 0] = out[pad:rows]


def _modulation(c_prompt, c_sample, w_ada, b_ada):
    batch, nb = c_prompt.shape[0], c_sample.shape[0]
    bf16_rows = 2 * SUBLANES
    c_prompt = jnp.pad(c_prompt, ((0, -batch % bf16_rows), (0, 0)))
    return pl.pallas_call(
        _mod_kernel,
        grid=(DEPTH, N_MOD),
        in_specs=[pl.BlockSpec(c_prompt.shape, lambda l, m: (0, 0)),
                  pl.BlockSpec((nb, D_MODEL), lambda l, m: (0, 0)),
                  pl.BlockSpec((1, D_MODEL, D_MODEL), lambda l, m: (l, 0, m)),
                  pl.BlockSpec((1, 1, D_MODEL), lambda l, m: (l, 0, m))],
        out_specs=[pl.BlockSpec((1, 1, batch, D_MODEL), lambda l, m: (l, m, 0, 0)),
                   pl.BlockSpec((1, 1, nb, D_MODEL), lambda l, m: (l, m, 0, 0))],
        out_shape=[jax.ShapeDtypeStruct((DEPTH, N_MOD, batch, D_MODEL), F32),
                   jax.ShapeDtypeStruct((DEPTH, N_MOD, nb, D_MODEL), F32)],
        compiler_params=_params("parallel", "parallel"),
        name="adaln_mod",
    )(c_prompt, c_sample, w_ada, b_ada.reshape(DEPTH, 1, N_MOD * D_MODEL))


def _ffn_chunk_math(g, g_m1, g_m2, u, cw, cb):
    gc = cw[2:3] * g + cw[1:2] * g_m1 + cw[0:1] * g_m2 + cb
    return (_gelu(gc) * u).astype(BF16)


def _ffn_chunk_weights(win_ref, wd_ref, cw_ref, cb_ref, j):
    cols = slice(j * FF_CHUNK, (j + 1) * FF_CHUNK)
    ucols = slice(D_FF + j * FF_CHUNK, D_FF + (j + 1) * FF_CHUNK)
    return win_ref[0, :, cols], win_ref[0, :, ucols], wd_ref[0, cols, :], cw_ref[0, :, cols], cb_ref[0, :, cols]


def _layer_resident(array, layer):
    nd = array.ndim
    return pl.BlockSpec((1,) + array.shape[1:], lambda *_: (layer,) + (0,) * (nd - 1),
                        pipeline_mode=pl.Buffered(1))


def _layer_prompt_kernel(mixer, last, tm, tiles_per_seq, *refs):
    refs = list(refs)
    x_ref, mod_ref, gains_ref, fg_ref = refs[:4]
    refs = refs[4:]
    if mixer == "pool":
        pw_ref, ps_ref = refs[:2]
    else:
        a_ref, wp_ref = refs[:2]
    win_ref, wd_ref, cw_ref, cb_ref = refs[2:6]
    refs = refs[6:]
    if mixer == "pool":
        y_ref, cs_ref, pst_ref, h2_ref, act_ref, gext_ref, cc_ref, hext_ref = refs
    else:
        y_ref, cs_ref, h2_ref, act_ref, gext_ref, cc_ref = refs

    i = pl.program_id(0)
    tile_in_seq = i % tiles_per_seq

    @pl.when(tile_in_seq == 0)
    def _():
        cc_ref[...] = jnp.zeros_like(cc_ref)
        if mixer == "pool":
            hext_ref[0:POOL_CARRY_ROWS, :] = jnp.zeros((POOL_CARRY_ROWS, D_MODEL), F32)

    x = x_ref[...]
    m = _prompt_mod(mod_ref, i // tiles_per_seq)
    gains = gains_ref[...]

    if mixer == "pool":
        h = _norm_mod(x, gains[0:1], m[0], m[1], None)
        hext_ref[POOL_CARRY_ROWS:, :] = h
        pos = tile_in_seq * tm + lax.broadcasted_iota(jnp.int32, (tm, 1), 0)
        parts = []
        for g, w in enumerate(POOL_WINDOWS):
            cols = slice(g * POOL_GROUP_DIM, (g + 1) * POOL_GROUP_DIM)
            s = hext_ref[:, cols]
            span = 1
            while span < w:
                s = s + pltpu.roll(s, span, 0)
                span *= 2
            inv_count = 1.0 / jnp.minimum(pos + 1, w).astype(F32)
            d = s[POOL_CARRY_ROWS:, :] * inv_count - h[:, cols]
            parts.append(_dot(d.astype(BF16), pw_ref[g]))
        mix = jnp.concatenate(parts, axis=-1) * ps_ref[...]
        tail = hext_ref[pl.ds(tm, POOL_CARRY_ROWS), :]
        hext_ref[0:POOL_CARRY_ROWS, :] = tail
        pst_ref[0] = tail
    else:
        mix = _dot(a_ref[...], wp_ref[...])

    x1 = x + m[2] * mix
    y_ref[...] = x1
    h2_ref[...] = _norm_mod(x1, gains[1:2], m[3], m[4], None).astype(BF16)

    def up_proj(j):
        wg, wu, _, _, _ = _ffn_chunk_weights(win_ref, wd_ref, cw_ref, cb_ref, j)
        h2 = h2_ref[...]
        return _dot(h2, wg), _dot(h2, wu)

    ahead = up_proj(0)
    for j in range(N_FF_CHUNKS):
        g, u = ahead
        if j + 1 < N_FF_CHUNKS:
            ahead = up_proj(j + 1)
        _, _, _, cw, cb = _ffn_chunk_weights(win_ref, wd_ref, cw_ref, cb_ref, j)
        gx = gext_ref.at[j % 2]
        gx[0:SUBLANES, :] = cc_ref[j]
        gx[SUBLANES:, :] = g
        act_ref[:, j * FF_CHUNK:(j + 1) * FF_CHUNK] = _ffn_chunk_math(
            g, gx[pl.ds(SUBLANES - 1, tm), :], gx[pl.ds(SUBLANES - 2, tm), :], u, cw, cb)
        tail = g[tm - SUBLANES:tm, :]
        cc_ref[j] = tail
        cs_ref[0, j] = tail
    xo = y_ref[...] + m[5] * _dot(act_ref[...], wd_ref[0])
    if last:
        xo = _rms(xo) * fg_ref[...]
    y_ref[...] = xo


def _layer_prompt(mixer, last, x, mod, gains, fgain, mix_args, ffn):
    n = x.shape[0]
    batch = mod.shape[1]
    seq = n // batch
    tm = PROMPT_ROW_TILE
    tps = seq // tm
    in_specs = [pl.BlockSpec((tm, D_MODEL), lambda i: (i, 0)),
                _resident(mod.shape), _resident((2, D_MODEL)), _resident((1, D_MODEL))]
    if mixer == "pool":
        pw, ps = mix_args
        in_specs += [_resident(pw.shape), _resident(ps.shape)]
    else:
        a, wp = mix_args
        in_specs += [pl.BlockSpec((tm, a.shape[1]), lambda i: (i, 0)), _resident(wp.shape)]
    layer, ffn_w = ffn
    in_specs += [_layer_resident(w, layer) for w in ffn_w]
    out_shape = [jax.ShapeDtypeStruct((n, D_MODEL), F32),
                 jax.ShapeDtypeStruct((batch, N_FF_CHUNKS, SUBLANES, FF_CHUNK), F32)]
    out_specs = [pl.BlockSpec((tm, D_MODEL), lambda i: (i, 0)),
                 pl.BlockSpec((1, N_FF_CHUNKS, SUBLANES, FF_CHUNK), lambda i: (i // tps, 0, 0, 0))]
    scratch = [pltpu.VMEM((tm, D_MODEL), BF16), pltpu.VMEM((tm, D_FF), BF16),
               pltpu.VMEM((2, tm + SUBLANES, FF_CHUNK), F32), pltpu.VMEM((N_FF_CHUNKS, SUBLANES, FF_CHUNK), F32)]
    if mixer == "pool":
        out_shape.append(jax.ShapeDtypeStruct((batch, POOL_CARRY_ROWS, D_MODEL), F32))
        out_specs.append(pl.BlockSpec((1, POOL_CARRY_ROWS, D_MODEL), lambda i: (i // tps, 0, 0)))
        scratch.append(pltpu.VMEM((tm + POOL_CARRY_ROWS, D_MODEL), F32))
    return pl.pallas_call(
        functools.partial(_layer_prompt_kernel, mixer, last, tm, tps),
        grid=(n // tm,), in_specs=in_specs, out_specs=out_specs, out_shape=out_shape,
        scratch_shapes=scratch, compiler_params=_params("arbitrary"),
        name=f"layer_prompt_{mixer}",
    )(x, mod, gains, fgain, *mix_args, *ffn_w)


def _layer_sample_kernel(mixer, last, nb, steps, *refs):
    refs = list(refs)
    x_ref, mod_ref, gains_ref, fg_ref = refs[:4]
    refs = refs[4:]
    if mixer == "pool":
        pw_ref, ps_ref, ppast_ref = refs[:3]
        refs = refs[3:]
    else:
        a_ref, wp_ref = refs[:2]
        refs = refs[2:]
    win_ref, wd_ref, cw_ref, cb_ref, cpast_ref = refs[:5]
    refs = refs[5:]
    if mixer == "pool":
        y_ref, cs_ref, pst_ref, h2_ref, act_ref = refs
    else:
        y_ref, cs_ref, h2_ref, act_ref = refs
    rows = steps * nb

    x = x_ref[...]
    gains = gains_ref[...]
    if mixer == "pool":
        h = _norm_mod(x, gains[0:1], mod_ref[0], mod_ref[1], nb)
        new = [h[t * nb:(t + 1) * nb, :] for t in range(steps)]

        def u_rows(p, cols):
            if p < POOL_STATE_ROWS:
                return ppast_ref[p, :, cols]
            return new[p - POOL_STATE_ROWS][:, cols]

        parts = []
        for g, w in enumerate(POOL_WINDOWS):
            cols = slice(g * POOL_GROUP_DIM, (g + 1) * POOL_GROUP_DIM)
            ds = []
            for t in range(steps):
                s = u_rows(POOL_STATE_ROWS + t, cols)
                for k in range(1, w):
                    s = s + u_rows(POOL_STATE_ROWS + t - k, cols)
                ds.append(s * (1.0 / w) - new[t][:, cols])
            parts.append(_dot(jnp.concatenate(ds, axis=0).astype(BF16), pw_ref[g]))
        mix = jnp.concatenate(parts, axis=-1) * ps_ref[...]
        full = slice(0, D_MODEL)
        for p in range(POOL_STATE_ROWS):
            pst_ref[p] = u_rows(p + steps, full)
    else:
        mix = _dot(a_ref[...], wp_ref[...])

    x1 = x + _bcast_rows(mod_ref[2], mix, nb)
    y_ref[...] = x1
    h2_ref[...] = _norm_mod(x1, gains[1:2], mod_ref[3], mod_ref[4], nb).astype(BF16)
    past_rows = (CONV_WIDTH - 1) * nb

    for j in range(N_FF_CHUNKS):
        h2 = h2_ref[...]
        wg, wu, _, cw, cb = _ffn_chunk_weights(win_ref, wd_ref, cw_ref, cb_ref, j)
        g = _dot(h2, wg)
        u = _dot(h2, wu)
        gall = jnp.concatenate([cpast_ref[j], g], axis=0)
        act_ref[:, j * FF_CHUNK:(j + 1) * FF_CHUNK] = _ffn_chunk_math(
            g, gall[nb:nb + rows, :], gall[0:rows, :], u, cw, cb)
        cs_ref[j] = gall[rows:rows + past_rows, :]
    xo = y_ref[...] + _bcast_rows(mod_ref[5], _dot(act_ref[...], wd_ref[0]), nb)
    if last:
        xo = _rms(xo) * fg_ref[...]
    y_ref[...] = xo


def _layer_sample(mixer, last, x, mod, gains, fgain, mix_args, ffn, conv_past):
    rows = x.shape[0]
    nb = mod.shape[1]
    steps = rows // nb
    layer, ffn_w = ffn
    head = [x, mod, gains, fgain, *mix_args]
    args = [*head, *ffn_w, conv_past]
    out_shape = [jax.ShapeDtypeStruct((rows, D_MODEL), F32),
                 jax.ShapeDtypeStruct(conv_past.shape, F32)]
    if mixer == "pool":
        out_shape.append(jax.ShapeDtypeStruct((POOL_STATE_ROWS, nb, D_MODEL), F32))
    return pl.pallas_call(
        functools.partial(_layer_sample_kernel, mixer, last, nb, steps),
        grid=(1,),
        in_specs=[_resident(a.shape) for a in head] + [_layer_resident(w, layer) for w in ffn_w]
                 + [_resident(conv_past.shape)],
        out_specs=[pl.BlockSpec(s.shape, functools.partial(lambda nd, i: (0,) * nd, len(s.shape))) for s in out_shape],
        out_shape=out_shape,
        scratch_shapes=[pltpu.VMEM((rows, D_MODEL), BF16), pltpu.VMEM((rows, D_FF), BF16)],
        compiler_params=_params("arbitrary"),
        name=f"layer_sample_{mixer}",
    )(*args)


def _qkv_prompt_kernel(seq, x_ref, mod_ref, gains_ref, w_ref, qkvp_ref, kt1_ref, kt2_ref, kt3_ref, h_ref, slab_ref):
    g = pl.program_id(1)
    kt_refs = (kt1_ref, kt2_ref, kt3_ref)

    @pl.when(g == 0)
    def _():
        m = _prompt_mod(mod_ref, pl.program_id(0))
        h_ref[...] = _norm_mod(x_ref[0], gains_ref[0:1, :], m[0], m[1], None).astype(BF16)

    h = h_ref[...]
    for c in range(3):
        r = _dot(h, w_ref[0, :, c * GROUP_WIDTH:(c + 1) * GROUP_WIDTH])
        if c == 0:
            r = r * (HEAD_DIM ** -0.5)
        slab_ref[2 * c] = r[:, 0:LANES]
        slab_ref[2 * c + 1] = r[:, LANES:2 * LANES]

    qkvp_ref[0, 0, 0:QUERY_BLOCK, :] = jnp.zeros((QUERY_BLOCK, 3 * GROUP_WIDTH), BF16)
    for gi, (_, dil) in enumerate(DILATED_GROUPS):
        @pl.when(g == gi)
        def _(dil=dil):
            per_class = seq // dil
            for s in range(6):
                for rho in range(dil):
                    if dil == 1:
                        v = slab_ref[s]
                    else:
                        v = slab_ref[s, pl.ds(rho, per_class, stride=dil), :]
                    r0 = QUERY_BLOCK + rho * per_class
                    qkvp_ref[0, 0, r0:r0 + per_class, s * LANES:(s + 1) * LANES] = v.astype(BF16)

    for gi, kt_ref in enumerate(kt_refs):
        @pl.when(g == gi)
        def _(kt_ref=kt_ref):
            keep = kt_ref.shape[-1]
            for s in range(2, 6):
                tile_t = slab_ref[s, seq - keep:seq, :].T
                for r in range(2):
                    kt_ref[0, 2 * (s - 2) + r] = tile_t[r * HEAD_DIM:(r + 1) * HEAD_DIM, :]


def _qkv_prompt(x3, mod, gains, w3):
    batch, seq, _ = x3.shape
    width = 3 * GROUP_WIDTH
    keeps = [min(window, seq) for window, _ in DILATED_GROUPS]
    return pl.pallas_call(
        functools.partial(_qkv_prompt_kernel, seq),
        grid=(batch, N_GROUPS),
        in_specs=[pl.BlockSpec((1, seq, D_MODEL), lambda b, g: (b, 0, 0)),
                  pl.BlockSpec(mod.shape, lambda b, g: (0, 0, 0)),
                  pl.BlockSpec((2, D_MODEL), lambda b, g: (0, 0)),
                  pl.BlockSpec((1, D_MODEL, width), lambda b, g: (g, 0, 0))],
        out_specs=[pl.BlockSpec((1, 1, seq + QUERY_BLOCK, width), lambda b, g: (b, g, 0, 0))]
                  + [pl.BlockSpec((1, 2 * HEADS_PER_GROUP, HEAD_DIM, keep), lambda b, g: (b, 0, 0, 0)) for keep in keeps],
        out_shape=[jax.ShapeDtypeStruct((batch, N_GROUPS, seq + QUERY_BLOCK, width), BF16)]
                  + [jax.ShapeDtypeStruct((batch, 2 * HEADS_PER_GROUP, HEAD_DIM, keep), F32) for keep in keeps],
        scratch_shapes=[pltpu.VMEM((seq, D_MODEL), BF16), pltpu.VMEM((6, seq, LANES), F32)],
        compiler_params=_params("arbitrary", "arbitrary"),
        name="qkv_prompt",
    )(x3, mod, gains, w3)


def _head_lane_mask(rows, h):
    lane = lax.broadcasted_iota(jnp.int32, (rows, GROUP_WIDTH), 1)
    return (lane >= h * HEAD_DIM) & (lane < (h + 1) * HEAD_DIM)


def _attn_block(q, k, v, bias_ref, cols):
    masks = [_head_lane_mask(QUERY_BLOCK, h) for h in range(HEADS_PER_GROUP)]
    qs = jnp.concatenate([jnp.where(hm, q, jnp.zeros_like(q)) for hm in masks], axis=0)
    s = _dot_nt(qs, k) + bias_ref[:, cols]
    m = jnp.max(s, axis=-1, keepdims=True)
    p = jnp.exp(s - m)
    l = jnp.sum(p, axis=-1, keepdims=True)
    pv = _dot(p.astype(BF16), v) * (1.0 / l)
    lse_rows = jnp.broadcast_to(m + jnp.log(l), pv.shape)
    o = pv[0:QUERY_BLOCK]
    lse = lse_rows[0:QUERY_BLOCK]
    for h in range(1, HEADS_PER_GROUP):
        rows = slice(h * QUERY_BLOCK, (h + 1) * QUERY_BLOCK)
        o = jnp.where(masks[h], pv[rows], o)
        lse = jnp.where(masks[h], lse_rows[rows], lse)
    return o, lse


def _attn_prompt_kernel(seq, qkv_ref, brow_ref, o_ref, os_ref, ls_ref, bias_ref):
    @pl.when(pl.program_id(0) == 0)
    def _():
        left = lax.broadcasted_iota(jnp.int32, (QUERY_BLOCK, 2 * QUERY_BLOCK), 1) < QUERY_BLOCK
        for g in range(N_GROUPS):
            for h in range(HEADS_PER_GROUP):
                base = jnp.broadcast_to(brow_ref[g, h], (QUERY_BLOCK, 2 * QUERY_BLOCK))
                band = pltpu.roll(base, 0, 1, stride=1, stride_axis=0)
                rows = slice(h * QUERY_BLOCK, (h + 1) * QUERY_BLOCK)
                bias_ref[g, 0, rows, :] = band
                bias_ref[g, 1, rows, :] = jnp.where(left, NEG_INF, band)

    qc = slice(0, GROUP_WIDTH)
    kc = slice(GROUP_WIDTH, 2 * GROUP_WIDTH)
    vc = slice(2 * GROUP_WIDTH, 3 * GROUP_WIDTH)
    n_blocks = seq // QUERY_BLOCK
    for g, (_, dil) in enumerate(DILATED_GROUPS):
        blocks_per_class = n_blocks // dil

        def do_block(blk, carry, g=g, dil=dil, blocks_per_class=blocks_per_class):
            r0 = pl.multiple_of(blk * QUERY_BLOCK, QUERY_BLOCK)
            cur = pl.ds(r0 + QUERY_BLOCK, QUERY_BLOCK)
            q = qkv_ref[0, g, cur, qc]
            rho = blk // blocks_per_class
            in_class = blk % blocks_per_class
            if blocks_per_class == 1:
                o, lse = _attn_block(q, qkv_ref[0, g, cur, kc], qkv_ref[0, g, cur, vc],
                                     bias_ref.at[g, 0], slice(QUERY_BLOCK, 2 * QUERY_BLOCK))
            else:
                both = pl.ds(r0, 2 * QUERY_BLOCK)
                first = jnp.asarray(in_class == 0, jnp.int32)
                o, lse = _attn_block(q, qkv_ref[0, g, both, kc], qkv_ref[0, g, both, vc],
                                     bias_ref.at[g, first], slice(0, 2 * QUERY_BLOCK))
            start = in_class * (QUERY_BLOCK * dil) + rho
            for s in range(2):
                cols = slice(s * LANES, (s + 1) * LANES)
                if dil == 1:
                    os_ref[g, s, pl.ds(r0, QUERY_BLOCK), :] = o[:, cols]
                    ls_ref[g, s, pl.ds(r0, QUERY_BLOCK), :] = lse[:, cols]
                else:
                    os_ref[g, s, pl.ds(start, QUERY_BLOCK, stride=dil), :] = o[:, cols]
                    ls_ref[g, s, pl.ds(start, QUERY_BLOCK, stride=dil), :] = lse[:, cols]
            return carry

        lax.fori_loop(0, n_blocks, do_block, 0, unroll=ATTN_BLOCK_UNROLL)

    def merge(i, carry):
        r0 = pl.multiple_of(i * QUERY_BLOCK, QUERY_BLOCK)
        rows = pl.ds(r0, QUERY_BLOCK)
        for s in range(2):
            ls = [ls_ref[g, s, rows, :] for g in range(N_GROUPS)]
            mx = jnp.maximum(jnp.maximum(ls[0], ls[1]), ls[2])
            es = [jnp.exp(l - mx) for l in ls]
            inv = 1.0 / (es[0] + es[1] + es[2])
            for g in range(N_GROUPS):
                c0 = g * GROUP_WIDTH + s * LANES
                o_ref[0, rows, c0:c0 + LANES] = (os_ref[g, s, rows, :] * (es[g] * inv)).astype(BF16)
        return carry

    lax.fori_loop(0, seq // QUERY_BLOCK, merge, 0)


def _attn_prompt(qkvp, bias_rows):
    batch, _, padded, width = qkvp.shape
    seq = padded - QUERY_BLOCK
    return pl.pallas_call(
        functools.partial(_attn_prompt_kernel, seq),
        grid=(batch,),
        in_specs=[pl.BlockSpec((1, N_GROUPS, padded, width), lambda b: (b, 0, 0, 0)),
                  _resident(bias_rows.shape)],
        out_specs=pl.BlockSpec((1, seq, ATTN_INNER), lambda b: (b, 0, 0)),
        out_shape=jax.ShapeDtypeStruct((batch, seq, ATTN_INNER), BF16),
        scratch_shapes=[pltpu.VMEM((N_GROUPS, 2, seq, LANES), F32), pltpu.VMEM((N_GROUPS, 2, seq, LANES), F32),
                        pltpu.VMEM((N_GROUPS, 2, HEADS_PER_GROUP * QUERY_BLOCK, 2 * QUERY_BLOCK), F32)],
        compiler_params=_params("arbitrary"),
        name="attn_prompt",
    )(qkvp, bias_rows)


def _proj_sample_kernel(nb, x_ref, mod_ref, gains_ref, w_ref, o_ref):
    h = _norm_mod(x_ref[...], gains_ref[0:1, :], mod_ref[0], mod_ref[1], nb).astype(BF16)
    o_ref[...] = _dot(h, w_ref[...])


def _proj_sample(x, mod, gains, w):
    rows = x.shape[0]
    nb = mod.shape[1]
    args = [x, mod, gains, w]
    return pl.pallas_call(
        functools.partial(_proj_sample_kernel, nb),
        grid=(1,),
        in_specs=[_resident(a.shape) for a in args],
        out_specs=pl.BlockSpec((rows, w.shape[1]), lambda i: (0, 0)),
        out_shape=jax.ShapeDtypeStruct((rows, w.shape[1]), F32),
        compiler_params=_params("arbitrary"),
        name="qkv_sample",
    )(*args)


def _split3_bf16(a):
    hi = a.astype(BF16)
    r1 = a - hi.astype(F32)
    mid = r1.astype(BF16)
    lo = (r1 - mid.astype(F32)).astype(BF16)
    return hi, mid, lo


def _attn_sample_kernel(steps, q_ref, kvn_ref, c1_ref, c2_ref, c3_ref, t1_ref, t2_ref, t3_ref, bn_ref,
                        o_ref, n1_ref, n2_ref, n3_ref):
    for i in range(q_ref.shape[0]):
        _attn_sample_one(steps, i, q_ref, kvn_ref, ((c1_ref, t1_ref, n1_ref), (c2_ref, t2_ref, n2_ref),
                                                  (c3_ref, t3_ref, n3_ref)), bn_ref, o_ref)


def _attn_sample_one(steps, i, q_ref, kvn_ref, groups, bn_ref, o_ref):
    pairs = HEADS_PER_GROUP // 2
    outs, lses = [], []
    lane = lax.broadcasted_iota(jnp.int32, (SUBLANES, LANES), 1)
    row = lax.broadcasted_iota(jnp.int32, (SUBLANES, LANES), 0)
    own_lanes = lane < HEAD_DIM
    sel_head = jnp.where((lane == row) & (row < steps), 1.0, 0.0).astype(BF16)
    sel_tail = jnp.where((lane == row + (LANES - steps)) & (row < steps), 1.0, 0.0).astype(BF16)
    for g, (c_ref, t_ref, n_ref) in enumerate(groups):
        p_rows = c_ref.shape[-1]
        pieces = _split3_bf16(kvn_ref[i, g])
        new_head = sum(_dot_tn(x, sel_head) for x in pieces)
        new_tail = sum(_dot_tn(x, sel_tail) for x in pieces)
        o_pairs, l_pairs = [], []
        for j in range(pairs):
            q = (q_ref[i, g, j] * (HEAD_DIM ** -0.5)).astype(BF16)
            kt = c_ref[i, 2 * j:2 * j + 2].reshape(LANES, p_rows).astype(BF16)
            vt = c_ref[i, HEADS_PER_GROUP + 2 * j:HEADS_PER_GROUP + 2 * j + 2].reshape(LANES, p_rows).astype(BF16)
            k_new = new_head[j * LANES:(j + 1) * LANES].astype(BF16)
            v_new = new_head[(pairs + j) * LANES:(pairs + j + 1) * LANES].astype(BF16)
            s = _dot(q, kt) + t_ref[j]
            s_new = _dot(q, k_new) + bn_ref[g, j]
            m = jnp.maximum(jnp.max(s, axis=-1, keepdims=True), jnp.max(s_new, axis=-1, keepdims=True))
            p = jnp.exp(s - m)
            p_new = jnp.exp(s_new - m)
            l = jnp.sum(p, axis=-1, keepdims=True) + jnp.sum(p_new, axis=-1, keepdims=True)
            o = _dot_nt(p.astype(BF16), vt) + _dot_nt(p_new.astype(BF16), v_new)
            o_pairs.append(o * (1.0 / l))
            l_pairs.append(m + jnp.log(l))
            for idx in (2 * j, 2 * j + 1, HEADS_PER_GROUP + 2 * j, HEADS_PER_GROUP + 2 * j + 1):
                n_ref[i, idx] = pltpu.roll(c_ref[i, idx], p_rows - steps, 1)
                n_ref[i, idx, :, p_rows - steps:p_rows] = new_tail[idx * HEAD_DIM:(idx + 1) * HEAD_DIM, LANES - steps:LANES]
        outs.append(o_pairs)
        lses.append(l_pairs)
    for j in range(pairs):
        ls = [lses[g][j] for g in range(N_GROUPS)]
        mx = jnp.maximum(jnp.maximum(ls[0], ls[1]), ls[2])
        es = [jnp.exp(l - mx) for l in ls]
        inv = 1.0 / (es[0] + es[1] + es[2])
        for g in range(N_GROUPS):
            og = outs[g][j] * (es[g] * inv)
            c0 = g * GROUP_WIDTH + j * LANES
            o_ref[i, :, c0:c0 + LANES] = jnp.where(own_lanes, og[0:SUBLANES], og[SUBLANES:2 * SUBLANES])


def _attn_sample(steps, q, kv_new, caches, tables, bias_new):
    nb = q.shape[0]
    sb = math.gcd(nb, ATTN_SAMPLE_SEQS)
    in_specs = [pl.BlockSpec((sb,) + q.shape[1:], lambda b: (b, 0, 0, 0, 0)),
                pl.BlockSpec((sb,) + kv_new.shape[1:], lambda b: (b, 0, 0, 0))]
    cache_specs = [pl.BlockSpec((sb,) + c.shape[1:], lambda b: (b, 0, 0, 0)) for c in caches]
    in_specs += cache_specs + [_resident(t.shape) for t in tables] + [_resident(bias_new.shape)]
    return pl.pallas_call(
        functools.partial(_attn_sample_kernel, steps),
        grid=(nb // sb,), in_specs=in_specs,
        out_specs=[pl.BlockSpec((sb, SUBLANES, ATTN_INNER), lambda b: (b, 0, 0))] + cache_specs,
        out_shape=[jax.ShapeDtypeStruct((nb, SUBLANES, ATTN_INNER), F32)]
                  + [jax.ShapeDtypeStruct(c.shape, F32) for c in caches],
        compiler_params=_params("arbitrary"),
        name="attn_sample",
    )(q, kv_new, *caches, *tables, bias_new)


def _gla_proj_kernel(nb, tiles_per_seq, x_ref, mod_ref, gains_ref, w_ref, wgd_ref, wgu_ref, bg_ref,
                     q_ref, k_ref, v_ref, r_ref, la_ref):
    if nb is None:
        m = _prompt_mod(mod_ref, pl.program_id(0) // tiles_per_seq)
        shift, scale = m[0], m[1]
    else:
        shift, scale = mod_ref[0], mod_ref[1]
    h = _norm_mod(x_ref[...], gains_ref[0:1, :], shift, scale, nb).astype(BF16)
    q_ref[...] = _dot(h, w_ref[:, 0:GLA_QK]) * (GLA_DK ** -0.5)
    k_ref[...] = _dot(h, w_ref[:, GLA_QK:2 * GLA_QK])
    v_ref[...] = _dot(h, w_ref[:, 2 * GLA_QK:2 * GLA_QK + GLA_V]).astype(BF16)
    r_ref[...] = _dot(h, w_ref[:, 2 * GLA_QK + GLA_V:2 * GLA_QK + 2 * GLA_V])
    gd = _dot(h, wgd_ref[...])
    gate = _dot(gd.astype(BF16), wgu_ref[...]) + bg_ref[...]
    la_ref[...] = jax.nn.log_sigmoid(gate) * (1.0 / GATE_TAU)


def _gla_proj(x, mod, gains, weights, nb, tm):
    rows = x.shape[0]
    tps = None if nb is not None else rows // mod.shape[1] // tm
    widths = (GLA_QK, GLA_QK, GLA_V, GLA_V, GLA_QK)
    dtypes = (F32, F32, BF16, F32, F32)
    return pl.pallas_call(
        functools.partial(_gla_proj_kernel, nb, tps),
        grid=(rows // tm,),
        in_specs=[pl.BlockSpec((tm, D_MODEL), lambda i: (i, 0)), _resident(mod.shape), _resident((2, D_MODEL))]
                 + [_resident(w.shape) for w in weights],
        out_specs=[pl.BlockSpec((tm, w), lambda i: (i, 0)) for w in widths],
        out_shape=[jax.ShapeDtypeStruct((rows, w), dt) for w, dt in zip(widths, dtypes)],
        compiler_params=_params("arbitrary"),
        name="gla_proj",
    )(x, mod, gains, *weights)


def _cumsum_rows(g):
    rows = g.shape[0]
    row = lax.broadcasted_iota(jnp.int32, g.shape, 0)
    b = g
    shift = 1
    while shift < rows:
        b = b + jnp.where(row >= shift, pltpu.roll(b, shift, 0), 0.0)
        shift *= 2
    return b


def _gla_chunk(q, k, v, g, r, gain, mid, get_state, set_state):
    c = q.shape[0]
    b = _cumsum_rows(g)
    b_end = b[c - 1:c, :]
    b_mid = b[mid:mid + 1, :]
    q_in = (q * jnp.exp(b)).astype(BF16)
    q_rel = (q * jnp.exp(b - b_mid)).astype(BF16)
    k_rel = (k * jnp.exp(b_mid - b)).astype(BF16)
    k_out = (k * jnp.exp(b_end - b)).astype(BF16)
    decay_end = jnp.broadcast_to(jnp.exp(b_end), (SUBLANES, GLA_QK))
    ti = lax.broadcasted_iota(jnp.int32, (c, c), 0)
    si = lax.broadcasted_iota(jnp.int32, (c, c), 1)
    outs = []
    for h in range(GLA_HEADS):
        ks = slice(h * GLA_DK, (h + 1) * GLA_DK)
        vs = slice(h * GLA_DV, (h + 1) * GLA_DV)
        state = get_state(h)
        o = _dot(q_in[:, ks], state.astype(BF16))
        att = jnp.where(si <= ti, _dot_nt(q_rel[:, ks], k_rel[:, ks]), 0.0)
        o = o + _dot(att.astype(BF16), v[:, vs])
        decay_col = decay_end[:, ks].T[:, 0:1]
        set_state(h, decay_col * state + _dot_tn(k_out[:, ks], v[:, vs]))
        outs.append(_rms(o) * gain[:, vs] * _silu(r[:, vs]))
    return jnp.concatenate(outs, axis=-1)


def _gla_prompt_kernel(tt, q_ref, k_ref, v_ref, r_ref, la_ref, gain_ref, a_ref, so_ref, s_ref):
    t = pl.program_id(1)

    @pl.when(t == 0)
    def _():
        s_ref[...] = jnp.zeros_like(s_ref)

    gain = gain_ref[...]

    def get_state(h):
        return s_ref[h]

    def set_state(h, val):
        s_ref[h] = val

    def chunk(ci, carry):
        rows = pl.ds(pl.multiple_of(ci * GLA_CHUNK, GLA_CHUNK), GLA_CHUNK)
        a = _gla_chunk(q_ref[0, rows, :], k_ref[0, rows, :], v_ref[0, rows, :], la_ref[0, rows, :],
                       r_ref[0, rows, :], gain, GLA_CHUNK // 2, get_state, set_state)
        a_ref[0, rows, :] = a.astype(BF16)
        return carry

    lax.fori_loop(0, tt // GLA_CHUNK, chunk, 0, unroll=GLA_CHUNK_UNROLL)

    @pl.when(t == pl.num_programs(1) - 1)
    def _():
        so_ref[0] = s_ref[...]


def _gla_prompt(q, k, v, r, la, gain):
    batch, seq, _ = q.shape
    tt = GLA_TIME_TILE

    def spec(width):
        return pl.BlockSpec((1, tt, width), lambda b, t: (b, t, 0))

    return pl.pallas_call(
        functools.partial(_gla_prompt_kernel, tt),
        grid=(batch, seq // tt),
        in_specs=[spec(GLA_QK), spec(GLA_QK), spec(GLA_V), spec(GLA_V), spec(GLA_QK),
                  pl.BlockSpec((1, GLA_V), lambda b, t: (0, 0))],
        out_specs=[spec(GLA_V), pl.BlockSpec((1, GLA_HEADS, GLA_DK, GLA_DV), lambda b, t: (b, 0, 0, 0))],
        out_shape=[jax.ShapeDtypeStruct((batch, seq, GLA_V), BF16),
                   jax.ShapeDtypeStruct((batch, GLA_HEADS, GLA_DK, GLA_DV), F32)],
        scratch_shapes=[pltpu.VMEM((GLA_HEADS, GLA_DK, GLA_DV), F32)],
        compiler_params=_params("arbitrary", "arbitrary"),
        name="gla_prompt",
    )(q, k, v, r, la, gain)


def _gla_sample_kernel(sb, q_ref, k_ref, v_ref, r_ref, la_ref, gain_ref, s0_ref, a_ref, so_ref):
    gain = gain_ref[...]

    def seq_body(i, carry):
        def get_state(h):
            return s0_ref[i, h]

        def set_state(h, val):
            so_ref[i, h] = val

        a_ref[i] = _gla_chunk(q_ref[i], k_ref[i], v_ref[i], la_ref[i], r_ref[i], gain, 0, get_state, set_state)
        return carry

    lax.fori_loop(0, sb, seq_body, 0, unroll=GLA_SAMPLE_UNROLL)


def _gla_sample(q, k, v, r, la, gain, s0):
    nb, pad, _ = q.shape
    sb = math.gcd(nb, 8)

    def spec(width):
        return pl.BlockSpec((sb, pad, width), lambda i: (i, 0, 0))

    state_spec = pl.BlockSpec((sb, GLA_HEADS, GLA_DK, GLA_DV), lambda i: (i, 0, 0, 0))
    return pl.pallas_call(
        functools.partial(_gla_sample_kernel, sb),
        grid=(nb // sb,),
        in_specs=[spec(GLA_QK), spec(GLA_QK), spec(GLA_V), spec(GLA_V), spec(GLA_QK),
                  pl.BlockSpec((1, GLA_V), lambda i: (0, 0)), state_spec],
        out_specs=[spec(GLA_V), state_spec],
        out_shape=[jax.ShapeDtypeStruct((nb, pad, GLA_V), F32),
                   jax.ShapeDtypeStruct((nb, GLA_HEADS, GLA_DK, GLA_DV), F32)],
        compiler_params=_params("arbitrary"),
        name="gla_sample",
    )(q, k, v, r, la, gain, s0)


def _t5_bucket(dist):
    max_exact = NUM_BUCKETS // 2
    d_f = jnp.maximum(dist, 1).astype(F32)
    large = max_exact + (jnp.log(d_f / max_exact) / math.log(MAX_DISTANCE / max_exact)
                         * (NUM_BUCKETS - max_exact)).astype(jnp.int32)
    large = jnp.minimum(large, NUM_BUCKETS - 1)
    return jnp.where(dist < max_exact, dist, large)


def _group_bias(rel_bias):
    rows = []
    for g, (window, dil) in enumerate(DILATED_GROUPS):
        buckets = _t5_bucket(jnp.arange(window // dil + 1) * dil)
        rows.append(rel_bias[buckets][:, g * HEADS_PER_GROUP:(g + 1) * HEADS_PER_GROUP].T)
    return jnp.stack(rows)


def _prompt_bias_rows(gb):
    band = gb[:, :, ::-1]
    off = jnp.full(gb.shape[:2] + (2 * QUERY_BLOCK - KEYS_PER_QUERY,), NEG_INF, F32)
    return jnp.concatenate([band, off], axis=-1)[:, :, None, :]


def _sample_bias_tables(gb, steps, past_rows):
    tables, new_tables = [], []
    t_idx = np.arange(SUBLANES)[:, None]
    c_idx = np.arange(LANES)[None, :]
    for g, (window, dil) in enumerate(DILATED_GROUPS):
        p = past_rows[g]
        assert p == window == (KEYS_PER_QUERY - 1) * dil
        b = gb[g]
        heads = b.shape[0]
        row0 = b[:, :0:-1]
        if dil > 1:
            gaps = jnp.full((heads, KEYS_PER_QUERY - 1, dil - 1), NEG_INF, F32)
            row0 = jnp.concatenate([row0[:, :, None], gaps], axis=2).reshape(heads, p)
        rows = []
        for t in range(SUBLANES):
            if t < steps:
                rows.append(jnp.concatenate([jnp.full((heads, t), NEG_INF, F32), row0[:, :p - t]], axis=1))
            else:
                rows.append(jnp.full((heads, p), NEG_INF, F32))
        past = jnp.stack(rows, axis=1)
        new = jnp.full((heads, SUBLANES, LANES), NEG_INF, F32)
        for j in range((steps - 1) // dil + 1):
            mask = (t_idx - c_idx == j * dil) & (t_idx < steps) & (c_idx < steps)
            new = jnp.where(jnp.asarray(mask)[None], b[:, j][:, None, None], new)
        tables.append(past.reshape(heads // 2, 2 * SUBLANES, p))
        new_tables.append(new.reshape(heads // 2, 2 * SUBLANES, LANES))
    return tables, jnp.stack(new_tables)


def _ffn_weights(w_in, conv_w, conv_b, w_down):
    return (w_in.astype(BF16), w_down.astype(BF16), conv_w, conv_b.reshape(DEPTH, 1, D_FF))


def _conv_tail_prompt(cs):
    batch = cs.shape[0]
    tail = cs[:, :, SUBLANES - (CONV_WIDTH - 1):, :]
    return jnp.transpose(tail, (0, 2, 1, 3)).reshape(batch, CONV_WIDTH - 1, D_FF)


def _conv_past_sample(state):
    nb = state.shape[0]
    s = state.reshape(nb, CONV_WIDTH - 1, N_FF_CHUNKS, FF_CHUNK)
    return jnp.transpose(s, (2, 1, 0, 3)).reshape(N_FF_CHUNKS, (CONV_WIDTH - 1) * nb, FF_CHUNK)


def _conv_tail_sample(cs, nb):
    s = cs.reshape(N_FF_CHUNKS, CONV_WIDTH - 1, nb, FF_CHUNK)
    return jnp.transpose(s, (2, 1, 0, 3)).reshape(nb, CONV_WIDTH - 1, D_FF)


def kernel(x_prompt, x_sample, state_pool, cache_win_g1, cache_win_g2, cache_win_g3, state_gla, state_ffn_conv,
           c_prompt, c_sample, w_ada, b_ada, norm_gain, final_gain, rel_bias, pool_w, pool_scale,
           attn_w_in, attn_w_out, gla_w_in, gla_w_gate_up, gla_b_gate, gla_norm_gain, gla_w_out,
           ffn_w_in, ffn_conv_w, ffn_conv_b, ffn_w_down):
    batch, seq, _ = x_prompt.shape
    nb, steps, _ = x_sample.shape
    caches = (cache_win_g1, cache_win_g2, cache_win_g3)

    mod_p, mod_s = _modulation(c_prompt, c_sample, w_ada, b_ada)
    fgain = final_gain.reshape(1, D_MODEL)

    xp = x_prompt.reshape(batch * seq, D_MODEL)
    xs = jnp.transpose(x_sample, (1, 0, 2)).reshape(steps * nb, D_MODEL)

    pool_p, pool_s, gla_p, gla_s, conv_p, conv_s = [], [], [], [], [], []
    win_p, win_s = None, None
    ffn_stack = _ffn_weights(ffn_w_in, ffn_conv_w, ffn_conv_b, ffn_w_down)

    for i in range(DEPTH):
        kind, j = i % 3, i // 3
        last = i == DEPTH - 1
        ffn = (i, ffn_stack)
        conv_past = _conv_past_sample(state_ffn_conv[i])
        gains = norm_gain[i]
        if kind == 0:
            mix_w = (pool_w[j].astype(BF16), pool_scale[j].reshape(1, D_MODEL))
            xp, cs, pst = _layer_prompt("pool", last, xp, mod_p[i], gains, fgain, mix_w, ffn)
            pool_p.append(pst[:, POOL_CARRY_ROWS - POOL_STATE_ROWS:])
            past = jnp.transpose(state_pool[j], (1, 0, 2))
            xs, css, psts = _layer_sample("pool", last, xs, mod_s[i], gains, fgain, mix_w + (past,), ffn, conv_past)
            pool_s.append(jnp.transpose(psts, (1, 0, 2)))
        elif kind == 1:
            w = attn_w_in[j]
            w3 = jnp.stack([jnp.concatenate([w[:, s * ATTN_INNER + g * GROUP_WIDTH:
                                                s * ATTN_INNER + (g + 1) * GROUP_WIDTH] for s in range(3)], axis=1)
                            for g in range(N_GROUPS)]).astype(BF16)
            gb = _group_bias(rel_bias)
            wo = attn_w_out[j].astype(BF16)
            qkvp, *kv_t = _qkv_prompt(xp.reshape(batch, seq, D_MODEL), mod_p[i], gains, w3)
            o_all = _attn_prompt(qkvp, _prompt_bias_rows(gb))
            win_p = [jnp.transpose(t.reshape(batch, 2, HEADS_PER_GROUP, HEAD_DIM, t.shape[-1]), (0, 4, 1, 2, 3))[None]
                     for t in kv_t]
            xp, cs = _layer_prompt("proj", last, xp, mod_p[i], gains, fgain,
                                   (o_all.reshape(batch * seq, ATTN_INNER), wo), ffn)
            qkv_s = _proj_sample(xs, mod_s[i], gains, w.astype(BF16))
            q6 = qkv_s.reshape(steps, nb, 3, N_GROUPS, HEADS_PER_GROUP, HEAD_DIM)
            q_s = jnp.pad(jnp.transpose(q6[:, :, 0], (1, 2, 3, 0, 4)),
                          ((0, 0), (0, 0), (0, 0), (0, SUBLANES - steps), (0, 0)))
            q_s = q_s.reshape(nb, N_GROUPS, HEADS_PER_GROUP // 2, 2, SUBLANES, HEAD_DIM)
            zeros = jnp.zeros_like(q_s[:, :, :, 0])
            q_s = jnp.stack([jnp.concatenate([q_s[:, :, :, 0], zeros], axis=-1),
                             jnp.concatenate([zeros, q_s[:, :, :, 1]], axis=-1)], axis=3)
            q_s = q_s.reshape(nb, N_GROUPS, HEADS_PER_GROUP // 2, 2 * SUBLANES, LANES)
            kv_new = jnp.transpose(q6[:, :, 1:], (1, 3, 0, 2, 4, 5)).reshape(nb, N_GROUPS, steps, 2 * GROUP_WIDTH)
            kv_new = jnp.pad(kv_new, ((0, 0), (0, 0), (0, SUBLANES - steps), (0, 0)))
            past_rows = [c.shape[2] for c in caches]
            cache_t = [jnp.transpose(c[j], (0, 2, 3, 4, 1)).reshape(nb, 2 * HEADS_PER_GROUP, HEAD_DIM, p)
                       for c, p in zip(caches, past_rows)]
            bias_past, bias_new = _sample_bias_tables(gb, steps, past_rows)
            o8, *new_caches = _attn_sample(steps, q_s, kv_new, cache_t, bias_past, bias_new)
            a_s = jnp.transpose(o8[:, :steps], (1, 0, 2)).reshape(steps * nb, ATTN_INNER)
            win_s = [jnp.transpose(c.reshape(nb, 2, HEADS_PER_GROUP, HEAD_DIM, p), (0, 4, 1, 2, 3))[None]
                     for c, p in zip(new_caches, past_rows)]
            xs, css = _layer_sample("proj", last, xs, mod_s[i], gains, fgain, (a_s.astype(BF16), wo), ffn, conv_past)
        else:
            w = gla_w_in[j]
            n_main = 2 * GLA_QK + 2 * GLA_V
            weights = (w[:, :n_main].astype(BF16),
                       jnp.pad(w[:, n_main:], ((0, 0), (0, LANES - GATE_RANK))).astype(BF16),
                       jnp.pad(gla_w_gate_up[j], ((0, LANES - GATE_RANK), (0, 0))).astype(BF16),
                       gla_b_gate[j].reshape(1, GLA_QK))
            gain = gla_norm_gain[j].reshape(1, GLA_V)
            wo = gla_w_out[j].astype(BF16)
            q, k, v, r, la = _gla_proj(xp, mod_p[i], gains, weights, None, PROMPT_ROW_TILE)
            shp = lambda a: a.reshape(batch, seq, a.shape[-1])
            a_p, s_p = _gla_prompt(shp(q), shp(k), shp(v), shp(r), shp(la), gain)
            gla_p.append(s_p)
            xp, cs = _layer_prompt("proj", last, xp, mod_p[i], gains, fgain,
                                   (a_p.reshape(batch * seq, GLA_V), wo), ffn)
            outs = _gla_proj(xs, mod_s[i], gains, weights, nb, steps * nb)

            def per_seq(a):
                a = jnp.transpose(a.reshape(steps, nb, a.shape[-1]), (1, 0, 2))
                return jnp.pad(a, ((0, 0), (0, SAMPLE_DEC_PAD - steps), (0, 0)))

            qs, ks, vs, rs, las = (per_seq(a) for a in outs)
            a16, s_s = _gla_sample(qs, ks, vs, rs, las, gain, state_gla[j])
            gla_s.append(s_s)
            a_s = jnp.transpose(a16[:, :steps], (1, 0, 2)).reshape(steps * nb, GLA_V).astype(BF16)
            xs, css = _layer_sample("proj", last, xs, mod_s[i], gains, fgain, (a_s, wo), ffn, conv_past)
        conv_p.append(_conv_tail_prompt(cs))
        conv_s.append(_conv_tail_sample(css, nb))

    y_prompt = xp.reshape(batch, seq, D_MODEL)
    y_sample = jnp.transpose(xs.reshape(steps, nb, D_MODEL), (1, 0, 2))
    return (y_prompt, y_sample, jnp.stack(pool_p), jnp.stack(pool_s),
            win_p[0], win_s[0], win_p[1], win_s[1], win_p[2], win_s[2],
            jnp.stack(gla_p), jnp.stack(gla_s), jnp.stack(conv_p), jnp.stack(conv_s))
```

```python
import functools
import math

import numpy as np
import jax
import jax.numpy as jnp
from jax import lax
from jax.experimental import pallas as pl
from jax.experimental.pallas import tpu as pltpu

F32 = jnp.float32
BF16 = jnp.bfloat16

D_MODEL = 1024
DEPTH = 4
N_MOD = 6
EPS = 1e-6
NEG_INF = -1e30
POOL_WINDOWS = (2, 4, 8, 16)
POOL_GROUP_DIM = D_MODEL // len(POOL_WINDOWS)
POOL_STATE_ROWS = max(POOL_WINDOWS) - 1
POOL_CARRY_ROWS = 16
DILATED_GROUPS = ((128, 1), (512, 4), (2048, 16))
N_GROUPS = len(DILATED_GROUPS)
HEADS_PER_GROUP = 4
HEAD_DIM = 64
GROUP_WIDTH = HEADS_PER_GROUP * HEAD_DIM
ATTN_INNER = N_GROUPS * GROUP_WIDTH
KEYS_PER_QUERY = 129
QUERY_BLOCK = 128
NUM_BUCKETS = 32
MAX_DISTANCE = 2048
GLA_HEADS = 4
GLA_DK = 128
GLA_DV = 256
GLA_QK = GLA_HEADS * GLA_DK
GLA_V = GLA_HEADS * GLA_DV
GATE_RANK = 16
GATE_TAU = 16.0
GLA_CHUNK = 128
D_FF = 2816
CONV_WIDTH = 3

LANES = 128
SUBLANES = 8
FF_CHUNK = 256
CHEAP_STRIDE = 4
N_FF_CHUNKS = D_FF // FF_CHUNK
VMEM_LIMIT_BYTES = 56 * 1024 * 1024
PROMPT_ROW_TILE = 1024
GLA_TIME_TILE = 512
ATTN_BLOCK_UNROLL = 16
GLA_CHUNK_UNROLL = 4
GLA_SAMPLE_UNROLL = 8
ATTN_SAMPLE_SEQS = 2
SAMPLE_DEC_PAD = 16


def _params(*semantics):
    return pltpu.CompilerParams(dimension_semantics=semantics, vmem_limit_bytes=VMEM_LIMIT_BYTES)


def _resident(shape):
    nd = len(shape)
    return pl.BlockSpec(shape, lambda *_: (0,) * nd, pipeline_mode=pl.Buffered(1))


def _dot(a, b):
    return jnp.dot(a, b, preferred_element_type=F32)


def _dot_nt(a, b):
    return lax.dot_general(a, b, (((1,), (1,)), ((), ())), preferred_element_type=F32)


def _dot_tn(a, b):
    return lax.dot_general(a, b, (((0,), (0,)), ((), ())), preferred_element_type=F32)


def _rms(x):
    return x * lax.rsqrt(jnp.mean(x * x, axis=-1, keepdims=True) + EPS)


def _bcast_rows(v, y, nb):
    if nb is None:
        return v * y
    rows, width = y.shape
    return (y.reshape(rows // nb, nb, width) * v[None]).reshape(rows, width)


def _norm_mod(x, gain, shift, scale, nb):
    y = _rms(x) * gain
    if nb is None:
        return y * (1.0 + scale) + shift
    rows, width = y.shape
    y3 = y.reshape(rows // nb, nb, width)
    return (y3 * (1.0 + scale)[None] + shift[None]).reshape(rows, width)


def _prompt_mod(mod_ref, b):
    return [mod_ref[k, pl.ds(b, 1), :] for k in range(N_MOD)]


def _gelu(x):
    return 0.5 * x * (1.0 + lax.erf(x * (1.0 / math.sqrt(2.0))))


def _silu(x):
    return x * jax.nn.sigmoid(x)


def _split_bf16(a):
    hi = a.astype(BF16)
    lo = (a - hi.astype(F32)).astype(BF16)
    return hi, lo


def _mod_kernel(cp_ref, cs_ref, w_ref, b_ref, op_ref, os_ref):
    batch = op_ref.shape[2]
    pad = cp_ref.shape[0]
    rows = pad + cs_ref.shape[0]
    w_hi, w_lo = _split_bf16(w_ref[0])
    a_hi, a_lo = _split_bf16(_silu(jnp.concatenate([cp_ref[...], cs_ref[...]], axis=0)))
    both = _dot(jnp.concatenate([a_hi, a_lo], axis=0), w_hi)
    out = both[0:rows] + both[rows:2 * rows] + _dot(a_hi, w_lo) + b_ref[0]
    op_ref[0, 0] = out[0:batch]
    os_ref[0, 0] = out[pad:rows]


def _modulation(c_prompt, c_sample, w_ada, b_ada):
    batch, nb = c_prompt.shape[0], c_sample.shape[0]
    bf16_rows = 2 * SUBLANES
    c_prompt = jnp.pad(c_prompt, ((0, -batch % bf16_rows), (0, 0)))
    return pl.pallas_call(
        _mod_kernel,
        grid=(DEPTH, N_MOD),
        in_specs=[pl.BlockSpec(c_prompt.shape, lambda l, m: (0, 0)),
                  pl.BlockSpec((nb, D_MODEL), lambda l, m: (0, 0)),
                  pl.BlockSpec((1, D_MODEL, D_MODEL), lambda l, m: (l, 0, m)),
                  pl.BlockSpec((1, 1, D_MODEL), lambda l, m: (l, 0, m))],
        out_specs=[pl.BlockSpec((1, 1, batch, D_MODEL), lambda l, m: (l, m, 0, 0)),
                   pl.BlockSpec((1, 1, nb, D_MODEL), lambda l, m: (l, m, 0, 0))],
        out_shape=[jax.ShapeDtypeStruct((DEPTH, N_MOD, batch, D_MODEL), F32),
                   jax.ShapeDtypeStruct((DEPTH, N_MOD, nb, D_MODEL), F32)],
        compiler_params=_params("parallel", "parallel"),
        name="adaln_mod",
    )(c_prompt, c_sample, w_ada, b_ada.reshape(DEPTH, 1, N_MOD * D_MODEL))


def _ffn_chunk_math(g, g_m1, g_m2, u, cw, cb):
    gc = cw[2:3] * g + cw[1:2] * g_m1 + cw[0:1] * g_m2 + cb
    return (_gelu(gc) * u).astype(BF16)


def _ffn_chunk_weights(win_ref, wd_ref, cw_ref, cb_ref, j):
    cols = slice(j * FF_CHUNK, (j + 1) * FF_CHUNK)
    ucols = slice(D_FF + j * FF_CHUNK, D_FF + (j + 1) * FF_CHUNK)
    return win_ref[0, :, cols], win_ref[0, :, ucols], wd_ref[0, cols, :], cw_ref[0, :, cols], cb_ref[0, :, cols]


def _layer_resident(array, layer):
    nd = array.ndim
    return pl.BlockSpec((1,) + array.shape[1:], lambda *_: (layer,) + (0,) * (nd - 1),
                        pipeline_mode=pl.Buffered(1))


def _layer_prompt_kernel(mixer, last, tm, tiles_per_seq, *refs):
    refs = list(refs)
    x_ref, mod_ref, gains_ref, fg_ref = refs[:4]
    refs = refs[4:]
    if mixer == "pool":
        pw_ref, ps_ref = refs[:2]
    else:
        a_ref, wp_ref = refs[:2]
    win_ref, wd_ref, cw_ref, cb_ref = refs[2:6]
    refs = refs[6:]
    if mixer == "pool":
        y_ref, cs_ref, pst_ref, h2_ref, act_ref, gext_ref, cc_ref, hext_ref = refs
    else:
        y_ref, cs_ref, h2_ref, act_ref, gext_ref, cc_ref = refs

    i = pl.program_id(0)
    tile_in_seq = i % tiles_per_seq

    @pl.when(tile_in_seq == 0)
    def _():
        cc_ref[...] = jnp.zeros_like(cc_ref)
        if mixer == "pool":
            hext_ref[0:POOL_CARRY_ROWS, :] = jnp.zeros((POOL_CARRY_ROWS, D_MODEL), F32)

    x = x_ref[...]
    m = _prompt_mod(mod_ref, i // tiles_per_seq)
    gains = gains_ref[...]

    if mixer == "pool":
        h = _norm_mod(x, gains[0:1], m[0], m[1], None)
        hext_ref[POOL_CARRY_ROWS:, :] = h
        pos = tile_in_seq * tm + lax.broadcasted_iota(jnp.int32, (tm, 1), 0)
        parts = []
        for g, w in enumerate(POOL_WINDOWS):
            cols = slice(g * POOL_GROUP_DIM, (g + 1) * POOL_GROUP_DIM)
            s = hext_ref[:, cols]
            span = 1
            while span < w:
                s = s + pltpu.roll(s, span, 0)
                span *= 2
            inv_count = 1.0 / jnp.minimum(pos + 1, w).astype(F32)
            d = s[POOL_CARRY_ROWS:, :] * inv_count - h[:, cols]
            parts.append(_dot(d.astype(BF16), pw_ref[g]))
        mix = jnp.concatenate(parts, axis=-1) * ps_ref[...]
        tail = hext_ref[pl.ds(tm, POOL_CARRY_ROWS), :]
        hext_ref[0:POOL_CARRY_ROWS, :] = tail
        pst_ref[0] = tail
    else:
        mix = _dot(a_ref[...], wp_ref[...])

    x1 = x + m[2] * mix
    y_ref[...] = x1
    h2_ref[...] = _norm_mod(x1, gains[1:2], m[3], m[4], None).astype(BF16)

    def up_proj(j):
        wg, wu, _, _, _ = _ffn_chunk_weights(win_ref, wd_ref, cw_ref, cb_ref, j)
        h2 = h2_ref[...]
        return _dot(h2, wg), _dot(h2, wu)

    ahead = up_proj(0)
    for j in range(N_FF_CHUNKS):
        g, u = ahead
        if j + 1 < N_FF_CHUNKS:
            ahead = up_proj(j + 1)
        _, _, _, cw, cb = _ffn_chunk_weights(win_ref, wd_ref, cw_ref, cb_ref, j)
        gx = gext_ref.at[j % 2]
        gx[0:SUBLANES, :] = cc_ref[j]
        gx[SUBLANES:, :] = g
        act_ref[:, j * FF_CHUNK:(j + 1) * FF_CHUNK] = _ffn_chunk_math(
            g, gx[pl.ds(SUBLANES - 1, tm), :], gx[pl.ds(SUBLANES - 2, tm), :], u, cw, cb)
        tail = g[tm - SUBLANES:tm, :]
        cc_ref[j] = tail
        cs_ref[0, j] = tail
    xo = y_ref[...] + m[5] * _dot(act_ref[...], wd_ref[0])
    if last:
        xo = _rms(xo) * fg_ref[...]
    y_ref[...] = xo


def _layer_prompt(mixer, last, x, mod, gains, fgain, mix_args, ffn):
    n = x.shape[0]
    batch = mod.shape[1]
    seq = n // batch
    tm = PROMPT_ROW_TILE
    tps = seq // tm
    in_specs = [pl.BlockSpec((tm, D_MODEL), lambda i: (i, 0)),
                _resident(mod.shape), _resident((2, D_MODEL)), _resident((1, D_MODEL))]
    if mixer == "pool":
        pw, ps = mix_args
        in_specs += [_resident(pw.shape), _resident(ps.shape)]
    else:
        a, wp = mix_args
        in_specs += [pl.BlockSpec((tm, a.shape[1]), lambda i: (i, 0)), _resident(wp.shape)]
    layer, ffn_w = ffn
    in_specs += [_layer_resident(w, layer) for w in ffn_w]
    out_shape = [jax.ShapeDtypeStruct((n, D_MODEL), F32),
                 jax.ShapeDtypeStruct((batch, N_FF_CHUNKS, SUBLANES, FF_CHUNK), F32)]
    out_specs = [pl.BlockSpec((tm, D_MODEL), lambda i: (i, 0)),
                 pl.BlockSpec((1, N_FF_CHUNKS, SUBLANES, FF_CHUNK), lambda i: (i // tps, 0, 0, 0))]
    scratch = [pltpu.VMEM((tm, D_MODEL), BF16), pltpu.VMEM((tm, D_FF), BF16),
               pltpu.VMEM((2, tm + SUBLANES, FF_CHUNK), F32), pltpu.VMEM((N_FF_CHUNKS, SUBLANES, FF_CHUNK), F32)]
    if mixer == "pool":
        out_shape.append(jax.ShapeDtypeStruct((batch, POOL_CARRY_ROWS, D_MODEL), F32))
        out_specs.append(pl.BlockSpec((1, POOL_CARRY_ROWS, D_MODEL), lambda i: (i // tps, 0, 0)))
        scratch.append(pltpu.VMEM((tm + POOL_CARRY_ROWS, D_MODEL), F32))
    return pl.pallas_call(
        functools.partial(_layer_prompt_kernel, mixer, last, tm, tps),
        grid=(n // tm,), in_specs=in_specs, out_specs=out_specs, out_shape=out_shape,
        scratch_shapes=scratch, compiler_params=_params("arbitrary"),
        name=f"layer_prompt_{mixer}",
    )(x, mod, gains, fgain, *mix_args, *ffn_w)


def _layer_sample_kernel(mixer, last, nb, steps, *refs):
    refs = list(refs)
    x_ref, mod_ref, gains_ref, fg_ref = refs[:4]
    refs = refs[4:]
    if mixer == "pool":
        pw_ref, ps_ref, ppast_ref = refs[:3]
        refs = refs[3:]
    else:
        a_ref, wp_ref = refs[:2]
        refs = refs[2:]
    wg_ref, wu_ref, wd_ref, cw_ref, cb_ref, cpast_ref = refs[:6]
    refs = refs[6:]
    if mixer == "pool":
        y_ref, cs_ref, pst_ref, h2_ref, acc_ref = refs
    else:
        y_ref, cs_ref, h2_ref, acc_ref = refs
    rows = steps * nb
    j = pl.program_id(0)

    @pl.when(j == 0)
    def _():
        x = x_ref[...]
        gains = gains_ref[...]
        if mixer == "pool":
            h = _norm_mod(x, gains[0:1], mod_ref[0], mod_ref[1], nb)
            new = [h[t * nb:(t + 1) * nb, :] for t in range(steps)]

            def u_rows(p, cols):
                if p < POOL_STATE_ROWS:
                    return ppast_ref[p, :, cols]
                return new[p - POOL_STATE_ROWS][:, cols]

            parts = []
            for g, w in enumerate(POOL_WINDOWS):
                cols = slice(g * POOL_GROUP_DIM, (g + 1) * POOL_GROUP_DIM)
                ds = []
                for t in range(steps):
                    s = u_rows(POOL_STATE_ROWS + t, cols)
                    for k in range(1, w):
                        s = s + u_rows(POOL_STATE_ROWS + t - k, cols)
                    ds.append(s * (1.0 / w) - new[t][:, cols])
                parts.append(_dot(jnp.concatenate(ds, axis=0).astype(BF16), pw_ref[g]))
            mix = jnp.concatenate(parts, axis=-1) * ps_ref[...]
            full = slice(0, D_MODEL)
            for p in range(POOL_STATE_ROWS):
                pst_ref[p] = u_rows(p + steps, full)
        else:
            mix = _dot(a_ref[...], wp_ref[...])
        x1 = x + _bcast_rows(mod_ref[2], mix, nb)
        y_ref[...] = x1
        h2_ref[...] = _norm_mod(x1, gains[1:2], mod_ref[3], mod_ref[4], nb).astype(BF16)
        acc_ref[...] = jnp.zeros_like(acc_ref)

    past_rows = (CONV_WIDTH - 1) * nb
    h2 = h2_ref[...]
    g = _dot(h2, wg_ref[0])
    u = _dot(h2, wu_ref[0])
    gall = jnp.concatenate([cpast_ref[0], g], axis=0)
    a = _ffn_chunk_math(g, gall[nb:nb + rows, :], gall[0:rows, :], u, cw_ref[0], cb_ref[0])
    acc_ref[...] += _dot(a, wd_ref[0])
    cs_ref[0] = gall[rows:rows + past_rows, :]

    @pl.when(j == pl.num_programs(0) - 1)
    def _():
        xo = y_ref[...] + _bcast_rows(mod_ref[5], acc_ref[...], nb)
        if last:
            xo = _rms(xo) * fg_ref[...]
        y_ref[...] = xo


def _layer_sample(mixer, last, x, mod, gains, fgain, mix_args, ffn, conv_past):
    rows = x.shape[0]
    nb = mod.shape[1]
    steps = rows // nb
    layer, (w_in, w_down, conv_w, conv_b) = ffn
    head = [x, mod, gains, fgain, *mix_args]
    past_block = (1,) + conv_past.shape[1:]
    chunk_specs = [pl.BlockSpec((1, D_MODEL, FF_CHUNK), lambda j: (layer, 0, j)),
                   pl.BlockSpec((1, D_MODEL, FF_CHUNK), lambda j: (layer, 0, N_FF_CHUNKS + j)),
                   pl.BlockSpec((1, FF_CHUNK, D_MODEL), lambda j: (layer, j, 0)),
                   pl.BlockSpec((1, CONV_WIDTH, FF_CHUNK), lambda j: (layer, 0, j)),
                   pl.BlockSpec((1, 1, FF_CHUNK), lambda j: (layer, 0, j)),
                   pl.BlockSpec(past_block, lambda j: (j, 0, 0))]
    out_shape = [jax.ShapeDtypeStruct((rows, D_MODEL), F32),
                 jax.ShapeDtypeStruct(conv_past.shape, F32)]
    out_specs = [pl.BlockSpec((rows, D_MODEL), lambda j: (0, 0)), pl.BlockSpec(past_block, lambda j: (j, 0, 0))]
    if mixer == "pool":
        out_shape.append(jax.ShapeDtypeStruct((POOL_STATE_ROWS, nb, D_MODEL), F32))
        out_specs.append(pl.BlockSpec((POOL_STATE_ROWS, nb, D_MODEL), lambda j: (0, 0, 0)))
    return pl.pallas_call(
        functools.partial(_layer_sample_kernel, mixer, last, nb, steps),
        grid=(N_FF_CHUNKS,),
        in_specs=[_resident(a.shape) for a in head] + chunk_specs,
        out_specs=out_specs,
        out_shape=out_shape,
        scratch_shapes=[pltpu.VMEM((rows, D_MODEL), BF16), pltpu.VMEM((rows, D_MODEL), F32)],
        compiler_params=_params("arbitrary"),
        name=f"layer_sample_{mixer}",
    )(*head, w_in, w_in, w_down, conv_w, conv_b, conv_past)


def _qkv_prompt_kernel(seq, x_ref, mod_ref, gains_ref, w_ref, qkvp_ref, kt1_ref, kt2_ref, kt3_ref, h_ref, slab_ref, tmp_ref):
    g = pl.program_id(1)
    kt_refs = (kt1_ref, kt2_ref, kt3_ref)

    @pl.when(g == 0)
    def _():
        m = _prompt_mod(mod_ref, pl.program_id(0))
        h_ref[...] = _norm_mod(x_ref[0], gains_ref[0:1, :], m[0], m[1], None).astype(BF16)

    h = h_ref[...]
    for c in range(3):
        r = _dot(h, w_ref[0, :, c * GROUP_WIDTH:(c + 1) * GROUP_WIDTH])
        if c == 0:
            r = r * (HEAD_DIM ** -0.5)
        slab_ref[2 * c] = r[:, 0:LANES]
        slab_ref[2 * c + 1] = r[:, LANES:2 * LANES]

    qkvp_ref[0, 0, 0:QUERY_BLOCK, :] = jnp.zeros((QUERY_BLOCK, 3 * GROUP_WIDTH), BF16)
    for gi, (_, dil) in enumerate(DILATED_GROUPS):
        @pl.when(g == gi)
        def _(dil=dil):
            per_class = seq // dil
            assert dil in (1, CHEAP_STRIDE, CHEAP_STRIDE ** 2)
            for s in range(6):
                if dil > CHEAP_STRIDE:
                    part = seq // CHEAP_STRIDE
                    for r in range(CHEAP_STRIDE):
                        tmp_ref[r * part:(r + 1) * part, :] = slab_ref[s, pl.ds(r, part, stride=CHEAP_STRIDE), :]
                for rho in range(dil):
                    if dil == 1:
                        v = slab_ref[s]
                    elif dil == CHEAP_STRIDE:
                        v = slab_ref[s, pl.ds(rho, per_class, stride=dil), :]
                    else:
                        r, q = rho % CHEAP_STRIDE, rho // CHEAP_STRIDE
                        v = tmp_ref[pl.ds(r * part + q, per_class, stride=dil // CHEAP_STRIDE), :]
                    r0 = QUERY_BLOCK + rho * per_class
                    qkvp_ref[0, 0, r0:r0 + per_class, s * LANES:(s + 1) * LANES] = v.astype(BF16)

    for gi, kt_ref in enumerate(kt_refs):
        @pl.when(g == gi)
        def _(kt_ref=kt_ref):
            keep = kt_ref.shape[-1]
            for s in range(2, 6):
                tile_t = slab_ref[s, seq - keep:seq, :].T
                for r in range(2):
                    kt_ref[0, 2 * (s - 2) + r] = tile_t[r * HEAD_DIM:(r + 1) * HEAD_DIM, :]


def _qkv_prompt(x3, mod, gains, w3):
    batch, seq, _ = x3.shape
    width = 3 * GROUP_WIDTH
    keeps = [min(window, seq) for window, _ in DILATED_GROUPS]
    return pl.pallas_call(
        functools.partial(_qkv_prompt_kernel, seq),
        grid=(batch, N_GROUPS),
        in_specs=[pl.BlockSpec((1, seq, D_MODEL), lambda b, g: (b, 0, 0)),
                  pl.BlockSpec(mod.shape, lambda b, g: (0, 0, 0)),
                  pl.BlockSpec((2, D_MODEL), lambda b, g: (0, 0)),
                  pl.BlockSpec((1, D_MODEL, width), lambda b, g: (g, 0, 0))],
        out_specs=[pl.BlockSpec((1, 1, seq + QUERY_BLOCK, width), lambda b, g: (b, g, 0, 0))]
                  + [pl.BlockSpec((1, 2 * HEADS_PER_GROUP, HEAD_DIM, keep), lambda b, g: (b, 0, 0, 0)) for keep in keeps],
        out_shape=[jax.ShapeDtypeStruct((batch, N_GROUPS, seq + QUERY_BLOCK, width), BF16)]
                  + [jax.ShapeDtypeStruct((batch, 2 * HEADS_PER_GROUP, HEAD_DIM, keep), F32) for keep in keeps],
        scratch_shapes=[pltpu.VMEM((seq, D_MODEL), BF16), pltpu.VMEM((6, seq, LANES), F32),
                        pltpu.VMEM((seq, LANES), F32)],
        compiler_params=_params("arbitrary", "arbitrary"),
        name="qkv_prompt",
    )(x3, mod, gains, w3)


def _head_lane_mask(rows, h):
    lane = lax.broadcasted_iota(jnp.int32, (rows, GROUP_WIDTH), 1)
    return (lane >= h * HEAD_DIM) & (lane < (h + 1) * HEAD_DIM)


def _attn_block(q, k, v, bias_ref, cols):
    masks = [_head_lane_mask(QUERY_BLOCK, h) for h in range(HEADS_PER_GROUP)]
    qs = jnp.concatenate([jnp.where(hm, q, jnp.zeros_like(q)) for hm in masks], axis=0)
    s = _dot_nt(qs, k) + bias_ref[:, cols]
    m = jnp.max(s, axis=-1, keepdims=True)
    p = jnp.exp(s - m)
    l = jnp.sum(p, axis=-1, keepdims=True)
    pv = _dot(p.astype(BF16), v) * (1.0 / l)
    lse_rows = jnp.broadcast_to(m + jnp.log(l), pv.shape)
    o = pv[0:QUERY_BLOCK]
    lse = lse_rows[0:QUERY_BLOCK]
    for h in range(1, HEADS_PER_GROUP):
        rows = slice(h * QUERY_BLOCK, (h + 1) * QUERY_BLOCK)
        o = jnp.where(masks[h], pv[rows], o)
        lse = jnp.where(masks[h], lse_rows[rows], lse)
    return o, lse


def _attn_prompt_kernel(seq, qkv_ref, brow_ref, o_ref, os_ref, ls_ref, bias_ref):
    @pl.when(pl.program_id(0) == 0)
    def _():
        left = lax.broadcasted_iota(jnp.int32, (QUERY_BLOCK, 2 * QUERY_BLOCK), 1) < QUERY_BLOCK
        for g in range(N_GROUPS):
            for h in range(HEADS_PER_GROUP):
                base = jnp.broadcast_to(brow_ref[g, h], (QUERY_BLOCK, 2 * QUERY_BLOCK))
                band = pltpu.roll(base, 0, 1, stride=1, stride_axis=0)
                rows = slice(h * QUERY_BLOCK, (h + 1) * QUERY_BLOCK)
                bias_ref[g, 0, rows, :] = band
                bias_ref[g, 1, rows, :] = jnp.where(left, NEG_INF, band)

    qc = slice(0, GROUP_WIDTH)
    kc = slice(GROUP_WIDTH, 2 * GROUP_WIDTH)
    vc = slice(2 * GROUP_WIDTH, 3 * GROUP_WIDTH)
    n_blocks = seq // QUERY_BLOCK
    for g, (_, dil) in enumerate(DILATED_GROUPS):
        blocks_per_class = n_blocks // dil

        def do_block(blk, carry, g=g, dil=dil, blocks_per_class=blocks_per_class):
            r0 = pl.multiple_of(blk * QUERY_BLOCK, QUERY_BLOCK)
            cur = pl.ds(r0 + QUERY_BLOCK, QUERY_BLOCK)
            q = qkv_ref[0, g, cur, qc]
            rho = blk // blocks_per_class
            in_class = blk % blocks_per_class
            if blocks_per_class == 1:
                o, lse = _attn_block(q, qkv_ref[0, g, cur, kc], qkv_ref[0, g, cur, vc],
                                     bias_ref.at[g, 0], slice(QUERY_BLOCK, 2 * QUERY_BLOCK))
            else:
                both = pl.ds(r0, 2 * QUERY_BLOCK)
                first = jnp.asarray(in_class == 0, jnp.int32)
                o, lse = _attn_block(q, qkv_ref[0, g, both, kc], qkv_ref[0, g, both, vc],
                                     bias_ref.at[g, first], slice(0, 2 * QUERY_BLOCK))
            start = in_class * (QUERY_BLOCK * dil) + rho
            for s in range(2):
                cols = slice(s * LANES, (s + 1) * LANES)
                if dil == 1:
                    os_ref[g, s, pl.ds(r0, QUERY_BLOCK), :] = o[:, cols]
                    ls_ref[g, s, pl.ds(r0, QUERY_BLOCK), :] = lse[:, cols]
                else:
                    os_ref[g, s, pl.ds(start, QUERY_BLOCK, stride=dil), :] = o[:, cols]
                    ls_ref[g, s, pl.ds(start, QUERY_BLOCK, stride=dil), :] = lse[:, cols]
            return carry

        lax.fori_loop(0, n_blocks, do_block, 0, unroll=ATTN_BLOCK_UNROLL)

    def merge(i, carry):
        r0 = pl.multiple_of(i * QUERY_BLOCK, QUERY_BLOCK)
        rows = pl.ds(r0, QUERY_BLOCK)
        for s in range(2):
            ls = [ls_ref[g, s, rows, :] for g in range(N_GROUPS)]
            mx = jnp.maximum(jnp.maximum(ls[0], ls[1]), ls[2])
            es = [jnp.exp(l - mx) for l in ls]
            inv = 1.0 / (es[0] + es[1] + es[2])
            for g in range(N_GROUPS):
                c0 = g * GROUP_WIDTH + s * LANES
                o_ref[0, rows, c0:c0 + LANES] = (os_ref[g, s, rows, :] * (es[g] * inv)).astype(BF16)
        return carry

    lax.fori_loop(0, seq // QUERY_BLOCK, merge, 0)


def _attn_prompt(qkvp, bias_rows):
    batch, _, padded, width = qkvp.shape
    seq = padded - QUERY_BLOCK
    return pl.pallas_call(
        functools.partial(_attn_prompt_kernel, seq),
        grid=(batch,),
        in_specs=[pl.BlockSpec((1, N_GROUPS, padded, width), lambda b: (b, 0, 0, 0)),
                  _resident(bias_rows.shape)],
        out_specs=pl.BlockSpec((1, seq, ATTN_INNER), lambda b: (b, 0, 0)),
        out_shape=jax.ShapeDtypeStruct((batch, seq, ATTN_INNER), BF16),
        scratch_shapes=[pltpu.VMEM((N_GROUPS, 2, seq, LANES), F32), pltpu.VMEM((N_GROUPS, 2, seq, LANES), F32),
                        pltpu.VMEM((N_GROUPS, 2, HEADS_PER_GROUP * QUERY_BLOCK, 2 * QUERY_BLOCK), F32)],
        compiler_params=_params("arbitrary"),
        name="attn_prompt",
    )(qkvp, bias_rows)


def _proj_sample_kernel(nb, x_ref, mod_ref, gains_ref, w_ref, o_ref):
    h = _norm_mod(x_ref[...], gains_ref[0:1, :], mod_ref[0], mod_ref[1], nb).astype(BF16)
    o_ref[...] = _dot(h, w_ref[...])


def _proj_sample(x, mod, gains, w):
    rows = x.shape[0]
    nb = mod.shape[1]
    args = [x, mod, gains, w]
    return pl.pallas_call(
        functools.partial(_proj_sample_kernel, nb),
        grid=(1,),
        in_specs=[_resident(a.shape) for a in args],
        out_specs=pl.BlockSpec((rows, w.shape[1]), lambda i: (0, 0)),
        out_shape=jax.ShapeDtypeStruct((rows, w.shape[1]), F32),
        compiler_params=_params("arbitrary"),
        name="qkv_sample",
    )(*args)


def _split3_bf16(a):
    hi = a.astype(BF16)
    r1 = a - hi.astype(F32)
    mid = r1.astype(BF16)
    lo = (r1 - mid.astype(F32)).astype(BF16)
    return hi, mid, lo


def _attn_sample_kernel(steps, q_ref, kvn_ref, c1_ref, c2_ref, c3_ref, t1_ref, t2_ref, t3_ref, bn_ref,
                        o_ref, n1_ref, n2_ref, n3_ref):
    for i in range(q_ref.shape[0]):
        _attn_sample_one(steps, i, q_ref, kvn_ref, ((c1_ref, t1_ref, n1_ref), (c2_ref, t2_ref, n2_ref),
                                                  (c3_ref, t3_ref, n3_ref)), bn_ref, o_ref)


def _attn_sample_one(steps, i, q_ref, kvn_ref, groups, bn_ref, o_ref):
    pairs = HEADS_PER_GROUP // 2
    outs, lses = [], []
    lane = lax.broadcasted_iota(jnp.int32, (SUBLANES, LANES), 1)
    row = lax.broadcasted_iota(jnp.int32, (SUBLANES, LANES), 0)
    own_lanes = lane < HEAD_DIM
    sel_head = jnp.where((lane == row) & (row < steps), 1.0, 0.0).astype(BF16)
    sel_tail = jnp.where((lane == row + (LANES - steps)) & (row < steps), 1.0, 0.0).astype(BF16)
    for g, (c_ref, t_ref, n_ref) in enumerate(groups):
        p_rows = c_ref.shape[-1]
        pieces = _split3_bf16(kvn_ref[i, g])
        new_head = sum(_dot_tn(x, sel_head) for x in pieces)
        new_tail = sum(_dot_tn(x, sel_tail) for x in pieces)
        o_pairs, l_pairs = [], []
        for j in range(pairs):
            q = (q_ref[i, g, j] * (HEAD_DIM ** -0.5)).astype(BF16)
            kt = c_ref[i, 2 * j:2 * j + 2].reshape(LANES, p_rows).astype(BF16)
            vt = c_ref[i, HEADS_PER_GROUP + 2 * j:HEADS_PER_GROUP + 2 * j + 2].reshape(LANES, p_rows).astype(BF16)
            k_new = new_head[j * LANES:(j + 1) * LANES].astype(BF16)
            v_new = new_head[(pairs + j) * LANES:(pairs + j + 1) * LANES].astype(BF16)
            s = _dot(q, kt) + t_ref[j]
            s_new = _dot(q, k_new) + bn_ref[g, j]
            m = jnp.maximum(jnp.max(s, axis=-1, keepdims=True), jnp.max(s_new, axis=-1, keepdims=True))
            p = jnp.exp(s - m)
            p_new = jnp.exp(s_new - m)
            l = jnp.sum(p, axis=-1, keepdims=True) + jnp.sum(p_new, axis=-1, keepdims=True)
            o = _dot_nt(p.astype(BF16), vt) + _dot_nt(p_new.astype(BF16), v_new)
            o_pairs.append(o * (1.0 / l))
            l_pairs.append(m + jnp.log(l))
            for idx in (2 * j, 2 * j + 1, HEADS_PER_GROUP + 2 * j, HEADS_PER_GROUP + 2 * j + 1):
                n_ref[i, idx] = pltpu.roll(c_ref[i, idx], p_rows - steps, 1)
                n_ref[i, idx, :, p_rows - steps:p_rows] = new_tail[idx * HEAD_DIM:(idx + 1) * HEAD_DIM, LANES - steps:LANES]
        outs.append(o_pairs)
        lses.append(l_pairs)
    for j in range(pairs):
        ls = [lses[g][j] for g in range(N_GROUPS)]
        mx = jnp.maximum(jnp.maximum(ls[0], ls[1]), ls[2])
        es = [jnp.exp(l - mx) for l in ls]
        inv = 1.0 / (es[0] + es[1] + es[2])
        for g in range(N_GROUPS):
            og = outs[g][j] * (es[g] * inv)
            c0 = g * GROUP_WIDTH + j * LANES
            o_ref[i, :, c0:c0 + LANES] = jnp.where(own_lanes, og[0:SUBLANES], og[SUBLANES:2 * SUBLANES])


def _attn_sample(steps, q, kv_new, caches, tables, bias_new):
    nb = q.shape[0]
    sb = math.gcd(nb, ATTN_SAMPLE_SEQS)
    in_specs = [pl.BlockSpec((sb,) + q.shape[1:], lambda b: (b, 0, 0, 0, 0)),
                pl.BlockSpec((sb,) + kv_new.shape[1:], lambda b: (b, 0, 0, 0))]
    cache_specs = [pl.BlockSpec((sb,) + c.shape[1:], lambda b: (b, 0, 0, 0)) for c in caches]
    in_specs += cache_specs + [_resident(t.shape) for t in tables] + [_resident(bias_new.shape)]
    return pl.pallas_call(
        functools.partial(_attn_sample_kernel, steps),
        grid=(nb // sb,), in_specs=in_specs,
        out_specs=[pl.BlockSpec((sb, SUBLANES, ATTN_INNER), lambda b: (b, 0, 0))] + cache_specs,
        out_shape=[jax.ShapeDtypeStruct((nb, SUBLANES, ATTN_INNER), F32)]
                  + [jax.ShapeDtypeStruct(c.shape, F32) for c in caches],
        compiler_params=_params("arbitrary"),
        name="attn_sample",
    )(q, kv_new, *caches, *tables, bias_new)


def _gla_proj_kernel(nb, tiles_per_seq, x_ref, mod_ref, gains_ref, w_ref, wgd_ref, wgu_ref, bg_ref,
                     q_ref, k_ref, v_ref, r_ref, la_ref):
    if nb is None:
        m = _prompt_mod(mod_ref, pl.program_id(0) // tiles_per_seq)
        shift, scale = m[0], m[1]
    else:
        shift, scale = mod_ref[0], mod_ref[1]
    h = _norm_mod(x_ref[...], gains_ref[0:1, :], shift, scale, nb).astype(BF16)
    q_ref[...] = _dot(h, w_ref[:, 0:GLA_QK]) * (GLA_DK ** -0.5)
    k_ref[...] = _dot(h, w_ref[:, GLA_QK:2 * GLA_QK])
    v_ref[...] = _dot(h, w_ref[:, 2 * GLA_QK:2 * GLA_QK + GLA_V]).astype(BF16)
    r_ref[...] = _dot(h, w_ref[:, 2 * GLA_QK + GLA_V:2 * GLA_QK + 2 * GLA_V])
    gd = _dot(h, wgd_ref[...])
    gate = _dot(gd.astype(BF16), wgu_ref[...]) + bg_ref[...]
    la_ref[...] = jax.nn.log_sigmoid(gate) * (1.0 / GATE_TAU)


def _gla_proj(x, mod, gains, weights, nb, tm):
    rows = x.shape[0]
    tps = None if nb is not None else rows // mod.shape[1] // tm
    widths = (GLA_QK, GLA_QK, GLA_V, GLA_V, GLA_QK)
    dtypes = (F32, F32, BF16, F32, F32)
    return pl.pallas_call(
        functools.partial(_gla_proj_kernel, nb, tps),
        grid=(rows // tm,),
        in_specs=[pl.BlockSpec((tm, D_MODEL), lambda i: (i, 0)), _resident(mod.shape), _resident((2, D_MODEL))]
                 + [_resident(w.shape) for w in weights],
        out_specs=[pl.BlockSpec((tm, w), lambda i: (i, 0)) for w in widths],
        out_shape=[jax.ShapeDtypeStruct((rows, w), dt) for w, dt in zip(widths, dtypes)],
        compiler_params=_params("arbitrary"),
        name="gla_proj",
    )(x, mod, gains, *weights)


def _cumsum_rows(g):
    rows = g.shape[0]
    row = lax.broadcasted_iota(jnp.int32, g.shape, 0)
    b = g
    shift = 1
    while shift < rows:
        b = b + jnp.where(row >= shift, pltpu.roll(b, shift, 0), 0.0)
        shift *= 2
    return b


def _gla_chunk(q, k, v, g, r, gain, mid, get_state, set_state):
    c = q.shape[0]
    b = _cumsum_rows(g)
    b_end = b[c - 1:c, :]
    b_mid = b[mid:mid + 1, :]
    q_in = (q * jnp.exp(b)).astype(BF16)
    q_rel = (q * jnp.exp(b - b_mid)).astype(BF16)
    k_rel = (k * jnp.exp(b_mid - b)).astype(BF16)
    k_out = (k * jnp.exp(b_end - b)).astype(BF16)
    decay_end = jnp.broadcast_to(jnp.exp(b_end), (SUBLANES, GLA_QK))
    ti = lax.broadcasted_iota(jnp.int32, (c, c), 0)
    si = lax.broadcasted_iota(jnp.int32, (c, c), 1)
    outs = []
    for h in range(GLA_HEADS):
        ks = slice(h * GLA_DK, (h + 1) * GLA_DK)
        vs = slice(h * GLA_DV, (h + 1) * GLA_DV)
        state = get_state(h)
        o = _dot(q_in[:, ks], state.astype(BF16))
        att = jnp.where(si <= ti, _dot_nt(q_rel[:, ks], k_rel[:, ks]), 0.0)
        o = o + _dot(att.astype(BF16), v[:, vs])
        decay_col = decay_end[:, ks].T[:, 0:1]
        set_state(h, decay_col * state + _dot_tn(k_out[:, ks], v[:, vs]))
        outs.append(_rms(o) * gain[:, vs] * _silu(r[:, vs]))
    return jnp.concatenate(outs, axis=-1)


def _gla_prompt_kernel(tt, q_ref, k_ref, v_ref, r_ref, la_ref, gain_ref, a_ref, so_ref, s_ref):
    t = pl.program_id(1)

    @pl.when(t == 0)
    def _():
        s_ref[...] = jnp.zeros_like(s_ref)

    gain = gain_ref[...]

    def get_state(h):
        return s_ref[h]

    def set_state(h, val):
        s_ref[h] = val

    def chunk(ci, carry):
        rows = pl.ds(pl.multiple_of(ci * GLA_CHUNK, GLA_CHUNK), GLA_CHUNK)
        a = _gla_chunk(q_ref[0, rows, :], k_ref[0, rows, :], v_ref[0, rows, :], la_ref[0, rows, :],
                       r_ref[0, rows, :], gain, GLA_CHUNK // 2, get_state, set_state)
        a_ref[0, rows, :] = a.astype(BF16)
        return carry

    lax.fori_loop(0, tt // GLA_CHUNK, chunk, 0, unroll=GLA_CHUNK_UNROLL)

    @pl.when(t == pl.num_programs(1) - 1)
    def _():
        so_ref[0] = s_ref[...]


def _gla_prompt(q, k, v, r, la, gain):
    batch, seq, _ = q.shape
    tt = GLA_TIME_TILE

    def spec(width):
        return pl.BlockSpec((1, tt, width), lambda b, t: (b, t, 0))

    return pl.pallas_call(
        functools.partial(_gla_prompt_kernel, tt),
        grid=(batch, seq // tt),
        in_specs=[spec(GLA_QK), spec(GLA_QK), spec(GLA_V), spec(GLA_V), spec(GLA_QK),
                  pl.BlockSpec((1, GLA_V), lambda b, t: (0, 0))],
        out_specs=[spec(GLA_V), pl.BlockSpec((1, GLA_HEADS, GLA_DK, GLA_DV), lambda b, t: (b, 0, 0, 0))],
        out_shape=[jax.ShapeDtypeStruct((batch, seq, GLA_V), BF16),
                   jax.ShapeDtypeStruct((batch, GLA_HEADS, GLA_DK, GLA_DV), F32)],
        scratch_shapes=[pltpu.VMEM((GLA_HEADS, GLA_DK, GLA_DV), F32)],
        compiler_params=_params("arbitrary", "arbitrary"),
        name="gla_prompt",
    )(q, k, v, r, la, gain)


def _gla_sample_kernel(sb, q_ref, k_ref, v_ref, r_ref, la_ref, gain_ref, s0_ref, a_ref, so_ref):
    gain = gain_ref[...]

    def seq_body(i, carry):
        def get_state(h):
            return s0_ref[i, h]

        def set_state(h, val):
            so_ref[i, h] = val

        a_ref[i] = _gla_chunk(q_ref[i], k_ref[i], v_ref[i], la_ref[i], r_ref[i], gain, 0, get_state, set_state)
        return carry

    lax.fori_loop(0, sb, seq_body, 0, unroll=GLA_SAMPLE_UNROLL)


def _gla_sample(q, k, v, r, la, gain, s0):
    nb, pad, _ = q.shape
    sb = math.gcd(nb, 8)

    def spec(width):
        return pl.BlockSpec((sb, pad, width), lambda i: (i, 0, 0))

    state_spec = pl.BlockSpec((sb, GLA_HEADS, GLA_DK, GLA_DV), lambda i: (i, 0, 0, 0))
    return pl.pallas_call(
        functools.partial(_gla_sample_kernel, sb),
        grid=(nb // sb,),
        in_specs=[spec(GLA_QK), spec(GLA_QK), spec(GLA_V), spec(GLA_V), spec(GLA_QK),
                  pl.BlockSpec((1, GLA_V), lambda i: (0, 0)), state_spec],
        out_specs=[spec(GLA_V), state_spec],
        out_shape=[jax.ShapeDtypeStruct((nb, pad, GLA_V), F32),
                   jax.ShapeDtypeStruct((nb, GLA_HEADS, GLA_DK, GLA_DV), F32)],
        compiler_params=_params("arbitrary"),
        name="gla_sample",
    )(q, k, v, r, la, gain, s0)


def _t5_bucket(dist):
    max_exact = NUM_BUCKETS // 2
    d_f = jnp.maximum(dist, 1).astype(F32)
    large = max_exact + (jnp.log(d_f / max_exact) / math.log(MAX_DISTANCE / max_exact)
                         * (NUM_BUCKETS - max_exact)).astype(jnp.int32)
    large = jnp.minimum(large, NUM_BUCKETS - 1)
    return jnp.where(dist < max_exact, dist, large)


def _group_bias(rel_bias):
    rows = []
    for g, (window, dil) in enumerate(DILATED_GROUPS):
        buckets = _t5_bucket(jnp.arange(window // dil + 1) * dil)
        rows.append(rel_bias[buckets][:, g * HEADS_PER_GROUP:(g + 1) * HEADS_PER_GROUP].T)
    return jnp.stack(rows)


def _prompt_bias_rows(gb):
    band = gb[:, :, ::-1]
    off = jnp.full(gb.shape[:2] + (2 * QUERY_BLOCK - KEYS_PER_QUERY,), NEG_INF, F32)
    return jnp.concatenate([band, off], axis=-1)[:, :, None, :]


def _sample_bias_tables(gb, steps, past_rows):
    tables, new_tables = [], []
    t_idx = np.arange(SUBLANES)[:, None]
    c_idx = np.arange(LANES)[None, :]
    for g, (window, dil) in enumerate(DILATED_GROUPS):
        p = past_rows[g]
        assert p == window == (KEYS_PER_QUERY - 1) * dil
        b = gb[g]
        heads = b.shape[0]
        row0 = b[:, :0:-1]
        if dil > 1:
            gaps = jnp.full((heads, KEYS_PER_QUERY - 1, dil - 1), NEG_INF, F32)
            row0 = jnp.concatenate([row0[:, :, None], gaps], axis=2).reshape(heads, p)
        rows = []
        for t in range(SUBLANES):
            if t < steps:
                rows.append(jnp.concatenate([jnp.full((heads, t), NEG_INF, F32), row0[:, :p - t]], axis=1))
            else:
                rows.append(jnp.full((heads, p), NEG_INF, F32))
        past = jnp.stack(rows, axis=1)
        new = jnp.full((heads, SUBLANES, LANES), NEG_INF, F32)
        for j in range((steps - 1) // dil + 1):
            mask = (t_idx - c_idx == j * dil) & (t_idx < steps) & (c_idx < steps)
            new = jnp.where(jnp.asarray(mask)[None], b[:, j][:, None, None], new)
        tables.append(past.reshape(heads // 2, 2 * SUBLANES, p))
        new_tables.append(new.reshape(heads // 2, 2 * SUBLANES, LANES))
    return tables, jnp.stack(new_tables)


def _ffn_weights(w_in, conv_w, conv_b, w_down):
    return (w_in.astype(BF16), w_down.astype(BF16), conv_w, conv_b.reshape(DEPTH, 1, D_FF))


def _conv_tail_prompt(cs):
    batch = cs.shape[0]
    tail = cs[:, :, SUBLANES - (CONV_WIDTH - 1):, :]
    return jnp.transpose(tail, (0, 2, 1, 3)).reshape(batch, CONV_WIDTH - 1, D_FF)


def _conv_past_sample(state):
    nb = state.shape[0]
    s = state.reshape(nb, CONV_WIDTH - 1, N_FF_CHUNKS, FF_CHUNK)
    return jnp.transpose(s, (2, 1, 0, 3)).reshape(N_FF_CHUNKS, (CONV_WIDTH - 1) * nb, FF_CHUNK)


def _conv_tail_sample(cs, nb):
    s = cs.reshape(N_FF_CHUNKS, CONV_WIDTH - 1, nb, FF_CHUNK)
    return jnp.transpose(s, (2, 1, 0, 3)).reshape(nb, CONV_WIDTH - 1, D_FF)


def kernel(x_prompt, x_sample, state_pool, cache_win_g1, cache_win_g2, cache_win_g3, state_gla, state_ffn_conv,
           c_prompt, c_sample, w_ada, b_ada, norm_gain, final_gain, rel_bias, pool_w, pool_scale,
           attn_w_in, attn_w_out, gla_w_in, gla_w_gate_up, gla_b_gate, gla_norm_gain, gla_w_out,
           ffn_w_in, ffn_conv_w, ffn_conv_b, ffn_w_down):
    batch, seq, _ = x_prompt.shape
    nb, steps, _ = x_sample.shape
    caches = (cache_win_g1, cache_win_g2, cache_win_g3)

    mod_p, mod_s = _modulation(c_prompt, c_sample, w_ada, b_ada)
    fgain = final_gain.reshape(1, D_MODEL)

    xp = x_prompt.reshape(batch * seq, D_MODEL)
    xs = jnp.transpose(x_sample, (1, 0, 2)).reshape(steps * nb, D_MODEL)

    pool_p, pool_s, gla_p, gla_s, conv_p, conv_s = [], [], [], [], [], []
    win_p, win_s = None, None
    ffn_stack = _ffn_weights(ffn_w_in, ffn_conv_w, ffn_conv_b, ffn_w_down)

    for i in range(DEPTH):
        kind, j = i % 3, i // 3
        last = i == DEPTH - 1
        ffn = (i, ffn_stack)
        conv_past = _conv_past_sample(state_ffn_conv[i])
        gains = norm_gain[i]
        if kind == 0:
            mix_w = (pool_w[j].astype(BF16), pool_scale[j].reshape(1, D_MODEL))
            xp, cs, pst = _layer_prompt("pool", last, xp, mod_p[i], gains, fgain, mix_w, ffn)
            pool_p.append(pst[:, POOL_CARRY_ROWS - POOL_STATE_ROWS:])
            past = jnp.transpose(state_pool[j], (1, 0, 2))
            xs, css, psts = _layer_sample("pool", last, xs, mod_s[i], gains, fgain, mix_w + (past,), ffn, conv_past)
            pool_s.append(jnp.transpose(psts, (1, 0, 2)))
        elif kind == 1:
            w = attn_w_in[j]
            w3 = jnp.stack([jnp.concatenate([w[:, s * ATTN_INNER + g * GROUP_WIDTH:
                                                s * ATTN_INNER + (g + 1) * GROUP_WIDTH] for s in range(3)], axis=1)
                            for g in range(N_GROUPS)]).astype(BF16)
            gb = _group_bias(rel_bias)
            wo = attn_w_out[j].astype(BF16)
            qkvp, *kv_t = _qkv_prompt(xp.reshape(batch, seq, D_MODEL), mod_p[i], gains, w3)
            o_all = _attn_prompt(qkvp, _prompt_bias_rows(gb))
            win_p = [jnp.transpose(t.reshape(batch, 2, HEADS_PER_GROUP, HEAD_DIM, t.shape[-1]), (0, 4, 1, 2, 3))[None]
                     for t in kv_t]
            xp, cs = _layer_prompt("proj", last, xp, mod_p[i], gains, fgain,
                                   (o_all.reshape(batch * seq, ATTN_INNER), wo), ffn)
            qkv_s = _proj_sample(xs, mod_s[i], gains, w.astype(BF16))
            q6 = qkv_s.reshape(steps, nb, 3, N_GROUPS, HEADS_PER_GROUP, HEAD_DIM)
            q_s = jnp.pad(jnp.transpose(q6[:, :, 0], (1, 2, 3, 0, 4)),
                          ((0, 0), (0, 0), (0, 0), (0, SUBLANES - steps), (0, 0)))
            q_s = q_s.reshape(nb, N_GROUPS, HEADS_PER_GROUP // 2, 2, SUBLANES, HEAD_DIM)
            zeros = jnp.zeros_like(q_s[:, :, :, 0])
            q_s = jnp.stack([jnp.concatenate([q_s[:, :, :, 0], zeros], axis=-1),
                             jnp.concatenate([zeros, q_s[:, :, :, 1]], axis=-1)], axis=3)
            q_s = q_s.reshape(nb, N_GROUPS, HEADS_PER_GROUP // 2, 2 * SUBLANES, LANES)
            kv_new = jnp.transpose(q6[:, :, 1:], (1, 3, 0, 2, 4, 5)).reshape(nb, N_GROUPS, steps, 2 * GROUP_WIDTH)
            kv_new = jnp.pad(kv_new, ((0, 0), (0, 0), (0, SUBLANES - steps), (0, 0)))
            past_rows = [c.shape[2] for c in caches]
            cache_t = [jnp.transpose(c[j], (0, 2, 3, 4, 1)).reshape(nb, 2 * HEADS_PER_GROUP, HEAD_DIM, p)
                       for c, p in zip(caches, past_rows)]
            bias_past, bias_new = _sample_bias_tables(gb, steps, past_rows)
            o8, *new_caches = _attn_sample(steps, q_s, kv_new, cache_t, bias_past, bias_new)
            a_s = jnp.transpose(o8[:, :steps], (1, 0, 2)).reshape(steps * nb, ATTN_INNER)
            win_s = [jnp.transpose(c.reshape(nb, 2, HEADS_PER_GROUP, HEAD_DIM, p), (0, 4, 1, 2, 3))[None]
                     for c, p in zip(new_caches, past_rows)]
            xs, css = _layer_sample("proj", last, xs, mod_s[i], gains, fgain, (a_s.astype(BF16), wo), ffn, conv_past)
        else:
            w = gla_w_in[j]
            n_main = 2 * GLA_QK + 2 * GLA_V
            weights = (w[:, :n_main].astype(BF16),
                       jnp.pad(w[:, n_main:], ((0, 0), (0, LANES - GATE_RANK))).astype(BF16),
                       jnp.pad(gla_w_gate_up[j], ((0, LANES - GATE_RANK), (0, 0))).astype(BF16),
                       gla_b_gate[j].reshape(1, GLA_QK))
            gain = gla_norm_gain[j].reshape(1, GLA_V)
            wo = gla_w_out[j].astype(BF16)
            q, k, v, r, la = _gla_proj(xp, mod_p[i], gains, weights, None, PROMPT_ROW_TILE)
            shp = lambda a: a.reshape(batch, seq, a.shape[-1])
            a_p, s_p = _gla_prompt(shp(q), shp(k), shp(v), shp(r), shp(la), gain)
            gla_p.append(s_p)
            xp, cs = _layer_prompt("proj", last, xp, mod_p[i], gains, fgain,
                                   (a_p.reshape(batch * seq, GLA_V), wo), ffn)
            outs = _gla_proj(xs, mod_s[i], gains, weights, nb, steps * nb)

            def per_seq(a):
                a = jnp.transpose(a.reshape(steps, nb, a.shape[-1]), (1, 0, 2))
                return jnp.pad(a, ((0, 0), (0, SAMPLE_DEC_PAD - steps), (0, 0)))

            qs, ks, vs, rs, las = (per_seq(a) for a in outs)
            a16, s_s = _gla_sample(qs, ks, vs, rs, las, gain, state_gla[j])
            gla_s.append(s_s)
            a_s = jnp.transpose(a16[:, :steps], (1, 0, 2)).reshape(steps * nb, GLA_V).astype(BF16)
            xs, css = _layer_sample("proj", last, xs, mod_s[i], gains, fgain, (a_s, wo), ffn, conv_past)
        conv_p.append(_conv_tail_prompt(cs))
        conv_s.append(_conv_tail_sample(css, nb))

    y_prompt = xp.reshape(batch, seq, D_MODEL)
    y_sample = jnp.transpose(xs.reshape(steps, nb, D_MODEL), (1, 0, 2))
    return (y_prompt, y_sample, jnp.stack(pool_p), jnp.stack(pool_s),
            win_p[0], win_s[0], win_p[1], win_s[1], win_p[2], win_s[2],
            jnp.stack(gla_p), jnp.stack(gla_s), jnp.stack(conv_p), jnp.stack(conv_s))
```

```python
import functools
import math

import numpy as np
import jax
import jax.numpy as jnp
from jax import lax
from jax.experimental import pallas as pl
from jax.experimental.pallas import tpu as pltpu

F32 = jnp.float32
BF16 = jnp.bfloat16

D_MODEL = 1024
DEPTH = 4
N_MOD = 6
EPS = 1e-6
NEG_INF = -1e30
POOL_WINDOWS = (2, 4, 8, 16)
POOL_GROUP_DIM = D_MODEL // len(POOL_WINDOWS)
POOL_STATE_ROWS = max(POOL_WINDOWS) - 1
POOL_CARRY_ROWS = 16
DILATED_GROUPS = ((128, 1), (512, 4), (2048, 16))
N_GROUPS = len(DILATED_GROUPS)
HEADS_PER_GROUP = 4
HEAD_DIM = 64
GROUP_WIDTH = HEADS_PER_GROUP * HEAD_DIM
ATTN_INNER = N_GROUPS * GROUP_WIDTH
KEYS_PER_QUERY = 129
QUERY_BLOCK = 128
NUM_BUCKETS = 32
MAX_DISTANCE = 2048
GLA_HEADS = 4
GLA_DK = 128
GLA_DV = 256
GLA_QK = GLA_HEADS * GLA_DK
GLA_V = GLA_HEADS * GLA_DV
GATE_RANK = 16
GATE_TAU = 16.0
GLA_CHUNK = 128
D_FF = 2816
CONV_WIDTH = 3

LANES = 128
SUBLANES = 8
FF_CHUNK = 256
CHEAP_STRIDE = 4
N_FF_CHUNKS = D_FF // FF_CHUNK
VMEM_LIMIT_BYTES = 56 * 1024 * 1024
PROMPT_ROW_TILE = 1024
MOD_SLOTS_PER_STEP = 2
GLA_TIME_TILE = 512
ATTN_BLOCK_UNROLL = 16
GLA_CHUNK_UNROLL = 4
GLA_SAMPLE_UNROLL = 8
ATTN_SAMPLE_SEQS = 2
SAMPLE_DEC_PAD = 16


def _params(*semantics):
    return pltpu.CompilerParams(dimension_semantics=semantics, vmem_limit_bytes=VMEM_LIMIT_BYTES)


def _resident(shape):
    nd = len(shape)
    return pl.BlockSpec(shape, lambda *_: (0,) * nd, pipeline_mode=pl.Buffered(1))


def _dot(a, b):
    return jnp.dot(a, b, preferred_element_type=F32)


def _dot_nt(a, b):
    return lax.dot_general(a, b, (((1,), (1,)), ((), ())), preferred_element_type=F32)


def _dot_tn(a, b):
    return lax.dot_general(a, b, (((0,), (0,)), ((), ())), preferred_element_type=F32)


def _rms(x):
    return x * lax.rsqrt(jnp.mean(x * x, axis=-1, keepdims=True) + EPS)


def _bcast_rows(v, y, nb):
    if nb is None:
        return v * y
    rows, width = y.shape
    return (y.reshape(rows // nb, nb, width) * v[None]).reshape(rows, width)


def _norm_mod(x, gain, shift, scale, nb):
    y = _rms(x) * gain
    if nb is None:
        return y * (1.0 + scale) + shift
    rows, width = y.shape
    y3 = y.reshape(rows // nb, nb, width)
    return (y3 * (1.0 + scale)[None] + shift[None]).reshape(rows, width)


def _prompt_mod(mod_ref, b):
    return [mod_ref[k, pl.ds(b, 1), :] for k in range(N_MOD)]


def _gelu(x):
    return 0.5 * x * (1.0 + lax.erf(x * (1.0 / math.sqrt(2.0))))


def _silu(x):
    return x * jax.nn.sigmoid(x)


def _split_bf16(a):
    hi = a.astype(BF16)
    lo = (a - hi.astype(F32)).astype(BF16)
    return hi, lo


def _mod_kernel(cp_ref, cs_ref, w_ref, b_ref, op_ref, os_ref):
    batch = op_ref.shape[2]
    pad = cp_ref.shape[0]
    rows = pad + cs_ref.shape[0]
    w_hi, w_lo = _split_bf16(w_ref[0])
    a_hi, a_lo = _split_bf16(_silu(jnp.concatenate([cp_ref[...], cs_ref[...]], axis=0)))
    both = _dot(jnp.concatenate([a_hi, a_lo], axis=0), w_hi)
    out = both[0:rows] + both[rows:2 * rows] + _dot(a_hi, w_lo) + b_ref[0]
    for k in range(MOD_SLOTS_PER_STEP):
        cols = slice(k * D_MODEL, (k + 1) * D_MODEL)
        op_ref[0, k] = out[0:batch, cols]
        os_ref[0, k] = out[pad:rows, cols]


def _modulation(c_prompt, c_sample, w_ada, b_ada):
    batch, nb = c_prompt.shape[0], c_sample.shape[0]
    bf16_rows = 2 * SUBLANES
    c_prompt = jnp.pad(c_prompt, ((0, -batch % bf16_rows), (0, 0)))
    return pl.pallas_call(
        _mod_kernel,
        grid=(DEPTH, N_MOD // MOD_SLOTS_PER_STEP),
        in_specs=[pl.BlockSpec(c_prompt.shape, lambda l, m: (0, 0)),
                  pl.BlockSpec((nb, D_MODEL), lambda l, m: (0, 0)),
                  pl.BlockSpec((1, D_MODEL, MOD_SLOTS_PER_STEP * D_MODEL), lambda l, m: (l, 0, m)),
                  pl.BlockSpec((1, 1, MOD_SLOTS_PER_STEP * D_MODEL), lambda l, m: (l, 0, m))],
        out_specs=[pl.BlockSpec((1, MOD_SLOTS_PER_STEP, batch, D_MODEL), lambda l, m: (l, m, 0, 0)),
                   pl.BlockSpec((1, MOD_SLOTS_PER_STEP, nb, D_MODEL), lambda l, m: (l, m, 0, 0))],
        out_shape=[jax.ShapeDtypeStruct((DEPTH, N_MOD, batch, D_MODEL), F32),
                   jax.ShapeDtypeStruct((DEPTH, N_MOD, nb, D_MODEL), F32)],
        compiler_params=_params("parallel", "parallel"),
        name="adaln_mod",
    )(c_prompt, c_sample, w_ada, b_ada.reshape(DEPTH, 1, N_MOD * D_MODEL))


def _ffn_chunk_math(g, g_m1, g_m2, u, cw, cb):
    gc = cw[2:3] * g + cw[1:2] * g_m1 + cw[0:1] * g_m2 + cb
    return (_gelu(gc) * u).astype(BF16)


def _ffn_chunk_weights(win_ref, wd_ref, cw_ref, cb_ref, j):
    cols = slice(j * FF_CHUNK, (j + 1) * FF_CHUNK)
    ucols = slice(D_FF + j * FF_CHUNK, D_FF + (j + 1) * FF_CHUNK)
    return win_ref[0, :, cols], win_ref[0, :, ucols], wd_ref[0, cols, :], cw_ref[0, :, cols], cb_ref[0, :, cols]


def _layer_resident(array, layer):
    nd = array.ndim
    return pl.BlockSpec((1,) + array.shape[1:], lambda *_: (layer,) + (0,) * (nd - 1),
                        pipeline_mode=pl.Buffered(1))


def _layer_prompt_kernel(mixer, last, tm, tiles_per_seq, *refs):
    refs = list(refs)
    x_ref, mod_ref, gains_ref, fg_ref = refs[:4]
    refs = refs[4:]
    if mixer == "pool":
        pw_ref, ps_ref = refs[:2]
    else:
        a_ref, wp_ref = refs[:2]
    win_ref, wd_ref, cw_ref, cb_ref = refs[2:6]
    refs = refs[6:]
    if mixer == "pool":
        y_ref, cs_ref, pst_ref, h2_ref, act_ref, gext_ref, cc_ref, hext_ref = refs
    else:
        y_ref, cs_ref, h2_ref, act_ref, gext_ref, cc_ref = refs

    i = pl.program_id(0)
    tile_in_seq = i % tiles_per_seq

    @pl.when(tile_in_seq == 0)
    def _():
        cc_ref[...] = jnp.zeros_like(cc_ref)
        if mixer == "pool":
            hext_ref[0:POOL_CARRY_ROWS, :] = jnp.zeros((POOL_CARRY_ROWS, D_MODEL), F32)

    x = x_ref[...]
    m = _prompt_mod(mod_ref, i // tiles_per_seq)
    gains = gains_ref[...]

    if mixer == "pool":
        h = _norm_mod(x, gains[0:1], m[0], m[1], None)
        hext_ref[POOL_CARRY_ROWS:, :] = h
        pos = tile_in_seq * tm + lax.broadcasted_iota(jnp.int32, (tm, 1), 0)
        parts = []
        for g, w in enumerate(POOL_WINDOWS):
            cols = slice(g * POOL_GROUP_DIM, (g + 1) * POOL_GROUP_DIM)
            s = hext_ref[:, cols]
            span = 1
            while span < w:
                s = s + pltpu.roll(s, span, 0)
                span *= 2
            inv_count = 1.0 / jnp.minimum(pos + 1, w).astype(F32)
            d = s[POOL_CARRY_ROWS:, :] * inv_count - h[:, cols]
            parts.append(_dot(d.astype(BF16), pw_ref[g]))
        mix = jnp.concatenate(parts, axis=-1) * ps_ref[...]
        tail = hext_ref[pl.ds(tm, POOL_CARRY_ROWS), :]
        hext_ref[0:POOL_CARRY_ROWS, :] = tail
        pst_ref[0] = tail
    else:
        mix = _dot(a_ref[...], wp_ref[...])

    x1 = x + m[2] * mix
    y_ref[...] = x1
    h2_ref[...] = _norm_mod(x1, gains[1:2], m[3], m[4], None).astype(BF16)

    def up_proj(j):
        wg, wu, _, _, _ = _ffn_chunk_weights(win_ref, wd_ref, cw_ref, cb_ref, j)
        h2 = h2_ref[...]
        return _dot(h2, wg), _dot(h2, wu)

    ahead = up_proj(0)
    for j in range(N_FF_CHUNKS):
        g, u = ahead
        if j + 1 < N_FF_CHUNKS:
            ahead = up_proj(j + 1)
        _, _, _, cw, cb = _ffn_chunk_weights(win_ref, wd_ref, cw_ref, cb_ref, j)
        gx = gext_ref.at[j % 2]
        gx[0:SUBLANES, :] = cc_ref[j]
        gx[SUBLANES:, :] = g
        act_ref[:, j * FF_CHUNK:(j + 1) * FF_CHUNK] = _ffn_chunk_math(
            g, gx[pl.ds(SUBLANES - 1, tm), :], gx[pl.ds(SUBLANES - 2, tm), :], u, cw, cb)
        tail = g[tm - SUBLANES:tm, :]
        cc_ref[j] = tail
        cs_ref[0, j] = tail
    xo = y_ref[...] + m[5] * _dot(act_ref[...], wd_ref[0])
    if last:
        xo = _rms(xo) * fg_ref[...]
    y_ref[...] = xo


def _layer_prompt(mixer, last, x, mod, gains, fgain, mix_args, ffn):
    n = x.shape[0]
    batch = mod.shape[1]
    seq = n // batch
    tm = PROMPT_ROW_TILE
    tps = seq // tm
    in_specs = [pl.BlockSpec((tm, D_MODEL), lambda i: (i, 0)),
                _resident(mod.shape), _resident((2, D_MODEL)), _resident((1, D_MODEL))]
    if mixer == "pool":
        pw, ps = mix_args
        in_specs += [_resident(pw.shape), _resident(ps.shape)]
    else:
        a, wp = mix_args
        in_specs += [pl.BlockSpec((tm, a.shape[1]), lambda i: (i, 0)), _resident(wp.shape)]
    layer, ffn_w = ffn
    in_specs += [_layer_resident(w, layer) for w in ffn_w]
    out_shape = [jax.ShapeDtypeStruct((n, D_MODEL), F32),
                 jax.ShapeDtypeStruct((batch, N_FF_CHUNKS, SUBLANES, FF_CHUNK), F32)]
    out_specs = [pl.BlockSpec((tm, D_MODEL), lambda i: (i, 0)),
                 pl.BlockSpec((1, N_FF_CHUNKS, SUBLANES, FF_CHUNK), lambda i: (i // tps, 0, 0, 0))]
    scratch = [pltpu.VMEM((tm, D_MODEL), BF16), pltpu.VMEM((tm, D_FF), BF16),
               pltpu.VMEM((2, tm + SUBLANES, FF_CHUNK), F32), pltpu.VMEM((N_FF_CHUNKS, SUBLANES, FF_CHUNK), F32)]
    if mixer == "pool":
        out_shape.append(jax.ShapeDtypeStruct((batch, POOL_CARRY_ROWS, D_MODEL), F32))
        out_specs.append(pl.BlockSpec((1, POOL_CARRY_ROWS, D_MODEL), lambda i: (i // tps, 0, 0)))
        scratch.append(pltpu.VMEM((tm + POOL_CARRY_ROWS, D_MODEL), F32))
    return pl.pallas_call(
        functools.partial(_layer_prompt_kernel, mixer, last, tm, tps),
        grid=(n // tm,), in_specs=in_specs, out_specs=out_specs, out_shape=out_shape,
        scratch_shapes=scratch, compiler_params=_params("arbitrary"),
        name=f"layer_prompt_{mixer}",
    )(x, mod, gains, fgain, *mix_args, *ffn_w)


def _layer_sample_kernel(mixer, last, nb, steps, *refs):
    refs = list(refs)
    x_ref, mod_ref, gains_ref, fg_ref = refs[:4]
    refs = refs[4:]
    if mixer == "pool":
        pw_ref, ps_ref, ppast_ref = refs[:3]
        refs = refs[3:]
    else:
        a_ref, wp_ref = refs[:2]
        refs = refs[2:]
    win_ref, wd_ref, cw_ref, cb_ref, cpast_ref = refs[:5]
    refs = refs[5:]
    if mixer == "pool":
        y_ref, cs_ref, pst_ref, h2_ref, act_ref = refs
    else:
        y_ref, cs_ref, h2_ref, act_ref = refs
    rows = steps * nb

    x = x_ref[...]
    gains = gains_ref[...]
    if mixer == "pool":
        h = _norm_mod(x, gains[0:1], mod_ref[0], mod_ref[1], nb)
        new = [h[t * nb:(t + 1) * nb, :] for t in range(steps)]

        def u_rows(p, cols):
            if p < POOL_STATE_ROWS:
                return ppast_ref[p, :, cols]
            return new[p - POOL_STATE_ROWS][:, cols]

        parts = []
        for g, w in enumerate(POOL_WINDOWS):
            cols = slice(g * POOL_GROUP_DIM, (g + 1) * POOL_GROUP_DIM)
            ds = []
            for t in range(steps):
                s = u_rows(POOL_STATE_ROWS + t, cols)
                for k in range(1, w):
                    s = s + u_rows(POOL_STATE_ROWS + t - k, cols)
                ds.append(s * (1.0 / w) - new[t][:, cols])
            parts.append(_dot(jnp.concatenate(ds, axis=0).astype(BF16), pw_ref[g]))
        mix = jnp.concatenate(parts, axis=-1) * ps_ref[...]
        full = slice(0, D_MODEL)
        for p in range(POOL_STATE_ROWS):
            pst_ref[p] = u_rows(p + steps, full)
    else:
        mix = _dot(a_ref[...], wp_ref[...])

    x1 = x + _bcast_rows(mod_ref[2], mix, nb)
    y_ref[...] = x1
    h2_ref[...] = _norm_mod(x1, gains[1:2], mod_ref[3], mod_ref[4], nb).astype(BF16)
    past_rows = (CONV_WIDTH - 1) * nb

    for j in range(N_FF_CHUNKS):
        h2 = h2_ref[...]
        wg, wu, _, cw, cb = _ffn_chunk_weights(win_ref, wd_ref, cw_ref, cb_ref, j)
        g = _dot(h2, wg)
        u = _dot(h2, wu)
        gall = jnp.concatenate([cpast_ref[j], g], axis=0)
        act_ref[:, j * FF_CHUNK:(j + 1) * FF_CHUNK] = _ffn_chunk_math(
            g, gall[nb:nb + rows, :], gall[0:rows, :], u, cw, cb)
        cs_ref[j] = gall[rows:rows + past_rows, :]
    xo = y_ref[...] + _bcast_rows(mod_ref[5], _dot(act_ref[...], wd_ref[0]), nb)
    if last:
        xo = _rms(xo) * fg_ref[...]
    y_ref[...] = xo


def _layer_sample(mixer, last, x, mod, gains, fgain, mix_args, ffn, conv_past):
    rows = x.shape[0]
    nb = mod.shape[1]
    steps = rows // nb
    layer, ffn_w = ffn
    head = [x, mod, gains, fgain, *mix_args]
    args = [*head, *ffn_w, conv_past]
    out_shape = [jax.ShapeDtypeStruct((rows, D_MODEL), F32),
                 jax.ShapeDtypeStruct(conv_past.shape, F32)]
    if mixer == "pool":
        out_shape.append(jax.ShapeDtypeStruct((POOL_STATE_ROWS, nb, D_MODEL), F32))
    return pl.pallas_call(
        functools.partial(_layer_sample_kernel, mixer, last, nb, steps),
        grid=(1,),
        in_specs=[_resident(a.shape) for a in head] + [_layer_resident(w, layer) for w in ffn_w]
                 + [_resident(conv_past.shape)],
        out_specs=[pl.BlockSpec(s.shape, functools.partial(lambda nd, i: (0,) * nd, len(s.shape))) for s in out_shape],
        out_shape=out_shape,
        scratch_shapes=[pltpu.VMEM((rows, D_MODEL), BF16), pltpu.VMEM((rows, D_FF), BF16)],
        compiler_params=_params("arbitrary"),
        name=f"layer_sample_{mixer}",
    )(*args)


def _qkv_prompt_kernel(seq, x_ref, mod_ref, gains_ref, w_ref, qkvp_ref, kt1_ref, kt2_ref, kt3_ref, h_ref, slab_ref, tmp_ref):
    g = pl.program_id(1)
    kt_refs = (kt1_ref, kt2_ref, kt3_ref)

    @pl.when(g == 0)
    def _():
        m = _prompt_mod(mod_ref, pl.program_id(0))
        h_ref[...] = _norm_mod(x_ref[0], gains_ref[0:1, :], m[0], m[1], None).astype(BF16)

    h = h_ref[...]
    for c in range(3):
        r = _dot(h, w_ref[0, :, c * GROUP_WIDTH:(c + 1) * GROUP_WIDTH])
        if c == 0:
            r = r * (HEAD_DIM ** -0.5)
        slab_ref[2 * c] = r[:, 0:LANES]
        slab_ref[2 * c + 1] = r[:, LANES:2 * LANES]

    qkvp_ref[0, 0, 0:QUERY_BLOCK, :] = jnp.zeros((QUERY_BLOCK, 3 * GROUP_WIDTH), BF16)
    for gi, (_, dil) in enumerate(DILATED_GROUPS):
        @pl.when(g == gi)
        def _(dil=dil):
            per_class = seq // dil
            assert dil in (1, CHEAP_STRIDE, CHEAP_STRIDE ** 2)
            for s in range(6):
                if dil > CHEAP_STRIDE:
                    part = seq // CHEAP_STRIDE
                    for r in range(CHEAP_STRIDE):
                        tmp_ref[r * part:(r + 1) * part, :] = slab_ref[s, pl.ds(r, part, stride=CHEAP_STRIDE), :]
                for rho in range(dil):
                    if dil == 1:
                        v = slab_ref[s]
                    elif dil == CHEAP_STRIDE:
                        v = slab_ref[s, pl.ds(rho, per_class, stride=dil), :]
                    else:
                        r, q = rho % CHEAP_STRIDE, rho // CHEAP_STRIDE
                        v = tmp_ref[pl.ds(r * part + q, per_class, stride=dil // CHEAP_STRIDE), :]
                    r0 = QUERY_BLOCK + rho * per_class
                    qkvp_ref[0, 0, r0:r0 + per_class, s * LANES:(s + 1) * LANES] = v.astype(BF16)

    for gi, kt_ref in enumerate(kt_refs):
        @pl.when(g == gi)
        def _(kt_ref=kt_ref):
            keep = kt_ref.shape[-1]
            for s in range(2, 6):
                tile_t = slab_ref[s, seq - keep:seq, :].T
                for r in range(2):
                    kt_ref[0, 2 * (s - 2) + r] = tile_t[r * HEAD_DIM:(r + 1) * HEAD_DIM, :]


def _qkv_prompt(x3, mod, gains, w3):
    batch, seq, _ = x3.shape
    width = 3 * GROUP_WIDTH
    keeps = [min(window, seq) for window, _ in DILATED_GROUPS]
    return pl.pallas_call(
        functools.partial(_qkv_prompt_kernel, seq),
        grid=(batch, N_GROUPS),
        in_specs=[pl.BlockSpec((1, seq, D_MODEL), lambda b, g: (b, 0, 0)),
                  pl.BlockSpec(mod.shape, lambda b, g: (0, 0, 0)),
                  pl.BlockSpec((2, D_MODEL), lambda b, g: (0, 0)),
                  pl.BlockSpec((1, D_MODEL, width), lambda b, g: (g, 0, 0))],
        out_specs=[pl.BlockSpec((1, 1, seq + QUERY_BLOCK, width), lambda b, g: (b, g, 0, 0))]
                  + [pl.BlockSpec((1, 2 * HEADS_PER_GROUP, HEAD_DIM, keep), lambda b, g: (b, 0, 0, 0)) for keep in keeps],
        out_shape=[jax.ShapeDtypeStruct((batch, N_GROUPS, seq + QUERY_BLOCK, width), BF16)]
                  + [jax.ShapeDtypeStruct((batch, 2 * HEADS_PER_GROUP, HEAD_DIM, keep), F32) for keep in keeps],
        scratch_shapes=[pltpu.VMEM((seq, D_MODEL), BF16), pltpu.VMEM((6, seq, LANES), F32),
                        pltpu.VMEM((seq, LANES), F32)],
        compiler_params=_params("arbitrary", "arbitrary"),
        name="qkv_prompt",
    )(x3, mod, gains, w3)


def _head_lane_mask(rows, h):
    lane = lax.broadcasted_iota(jnp.int32, (rows, GROUP_WIDTH), 1)
    return (lane >= h * HEAD_DIM) & (lane < (h + 1) * HEAD_DIM)


def _attn_block(q, k, v, bias_ref, cols):
    masks = [_head_lane_mask(QUERY_BLOCK, h) for h in range(HEADS_PER_GROUP)]
    qs = jnp.concatenate([jnp.where(hm, q, jnp.zeros_like(q)) for hm in masks], axis=0)
    s = _dot_nt(qs, k) + bias_ref[:, cols]
    m = jnp.max(s, axis=-1, keepdims=True)
    p = jnp.exp(s - m)
    l = jnp.sum(p, axis=-1, keepdims=True)
    pv = _dot(p.astype(BF16), v) * (1.0 / l)
    lse_rows = jnp.broadcast_to(m + jnp.log(l), pv.shape)
    o = pv[0:QUERY_BLOCK]
    lse = lse_rows[0:QUERY_BLOCK]
    for h in range(1, HEADS_PER_GROUP):
        rows = slice(h * QUERY_BLOCK, (h + 1) * QUERY_BLOCK)
        o = jnp.where(masks[h], pv[rows], o)
        lse = jnp.where(masks[h], lse_rows[rows], lse)
    return o, lse


def _attn_prompt_kernel(seq, qkv_ref, brow_ref, o_ref, os_ref, ls_ref, bias_ref):
    @pl.when(pl.program_id(0) == 0)
    def _():
        left = lax.broadcasted_iota(jnp.int32, (QUERY_BLOCK, 2 * QUERY_BLOCK), 1) < QUERY_BLOCK
        for g in range(N_GROUPS):
            for h in range(HEADS_PER_GROUP):
                base = jnp.broadcast_to(brow_ref[g, h], (QUERY_BLOCK, 2 * QUERY_BLOCK))
                band = pltpu.roll(base, 0, 1, stride=1, stride_axis=0)
                rows = slice(h * QUERY_BLOCK, (h + 1) * QUERY_BLOCK)
                bias_ref[g, 0, rows, :] = band
                bias_ref[g, 1, rows, :] = jnp.where(left, NEG_INF, band)

    qc = slice(0, GROUP_WIDTH)
    kc = slice(GROUP_WIDTH, 2 * GROUP_WIDTH)
    vc = slice(2 * GROUP_WIDTH, 3 * GROUP_WIDTH)
    n_blocks = seq // QUERY_BLOCK
    for g, (_, dil) in enumerate(DILATED_GROUPS):
        blocks_per_class = n_blocks // dil

        def do_block(blk, carry, g=g, dil=dil, blocks_per_class=blocks_per_class):
            r0 = pl.multiple_of(blk * QUERY_BLOCK, QUERY_BLOCK)
            cur = pl.ds(r0 + QUERY_BLOCK, QUERY_BLOCK)
            q = qkv_ref[0, g, cur, qc]
            rho = blk // blocks_per_class
            in_class = blk % blocks_per_class
            if blocks_per_class == 1:
                o, lse = _attn_block(q, qkv_ref[0, g, cur, kc], qkv_ref[0, g, cur, vc],
                                     bias_ref.at[g, 0], slice(QUERY_BLOCK, 2 * QUERY_BLOCK))
            else:
                both = pl.ds(r0, 2 * QUERY_BLOCK)
                first = jnp.asarray(in_class == 0, jnp.int32)
                o, lse = _attn_block(q, qkv_ref[0, g, both, kc], qkv_ref[0, g, both, vc],
                                     bias_ref.at[g, first], slice(0, 2 * QUERY_BLOCK))
            start = in_class * (QUERY_BLOCK * dil) + rho
            for s in range(2):
                cols = slice(s * LANES, (s + 1) * LANES)
                if dil == 1:
                    os_ref[g, s, pl.ds(r0, QUERY_BLOCK), :] = o[:, cols]
                    ls_ref[g, s, pl.ds(r0, QUERY_BLOCK), :] = lse[:, cols]
                else:
                    os_ref[g, s, pl.ds(start, QUERY_BLOCK, stride=dil), :] = o[:, cols]
                    ls_ref[g, s, pl.ds(start, QUERY_BLOCK, stride=dil), :] = lse[:, cols]
            return carry

        lax.fori_loop(0, n_blocks, do_block, 0, unroll=ATTN_BLOCK_UNROLL)

    def merge(i, carry):
        r0 = pl.multiple_of(i * QUERY_BLOCK, QUERY_BLOCK)
        rows = pl.ds(r0, QUERY_BLOCK)
        for s in range(2):
            ls = [ls_ref[g, s, rows, :] for g in range(N_GROUPS)]
            mx = jnp.maximum(jnp.maximum(ls[0], ls[1]), ls[2])
            es = [jnp.exp(l - mx) for l in ls]
            inv = 1.0 / (es[0] + es[1] + es[2])
            for g in range(N_GROUPS):
                c0 = g * GROUP_WIDTH + s * LANES
                o_ref[0, rows, c0:c0 + LANES] = (os_ref[g, s, rows, :] * (es[g] * inv)).astype(BF16)
        return carry

    lax.fori_loop(0, seq // QUERY_BLOCK, merge, 0)


def _attn_prompt(qkvp, bias_rows):
    batch, _, padded, width = qkvp.shape
    seq = padded - QUERY_BLOCK
    return pl.pallas_call(
        functools.partial(_attn_prompt_kernel, seq),
        grid=(batch,),
        in_specs=[pl.BlockSpec((1, N_GROUPS, padded, width), lambda b: (b, 0, 0, 0)),
                  _resident(bias_rows.shape)],
        out_specs=pl.BlockSpec((1, seq, ATTN_INNER), lambda b: (b, 0, 0)),
        out_shape=jax.ShapeDtypeStruct((batch, seq, ATTN_INNER), BF16),
        scratch_shapes=[pltpu.VMEM((N_GROUPS, 2, seq, LANES), F32), pltpu.VMEM((N_GROUPS, 2, seq, LANES), F32),
                        pltpu.VMEM((N_GROUPS, 2, HEADS_PER_GROUP * QUERY_BLOCK, 2 * QUERY_BLOCK), F32)],
        compiler_params=_params("arbitrary"),
        name="attn_prompt",
    )(qkvp, bias_rows)


def _proj_sample_kernel(nb, x_ref, mod_ref, gains_ref, w_ref, o_ref):
    h = _norm_mod(x_ref[...], gains_ref[0:1, :], mod_ref[0], mod_ref[1], nb).astype(BF16)
    o_ref[...] = _dot(h, w_ref[...])


def _proj_sample(x, mod, gains, w):
    rows = x.shape[0]
    nb = mod.shape[1]
    args = [x, mod, gains, w]
    return pl.pallas_call(
        functools.partial(_proj_sample_kernel, nb),
        grid=(1,),
        in_specs=[_resident(a.shape) for a in args],
        out_specs=pl.BlockSpec((rows, w.shape[1]), lambda i: (0, 0)),
        out_shape=jax.ShapeDtypeStruct((rows, w.shape[1]), F32),
        compiler_params=_params("arbitrary"),
        name="qkv_sample",
    )(*args)


def _split3_bf16(a):
    hi = a.astype(BF16)
    r1 = a - hi.astype(F32)
    mid = r1.astype(BF16)
    lo = (r1 - mid.astype(F32)).astype(BF16)
    return hi, mid, lo


def _attn_sample_kernel(steps, q_ref, kvn_ref, c1_ref, c2_ref, c3_ref, t1_ref, t2_ref, t3_ref, bn_ref,
                        o_ref, n1_ref, n2_ref, n3_ref):
    for i in range(q_ref.shape[0]):
        _attn_sample_one(steps, i, q_ref, kvn_ref, ((c1_ref, t1_ref, n1_ref), (c2_ref, t2_ref, n2_ref),
                                                  (c3_ref, t3_ref, n3_ref)), bn_ref, o_ref)


def _attn_sample_one(steps, i, q_ref, kvn_ref, groups, bn_ref, o_ref):
    pairs = HEADS_PER_GROUP // 2
    outs, lses = [], []
    lane = lax.broadcasted_iota(jnp.int32, (SUBLANES, LANES), 1)
    row = lax.broadcasted_iota(jnp.int32, (SUBLANES, LANES), 0)
    own_lanes = lane < HEAD_DIM
    sel_head = jnp.where((lane == row) & (row < steps), 1.0, 0.0).astype(BF16)
    sel_tail = jnp.where((lane == row + (LANES - steps)) & (row < steps), 1.0, 0.0).astype(BF16)
    for g, (c_ref, t_ref, n_ref) in enumerate(groups):
        p_rows = c_ref.shape[-1]
        pieces = _split3_bf16(kvn_ref[i, g])
        new_head = sum(_dot_tn(x, sel_head) for x in pieces)
        new_tail = sum(_dot_tn(x, sel_tail) for x in pieces)
        o_pairs, l_pairs = [], []
        for j in range(pairs):
            q = (q_ref[i, g, j] * (HEAD_DIM ** -0.5)).astype(BF16)
            kt = c_ref[i, 2 * j:2 * j + 2].reshape(LANES, p_rows).astype(BF16)
            vt = c_ref[i, HEADS_PER_GROUP + 2 * j:HEADS_PER_GROUP + 2 * j + 2].reshape(LANES, p_rows).astype(BF16)
            k_new = new_head[j * LANES:(j + 1) * LANES].astype(BF16)
            v_new = new_head[(pairs + j) * LANES:(pairs + j + 1) * LANES].astype(BF16)
            s = _dot(q, kt) + t_ref[j]
            s_new = _dot(q, k_new) + bn_ref[g, j]
            m = jnp.maximum(jnp.max(s, axis=-1, keepdims=True), jnp.max(s_new, axis=-1, keepdims=True))
            p = jnp.exp(s - m)
            p_new = jnp.exp(s_new - m)
            l = jnp.sum(p, axis=-1, keepdims=True) + jnp.sum(p_new, axis=-1, keepdims=True)
            o = _dot_nt(p.astype(BF16), vt) + _dot_nt(p_new.astype(BF16), v_new)
            o_pairs.append(o * (1.0 / l))
            l_pairs.append(m + jnp.log(l))
            for idx in (2 * j, 2 * j + 1, HEADS_PER_GROUP + 2 * j, HEADS_PER_GROUP + 2 * j + 1):
                n_ref[i, idx] = pltpu.roll(c_ref[i, idx], p_rows - steps, 1)
                n_ref[i, idx, :, p_rows - steps:p_rows] = new_tail[idx * HEAD_DIM:(idx + 1) * HEAD_DIM, LANES - steps:LANES]
        outs.append(o_pairs)
        lses.append(l_pairs)
    for j in range(pairs):
        ls = [lses[g][j] for g in range(N_GROUPS)]
        mx = jnp.maximum(jnp.maximum(ls[0], ls[1]), ls[2])
        es = [jnp.exp(l - mx) for l in ls]
        inv = 1.0 / (es[0] + es[1] + es[2])
        for g in range(N_GROUPS):
            og = outs[g][j] * (es[g] * inv)
            c0 = g * GROUP_WIDTH + j * LANES
            o_ref[i, :, c0:c0 + LANES] = jnp.where(own_lanes, og[0:SUBLANES], og[SUBLANES:2 * SUBLANES])


def _attn_sample(steps, q, kv_new, caches, tables, bias_new):
    nb = q.shape[0]
    sb = math.gcd(nb, ATTN_SAMPLE_SEQS)
    in_specs = [pl.BlockSpec((sb,) + q.shape[1:], lambda b: (b, 0, 0, 0, 0)),
                pl.BlockSpec((sb,) + kv_new.shape[1:], lambda b: (b, 0, 0, 0))]
    cache_specs = [pl.BlockSpec((sb,) + c.shape[1:], lambda b: (b, 0, 0, 0)) for c in caches]
    in_specs += cache_specs + [_resident(t.shape) for t in tables] + [_resident(bias_new.shape)]
    return pl.pallas_call(
        functools.partial(_attn_sample_kernel, steps),
        grid=(nb // sb,), in_specs=in_specs,
        out_specs=[pl.BlockSpec((sb, SUBLANES, ATTN_INNER), lambda b: (b, 0, 0))] + cache_specs,
        out_shape=[jax.ShapeDtypeStruct((nb, SUBLANES, ATTN_INNER), F32)]
                  + [jax.ShapeDtypeStruct(c.shape, F32) for c in caches],
        compiler_params=_params("arbitrary"),
        name="attn_sample",
    )(q, kv_new, *caches, *tables, bias_new)


def _gla_proj_kernel(nb, tiles_per_seq, x_ref, mod_ref, gains_ref, w_ref, wgd_ref, wgu_ref, bg_ref,
                     q_ref, k_ref, v_ref, r_ref, la_ref):
    if nb is None:
        m = _prompt_mod(mod_ref, pl.program_id(0) // tiles_per_seq)
        shift, scale = m[0], m[1]
    else:
        shift, scale = mod_ref[0], mod_ref[1]
    h = _norm_mod(x_ref[...], gains_ref[0:1, :], shift, scale, nb).astype(BF16)
    q_ref[...] = _dot(h, w_ref[:, 0:GLA_QK]) * (GLA_DK ** -0.5)
    k_ref[...] = _dot(h, w_ref[:, GLA_QK:2 * GLA_QK])
    v_ref[...] = _dot(h, w_ref[:, 2 * GLA_QK:2 * GLA_QK + GLA_V]).astype(BF16)
    r_ref[...] = _dot(h, w_ref[:, 2 * GLA_QK + GLA_V:2 * GLA_QK + 2 * GLA_V])
    gd = _dot(h, wgd_ref[...])
    gate = _dot(gd.astype(BF16), wgu_ref[...]) + bg_ref[...]
    la_ref[...] = jax.nn.log_sigmoid(gate) * (1.0 / GATE_TAU)


def _gla_proj(x, mod, gains, weights, nb, tm):
    rows = x.shape[0]
    tps = None if nb is not None else rows // mod.shape[1] // tm
    widths = (GLA_QK, GLA_QK, GLA_V, GLA_V, GLA_QK)
    dtypes = (F32, F32, BF16, F32, F32)
    return pl.pallas_call(
        functools.partial(_gla_proj_kernel, nb, tps),
        grid=(rows // tm,),
        in_specs=[pl.BlockSpec((tm, D_MODEL), lambda i: (i, 0)), _resident(mod.shape), _resident((2, D_MODEL))]
                 + [_resident(w.shape) for w in weights],
        out_specs=[pl.BlockSpec((tm, w), lambda i: (i, 0)) for w in widths],
        out_shape=[jax.ShapeDtypeStruct((rows, w), dt) for w, dt in zip(widths, dtypes)],
        compiler_params=_params("arbitrary"),
        name="gla_proj",
    )(x, mod, gains, *weights)


def _cumsum_rows(g):
    rows = g.shape[0]
    row = lax.broadcasted_iota(jnp.int32, g.shape, 0)
    b = g
    shift = 1
    while shift < rows:
        b = b + jnp.where(row >= shift, pltpu.roll(b, shift, 0), 0.0)
        shift *= 2
    return b


def _gla_chunk(q, k, v, g, r, gain, mid, get_state, set_state):
    c = q.shape[0]
    b = _cumsum_rows(g)
    b_end = b[c - 1:c, :]
    b_mid = b[mid:mid + 1, :]
    q_in = (q * jnp.exp(b)).astype(BF16)
    q_rel = (q * jnp.exp(b - b_mid)).astype(BF16)
    k_rel = (k * jnp.exp(b_mid - b)).astype(BF16)
    k_out = (k * jnp.exp(b_end - b)).astype(BF16)
    decay_end = jnp.broadcast_to(jnp.exp(b_end), (SUBLANES, GLA_QK))
    ti = lax.broadcasted_iota(jnp.int32, (c, c), 0)
    si = lax.broadcasted_iota(jnp.int32, (c, c), 1)
    outs = []
    for h in range(GLA_HEADS):
        ks = slice(h * GLA_DK, (h + 1) * GLA_DK)
        vs = slice(h * GLA_DV, (h + 1) * GLA_DV)
        state = get_state(h)
        o = _dot(q_in[:, ks], state.astype(BF16))
        att = jnp.where(si <= ti, _dot_nt(q_rel[:, ks], k_rel[:, ks]), 0.0)
        o = o + _dot(att.astype(BF16), v[:, vs])
        decay_col = decay_end[:, ks].T[:, 0:1]
        set_state(h, decay_col * state + _dot_tn(k_out[:, ks], v[:, vs]))
        outs.append(_rms(o) * gain[:, vs] * _silu(r[:, vs]))
    return jnp.concatenate(outs, axis=-1)


def _gla_prompt_kernel(tt, q_ref, k_ref, v_ref, r_ref, la_ref, gain_ref, a_ref, so_ref, s_ref):
    t = pl.program_id(1)

    @pl.when(t == 0)
    def _():
        s_ref[...] = jnp.zeros_like(s_ref)

    gain = gain_ref[...]

    def get_state(h):
        return s_ref[h]

    def set_state(h, val):
        s_ref[h] = val

    def chunk(ci, carry):
        rows = pl.ds(pl.multiple_of(ci * GLA_CHUNK, GLA_CHUNK), GLA_CHUNK)
        a = _gla_chunk(q_ref[0, rows, :], k_ref[0, rows, :], v_ref[0, rows, :], la_ref[0, rows, :],
                       r_ref[0, rows, :], gain, GLA_CHUNK // 2, get_state, set_state)
        a_ref[0, rows, :] = a.astype(BF16)
        return carry

    lax.fori_loop(0, tt // GLA_CHUNK, chunk, 0, unroll=GLA_CHUNK_UNROLL)

    @pl.when(t == pl.num_programs(1) - 1)
    def _():
        so_ref[0] = s_ref[...]


def _gla_prompt(q, k, v, r, la, gain):
    batch, seq, _ = q.shape
    tt = GLA_TIME_TILE

    def spec(width):
        return pl.BlockSpec((1, tt, width), lambda b, t: (b, t, 0))

    return pl.pallas_call(
        functools.partial(_gla_prompt_kernel, tt),
        grid=(batch, seq // tt),
        in_specs=[spec(GLA_QK), spec(GLA_QK), spec(GLA_V), spec(GLA_V), spec(GLA_QK),
                  pl.BlockSpec((1, GLA_V), lambda b, t: (0, 0))],
        out_specs=[spec(GLA_V), pl.BlockSpec((1, GLA_HEADS, GLA_DK, GLA_DV), lambda b, t: (b, 0, 0, 0))],
        out_shape=[jax.ShapeDtypeStruct((batch, seq, GLA_V), BF16),
                   jax.ShapeDtypeStruct((batch, GLA_HEADS, GLA_DK, GLA_DV), F32)],
        scratch_shapes=[pltpu.VMEM((GLA_HEADS, GLA_DK, GLA_DV), F32)],
        compiler_params=_params("arbitrary", "arbitrary"),
        name="gla_prompt",
    )(q, k, v, r, la, gain)


def _gla_sample_kernel(sb, q_ref, k_ref, v_ref, r_ref, la_ref, gain_ref, s0_ref, a_ref, so_ref):
    gain = gain_ref[...]

    def seq_body(i, carry):
        def get_state(h):
            return s0_ref[i, h]

        def set_state(h, val):
            so_ref[i, h] = val

        a_ref[i] = _gla_chunk(q_ref[i], k_ref[i], v_ref[i], la_ref[i], r_ref[i], gain, 0, get_state, set_state)
        return carry

    lax.fori_loop(0, sb, seq_body, 0, unroll=GLA_SAMPLE_UNROLL)


def _gla_sample(q, k, v, r, la, gain, s0):
    nb, pad, _ = q.shape
    sb = math.gcd(nb, 8)

    def spec(width):
        return pl.BlockSpec((sb, pad, width), lambda i: (i, 0, 0))

    state_spec = pl.BlockSpec((sb, GLA_HEADS, GLA_DK, GLA_DV), lambda i: (i, 0, 0, 0))
    return pl.pallas_call(
        functools.partial(_gla_sample_kernel, sb),
        grid=(nb // sb,),
        in_specs=[spec(GLA_QK), spec(GLA_QK), spec(GLA_V), spec(GLA_V), spec(GLA_QK),
                  pl.BlockSpec((1, GLA_V), lambda i: (0, 0)), state_spec],
        out_specs=[spec(GLA_V), state_spec],
        out_shape=[jax.ShapeDtypeStruct((nb, pad, GLA_V), F32),
                   jax.ShapeDtypeStruct((nb, GLA_HEADS, GLA_DK, GLA_DV), F32)],
        compiler_params=_params("arbitrary"),
        name="gla_sample",
    )(q, k, v, r, la, gain, s0)


def _t5_bucket(dist):
    max_exact = NUM_BUCKETS // 2
    d_f = jnp.maximum(dist, 1).astype(F32)
    large = max_exact + (jnp.log(d_f / max_exact) / math.log(MAX_DISTANCE / max_exact)
                         * (NUM_BUCKETS - max_exact)).astype(jnp.int32)
    large = jnp.minimum(large, NUM_BUCKETS - 1)
    return jnp.where(dist < max_exact, dist, large)


def _group_bias(rel_bias):
    rows = []
    for g, (window, dil) in enumerate(DILATED_GROUPS):
        buckets = _t5_bucket(jnp.arange(window // dil + 1) * dil)
        rows.append(rel_bias[buckets][:, g * HEADS_PER_GROUP:(g + 1) * HEADS_PER_GROUP].T)
    return jnp.stack(rows)


def _prompt_bias_rows(gb):
    band = gb[:, :, ::-1]
    off = jnp.full(gb.shape[:2] + (2 * QUERY_BLOCK - KEYS_PER_QUERY,), NEG_INF, F32)
    return jnp.concatenate([band, off], axis=-1)[:, :, None, :]


def _sample_bias_tables(gb, steps, past_rows):
    tables, new_tables = [], []
    t_idx = np.arange(SUBLANES)[:, None]
    c_idx = np.arange(LANES)[None, :]
    for g, (window, dil) in enumerate(DILATED_GROUPS):
        p = past_rows[g]
        assert p == window == (KEYS_PER_QUERY - 1) * dil
        b = gb[g]
        heads = b.shape[0]
        row0 = b[:, :0:-1]
        if dil > 1:
            gaps = jnp.full((heads, KEYS_PER_QUERY - 1, dil - 1), NEG_INF, F32)
            row0 = jnp.concatenate([row0[:, :, None], gaps], axis=2).reshape(heads, p)
        rows = []
        for t in range(SUBLANES):
            if t < steps:
                rows.append(jnp.concatenate([jnp.full((heads, t), NEG_INF, F32), row0[:, :p - t]], axis=1))
            else:
                rows.append(jnp.full((heads, p), NEG_INF, F32))
        past = jnp.stack(rows, axis=1)
        new = jnp.full((heads, SUBLANES, LANES), NEG_INF, F32)
        for j in range((steps - 1) // dil + 1):
            mask = (t_idx - c_idx == j * dil) & (t_idx < steps) & (c_idx < steps)
            new = jnp.where(jnp.asarray(mask)[None], b[:, j][:, None, None], new)
        tables.append(past.reshape(heads // 2, 2 * SUBLANES, p))
        new_tables.append(new.reshape(heads // 2, 2 * SUBLANES, LANES))
    return tables, jnp.stack(new_tables)


def _ffn_weights(w_in, conv_w, conv_b, w_down):
    return (w_in.astype(BF16), w_down.astype(BF16), conv_w, conv_b.reshape(DEPTH, 1, D_FF))


def _conv_tail_prompt(cs):
    batch = cs.shape[0]
    tail = cs[:, :, SUBLANES - (CONV_WIDTH - 1):, :]
    return jnp.transpose(tail, (0, 2, 1, 3)).reshape(batch, CONV_WIDTH - 1, D_FF)


def _conv_past_sample(state):
    nb = state.shape[0]
    s = state.reshape(nb, CONV_WIDTH - 1, N_FF_CHUNKS, FF_CHUNK)
    return jnp.transpose(s, (2, 1, 0, 3)).reshape(N_FF_CHUNKS, (CONV_WIDTH - 1) * nb, FF_CHUNK)


def _conv_tail_sample(cs, nb):
    s = cs.reshape(N_FF_CHUNKS, CONV_WIDTH - 1, nb, FF_CHUNK)
    return jnp.transpose(s, (2, 1, 0, 3)).reshape(nb, CONV_WIDTH - 1, D_FF)


def kernel(x_prompt, x_sample, state_pool, cache_win_g1, cache_win_g2, cache_win_g3, state_gla, state_ffn_conv,
           c_prompt, c_sample, w_ada, b_ada, norm_gain, final_gain, rel_bias, pool_w, pool_scale,
           attn_w_in, attn_w_out, gla_w_in, gla_w_gate_up, gla_b_gate, gla_norm_gain, gla_w_out,
           ffn_w_in, ffn_conv_w, ffn_conv_b, ffn_w_down):
    batch, seq, _ = x_prompt.shape
    nb, steps, _ = x_sample.shape
    caches = (cache_win_g1, cache_win_g2, cache_win_g3)

    mod_p, mod_s = _modulation(c_prompt, c_sample, w_ada, b_ada)
    fgain = final_gain.reshape(1, D_MODEL)

    xp = x_prompt.reshape(batch * seq, D_MODEL)
    xs = jnp.transpose(x_sample, (1, 0, 2)).reshape(steps * nb, D_MODEL)

    pool_p, pool_s, gla_p, gla_s, conv_p, conv_s = [], [], [], [], [], []
    win_p, win_s = None, None
    ffn_stack = _ffn_weights(ffn_w_in, ffn_conv_w, ffn_conv_b, ffn_w_down)

    for i in range(DEPTH):
        kind, j = i % 3, i // 3
        last = i == DEPTH - 1
        ffn = (i, ffn_stack)
        conv_past = _conv_past_sample(state_ffn_conv[i])
        gains = norm_gain[i]
        if kind == 0:
            mix_w = (pool_w[j].astype(BF16), pool_scale[j].reshape(1, D_MODEL))
            xp, cs, pst = _layer_prompt("pool", last, xp, mod_p[i], gains, fgain, mix_w, ffn)
            pool_p.append(pst[:, POOL_CARRY_ROWS - POOL_STATE_ROWS:])
            past = jnp.transpose(state_pool[j], (1, 0, 2))
            xs, css, psts = _layer_sample("pool", last, xs, mod_s[i], gains, fgain, mix_w + (past,), ffn, conv_past)
            pool_s.append(jnp.transpose(psts, (1, 0, 2)))
        elif kind == 1:
            w = attn_w_in[j]
            w3 = jnp.stack([jnp.concatenate([w[:, s * ATTN_INNER + g * GROUP_WIDTH:
                                                s * ATTN_INNER + (g + 1) * GROUP_WIDTH] for s in range(3)], axis=1)
                            for g in range(N_GROUPS)]).astype(BF16)
            gb = _group_bias(rel_bias)
            wo = attn_w_out[j].astype(BF16)
            qkvp, *kv_t = _qkv_prompt(xp.reshape(batch, seq, D_MODEL), mod_p[i], gains, w3)
            o_all = _attn_prompt(qkvp, _prompt_bias_rows(gb))
            win_p = [jnp.transpose(t.reshape(batch, 2, HEADS_PER_GROUP, HEAD_DIM, t.shape[-1]), (0, 4, 1, 2, 3))[None]
                     for t in kv_t]
            xp, cs = _layer_prompt("proj", last, xp, mod_p[i], gains, fgain,
                                   (o_all.reshape(batch * seq, ATTN_INNER), wo), ffn)
            qkv_s = _proj_sample(xs, mod_s[i], gains, w.astype(BF16))
            q6 = qkv_s.reshape(steps, nb, 3, N_GROUPS, HEADS_PER_GROUP, HEAD_DIM)
            q_s = jnp.pad(jnp.transpose(q6[:, :, 0], (1, 2, 3, 0, 4)),
                          ((0, 0), (0, 0), (0, 0), (0, SUBLANES - steps), (0, 0)))
            q_s = q_s.reshape(nb, N_GROUPS, HEADS_PER_GROUP // 2, 2, SUBLANES, HEAD_DIM)
            zeros = jnp.zeros_like(q_s[:, :, :, 0])
            q_s = jnp.stack([jnp.concatenate([q_s[:, :, :, 0], zeros], axis=-1),
                             jnp.concatenate([zeros, q_s[:, :, :, 1]], axis=-1)], axis=3)
            q_s = q_s.reshape(nb, N_GROUPS, HEADS_PER_GROUP // 2, 2 * SUBLANES, LANES)
            kv_new = jnp.transpose(q6[:, :, 1:], (1, 3, 0, 2, 4, 5)).reshape(nb, N_GROUPS, steps, 2 * GROUP_WIDTH)
            kv_new = jnp.pad(kv_new, ((0, 0), (0, 0), (0, SUBLANES - steps), (0, 0)))
            past_rows = [c.shape[2] for c in caches]
            cache_t = [jnp.transpose(c[j], (0, 2, 3, 4, 1)).reshape(nb, 2 * HEADS_PER_GROUP, HEAD_DIM, p)
                       for c, p in zip(caches, past_rows)]
            bias_past, bias_new = _sample_bias_tables(gb, steps, past_rows)
            o8, *new_caches = _attn_sample(steps, q_s, kv_new, cache_t, bias_past, bias_new)
            a_s = jnp.transpose(o8[:, :steps], (1, 0, 2)).reshape(steps * nb, ATTN_INNER)
            win_s = [jnp.transpose(c.reshape(nb, 2, HEADS_PER_GROUP, HEAD_DIM, p), (0, 4, 1, 2, 3))[None]
                     for c, p in zip(new_caches, past_rows)]
            xs, css = _layer_sample("proj", last, xs, mod_s[i], gains, fgain, (a_s.astype(BF16), wo), ffn, conv_past)
        else:
            w = gla_w_in[j]
            n_main = 2 * GLA_QK + 2 * GLA_V
            weights = (w[:, :n_main].astype(BF16),
                       jnp.pad(w[:, n_main:], ((0, 0), (0, LANES - GATE_RANK))).astype(BF16),
                       jnp.pad(gla_w_gate_up[j], ((0, LANES - GATE_RANK), (0, 0))).astype(BF16),
                       gla_b_gate[j].reshape(1, GLA_QK))
            gain = gla_norm_gain[j].reshape(1, GLA_V)
            wo = gla_w_out[j].astype(BF16)
            q, k, v, r, la = _gla_proj(xp, mod_p[i], gains, weights, None, PROMPT_ROW_TILE)
            shp = lambda a: a.reshape(batch, seq, a.shape[-1])
            a_p, s_p = _gla_prompt(shp(q), shp(k), shp(v), shp(r), shp(la), gain)
            gla_p.append(s_p)
            xp, cs = _layer_prompt("proj", last, xp, mod_p[i], gains, fgain,
                                   (a_p.reshape(batch * seq, GLA_V), wo), ffn)
            outs = _gla_proj(xs, mod_s[i], gains, weights, nb, steps * nb)

            def per_seq(a):
                a = jnp.transpose(a.reshape(steps, nb, a.shape[-1]), (1, 0, 2))
                return jnp.pad(a, ((0, 0), (0, SAMPLE_DEC_PAD - steps), (0, 0)))

            qs, ks, vs, rs, las = (per_seq(a) for a in outs)
            a16, s_s = _gla_sample(qs, ks, vs, rs, las, gain, state_gla[j])
            gla_s.append(s_s)
            a_s = jnp.transpose(a16[:, :steps], (1, 0, 2)).reshape(steps * nb, GLA_V).astype(BF16)
            xs, css = _layer_sample("proj", last, xs, mod_s[i], gains, fgain, (a_s, wo), ffn, conv_past)
        conv_p.append(_conv_tail_prompt(cs))
        conv_s.append(_conv_tail_sample(css, nb))

    y_prompt = xp.reshape(batch, seq, D_MODEL)
    y_sample = jnp.transpose(xs.reshape(steps, nb, D_MODEL), (1, 0, 2))
    return (y_prompt, y_sample, jnp.stack(pool_p), jnp.stack(pool_s),
            win_p[0], win_s[0], win_p[1], win_s[1], win_p[2], win_s[2],
            jnp.stack(gla_p), jnp.stack(gla_s), jnp.stack(conv_p), jnp.stack(conv_s))
```

```python
import functools
import math

import numpy as np
import jax
import jax.numpy as jnp
from jax import lax
from jax.experimental import pallas as pl
from jax.experimental.pallas import tpu as pltpu

F32 = jnp.float32
BF16 = jnp.bfloat16

D_MODEL = 1024
DEPTH = 4
N_MOD = 6
EPS = 1e-6
NEG_INF = -1e30
POOL_WINDOWS = (2, 4, 8, 16)
POOL_GROUP_DIM = D_MODEL // len(POOL_WINDOWS)
POOL_STATE_ROWS = max(POOL_WINDOWS) - 1
POOL_CARRY_ROWS = 16
DILATED_GROUPS = ((128, 1), (512, 4), (2048, 16))
N_GROUPS = len(DILATED_GROUPS)
HEADS_PER_GROUP = 4
HEAD_DIM = 64
GROUP_WIDTH = HEADS_PER_GROUP * HEAD_DIM
ATTN_INNER = N_GROUPS * GROUP_WIDTH
KEYS_PER_QUERY = 129
QUERY_BLOCK = 128
NUM_BUCKETS = 32
MAX_DISTANCE = 2048
GLA_HEADS = 4
GLA_DK = 128
GLA_DV = 256
GLA_QK = GLA_HEADS * GLA_DK
GLA_V = GLA_HEADS * GLA_DV
GATE_RANK = 16
GATE_TAU = 16.0
GLA_CHUNK = 128
D_FF = 2816
CONV_WIDTH = 3

LANES = 128
SUBLANES = 8
FF_CHUNK = 256
CHEAP_STRIDE = 4
N_FF_CHUNKS = D_FF // FF_CHUNK
VMEM_LIMIT_BYTES = 56 * 1024 * 1024
PROMPT_ROW_TILE = 1024
MOD_SLOTS_PER_STEP = 2
GLA_TIME_TILE = 1024
ATTN_BLOCK_UNROLL = 16
GLA_CHUNK_UNROLL = 4
GLA_SAMPLE_UNROLL = 8
ATTN_SAMPLE_SEQS = 2
SAMPLE_DEC_PAD = 16


def _params(*semantics):
    return pltpu.CompilerParams(dimension_semantics=semantics, vmem_limit_bytes=VMEM_LIMIT_BYTES)


def _resident(shape):
    nd = len(shape)
    return pl.BlockSpec(shape, lambda *_: (0,) * nd, pipeline_mode=pl.Buffered(1))


def _dot(a, b):
    return jnp.dot(a, b, preferred_element_type=F32)


def _dot_nt(a, b):
    return lax.dot_general(a, b, (((1,), (1,)), ((), ())), preferred_element_type=F32)


def _dot_tn(a, b):
    return lax.dot_general(a, b, (((0,), (0,)), ((), ())), preferred_element_type=F32)


def _rms(x):
    return x * lax.rsqrt(jnp.mean(x * x, axis=-1, keepdims=True) + EPS)


def _bcast_rows(v, y, nb):
    if nb is None:
        return v * y
    rows, width = y.shape
    return (y.reshape(rows // nb, nb, width) * v[None]).reshape(rows, width)


def _norm_mod(x, gain, shift, scale, nb):
    y = _rms(x) * gain
    if nb is None:
        return y * (1.0 + scale) + shift
    rows, width = y.shape
    y3 = y.reshape(rows // nb, nb, width)
    return (y3 * (1.0 + scale)[None] + shift[None]).reshape(rows, width)


def _prompt_mod(mod_ref, b):
    return [mod_ref[k, pl.ds(b, 1), :] for k in range(N_MOD)]


def _gelu(x):
    return 0.5 * x * (1.0 + lax.erf(x * (1.0 / math.sqrt(2.0))))


def _silu(x):
    return x * jax.nn.sigmoid(x)


def _split_bf16(a):
    hi = a.astype(BF16)
    lo = (a - hi.astype(F32)).astype(BF16)
    return hi, lo


def _mod_kernel(cp_ref, cs_ref, w_ref, b_ref, op_ref, os_ref):
    batch = op_ref.shape[2]
    pad = cp_ref.shape[0]
    rows = pad + cs_ref.shape[0]
    w_hi, w_lo = _split_bf16(w_ref[0])
    a_hi, a_lo = _split_bf16(_silu(jnp.concatenate([cp_ref[...], cs_ref[...]], axis=0)))
    both = _dot(jnp.concatenate([a_hi, a_lo], axis=0), w_hi)
    out = both[0:rows] + both[rows:2 * rows] + _dot(a_hi, w_lo) + b_ref[0]
    for k in range(MOD_SLOTS_PER_STEP):
        cols = slice(k * D_MODEL, (k + 1) * D_MODEL)
        op_ref[0, k] = out[0:batch, cols]
        os_ref[0, k] = out[pad:rows, cols]


def _modulation(c_prompt, c_sample, w_ada, b_ada):
    batch, nb = c_prompt.shape[0], c_sample.shape[0]
    bf16_rows = 2 * SUBLANES
    c_prompt = jnp.pad(c_prompt, ((0, -batch % bf16_rows), (0, 0)))
    return pl.pallas_call(
        _mod_kernel,
        grid=(DEPTH, N_MOD // MOD_SLOTS_PER_STEP),
        in_specs=[pl.BlockSpec(c_prompt.shape, lambda l, m: (0, 0)),
                  pl.BlockSpec((nb, D_MODEL), lambda l, m: (0, 0)),
                  pl.BlockSpec((1, D_MODEL, MOD_SLOTS_PER_STEP * D_MODEL), lambda l, m: (l, 0, m)),
                  pl.BlockSpec((1, 1, MOD_SLOTS_PER_STEP * D_MODEL), lambda l, m: (l, 0, m))],
        out_specs=[pl.BlockSpec((1, MOD_SLOTS_PER_STEP, batch, D_MODEL), lambda l, m: (l, m, 0, 0)),
                   pl.BlockSpec((1, MOD_SLOTS_PER_STEP, nb, D_MODEL), lambda l, m: (l, m, 0, 0))],
        out_shape=[jax.ShapeDtypeStruct((DEPTH, N_MOD, batch, D_MODEL), F32),
                   jax.ShapeDtypeStruct((DEPTH, N_MOD, nb, D_MODEL), F32)],
        compiler_params=_params("parallel", "parallel"),
        name="adaln_mod",
    )(c_prompt, c_sample, w_ada, b_ada.reshape(DEPTH, 1, N_MOD * D_MODEL))


def _ffn_chunk_math(g, g_m1, g_m2, u, cw, cb):
    gc = cw[2:3] * g + cw[1:2] * g_m1 + cw[0:1] * g_m2 + cb
    return (_gelu(gc) * u).astype(BF16)


def _ffn_chunk_weights(win_ref, wd_ref, cw_ref, cb_ref, j):
    cols = slice(j * FF_CHUNK, (j + 1) * FF_CHUNK)
    ucols = slice(D_FF + j * FF_CHUNK, D_FF + (j + 1) * FF_CHUNK)
    return win_ref[0, :, cols], win_ref[0, :, ucols], wd_ref[0, cols, :], cw_ref[0, :, cols], cb_ref[0, :, cols]


def _layer_resident(array, layer):
    nd = array.ndim
    return pl.BlockSpec((1,) + array.shape[1:], lambda *_: (layer,) + (0,) * (nd - 1),
                        pipeline_mode=pl.Buffered(1))


def _layer_prompt_kernel(mixer, last, tm, tiles_per_seq, *refs):
    refs = list(refs)
    x_ref, mod_ref, gains_ref, fg_ref = refs[:4]
    refs = refs[4:]
    if mixer == "pool":
        pw_ref, ps_ref = refs[:2]
    else:
        a_ref, wp_ref = refs[:2]
    win_ref, wd_ref, cw_ref, cb_ref = refs[2:6]
    refs = refs[6:]
    if mixer == "pool":
        y_ref, cs_ref, pst_ref, h2_ref, act_ref, gext_ref, cc_ref, hext_ref = refs
    else:
        y_ref, cs_ref, h2_ref, act_ref, gext_ref, cc_ref = refs

    i = pl.program_id(0)
    tile_in_seq = i % tiles_per_seq

    @pl.when(tile_in_seq == 0)
    def _():
        cc_ref[...] = jnp.zeros_like(cc_ref)
        if mixer == "pool":
            hext_ref[0:POOL_CARRY_ROWS, :] = jnp.zeros((POOL_CARRY_ROWS, D_MODEL), F32)

    x = x_ref[...]
    m = _prompt_mod(mod_ref, i // tiles_per_seq)
    gains = gains_ref[...]

    if mixer == "pool":
        h = _norm_mod(x, gains[0:1], m[0], m[1], None)
        hext_ref[POOL_CARRY_ROWS:, :] = h
        pos = tile_in_seq * tm + lax.broadcasted_iota(jnp.int32, (tm, 1), 0)
        parts = []
        for g, w in enumerate(POOL_WINDOWS):
            cols = slice(g * POOL_GROUP_DIM, (g + 1) * POOL_GROUP_DIM)
            s = hext_ref[:, cols]
            span = 1
            while span < w:
                s = s + pltpu.roll(s, span, 0)
                span *= 2
            inv_count = 1.0 / jnp.minimum(pos + 1, w).astype(F32)
            d = s[POOL_CARRY_ROWS:, :] * inv_count - h[:, cols]
            parts.append(_dot(d.astype(BF16), pw_ref[g]))
        mix = jnp.concatenate(parts, axis=-1) * ps_ref[...]
        tail = hext_ref[pl.ds(tm, POOL_CARRY_ROWS), :]
        hext_ref[0:POOL_CARRY_ROWS, :] = tail
        pst_ref[0] = tail
    else:
        mix = _dot(a_ref[...], wp_ref[...])

    x1 = x + m[2] * mix
    y_ref[...] = x1
    h2_ref[...] = _norm_mod(x1, gains[1:2], m[3], m[4], None).astype(BF16)

    def up_proj(j):
        wg, wu, _, _, _ = _ffn_chunk_weights(win_ref, wd_ref, cw_ref, cb_ref, j)
        h2 = h2_ref[...]
        return _dot(h2, wg), _dot(h2, wu)

    ahead = up_proj(0)
    for j in range(N_FF_CHUNKS):
        g, u = ahead
        if j + 1 < N_FF_CHUNKS:
            ahead = up_proj(j + 1)
        _, _, _, cw, cb = _ffn_chunk_weights(win_ref, wd_ref, cw_ref, cb_ref, j)
        gx = gext_ref.at[j % 2]
        gx[0:SUBLANES, :] = cc_ref[j]
        gx[SUBLANES:, :] = g
        act_ref[:, j * FF_CHUNK:(j + 1) * FF_CHUNK] = _ffn_chunk_math(
            g, gx[pl.ds(SUBLANES - 1, tm), :], gx[pl.ds(SUBLANES - 2, tm), :], u, cw, cb)
        tail = g[tm - SUBLANES:tm, :]
        cc_ref[j] = tail
        cs_ref[0, j] = tail
    xo = y_ref[...] + m[5] * _dot(act_ref[...], wd_ref[0])
    if last:
        xo = _rms(xo) * fg_ref[...]
    y_ref[...] = xo


def _layer_prompt(mixer, last, x, mod, gains, fgain, mix_args, ffn):
    n = x.shape[0]
    batch = mod.shape[1]
    seq = n // batch
    tm = PROMPT_ROW_TILE
    tps = seq // tm
    in_specs = [pl.BlockSpec((tm, D_MODEL), lambda i: (i, 0)),
                _resident(mod.shape), _resident((2, D_MODEL)), _resident((1, D_MODEL))]
    if mixer == "pool":
        pw, ps = mix_args
        in_specs += [_resident(pw.shape), _resident(ps.shape)]
    else:
        a, wp = mix_args
        in_specs += [pl.BlockSpec((tm, a.shape[1]), lambda i: (i, 0)), _resident(wp.shape)]
    layer, ffn_w = ffn
    in_specs += [_layer_resident(w, layer) for w in ffn_w]
    out_shape = [jax.ShapeDtypeStruct((n, D_MODEL), F32),
                 jax.ShapeDtypeStruct((batch, N_FF_CHUNKS, SUBLANES, FF_CHUNK), F32)]
    out_specs = [pl.BlockSpec((tm, D_MODEL), lambda i: (i, 0)),
                 pl.BlockSpec((1, N_FF_CHUNKS, SUBLANES, FF_CHUNK), lambda i: (i // tps, 0, 0, 0))]
    scratch = [pltpu.VMEM((tm, D_MODEL), BF16), pltpu.VMEM((tm, D_FF), BF16),
               pltpu.VMEM((2, tm + SUBLANES, FF_CHUNK), F32), pltpu.VMEM((N_FF_CHUNKS, SUBLANES, FF_CHUNK), F32)]
    if mixer == "pool":
        out_shape.append(jax.ShapeDtypeStruct((batch, POOL_CARRY_ROWS, D_MODEL), F32))
        out_specs.append(pl.BlockSpec((1, POOL_CARRY_ROWS, D_MODEL), lambda i: (i // tps, 0, 0)))
        scratch.append(pltpu.VMEM((tm + POOL_CARRY_ROWS, D_MODEL), F32))
    return pl.pallas_call(
        functools.partial(_layer_prompt_kernel, mixer, last, tm, tps),
        grid=(n // tm,), in_specs=in_specs, out_specs=out_specs, out_shape=out_shape,
        scratch_shapes=scratch, compiler_params=_params("arbitrary"),
        name=f"layer_prompt_{mixer}",
    )(x, mod, gains, fgain, *mix_args, *ffn_w)


def _layer_sample_kernel(mixer, last, nb, steps, *refs):
    refs = list(refs)
    x_ref, mod_ref, gains_ref, fg_ref = refs[:4]
    refs = refs[4:]
    if mixer == "pool":
        pw_ref, ps_ref, ppast_ref = refs[:3]
        refs = refs[3:]
    else:
        a_ref, wp_ref = refs[:2]
        refs = refs[2:]
    win_ref, wd_ref, cw_ref, cb_ref, cpast_ref = refs[:5]
    refs = refs[5:]
    if mixer == "pool":
        y_ref, cs_ref, pst_ref, h2_ref, act_ref = refs
    else:
        y_ref, cs_ref, h2_ref, act_ref = refs
    rows = steps * nb

    x = x_ref[...]
    gains = gains_ref[...]
    if mixer == "pool":
        h = _norm_mod(x, gains[0:1], mod_ref[0], mod_ref[1], nb)
        new = [h[t * nb:(t + 1) * nb, :] for t in range(steps)]

        def u_rows(p, cols):
            if p < POOL_STATE_ROWS:
                return ppast_ref[p, :, cols]
            return new[p - POOL_STATE_ROWS][:, cols]

        parts = []
        for g, w in enumerate(POOL_WINDOWS):
            cols = slice(g * POOL_GROUP_DIM, (g + 1) * POOL_GROUP_DIM)
            ds = []
            for t in range(steps):
                s = u_rows(POOL_STATE_ROWS + t, cols)
                for k in range(1, w):
                    s = s + u_rows(POOL_STATE_ROWS + t - k, cols)
                ds.append(s * (1.0 / w) - new[t][:, cols])
            parts.append(_dot(jnp.concatenate(ds, axis=0).astype(BF16), pw_ref[g]))
        mix = jnp.concatenate(parts, axis=-1) * ps_ref[...]
        full = slice(0, D_MODEL)
        for p in range(POOL_STATE_ROWS):
            pst_ref[p] = u_rows(p + steps, full)
    else:
        mix = _dot(a_ref[...], wp_ref[...])

    x1 = x + _bcast_rows(mod_ref[2], mix, nb)
    y_ref[...] = x1
    h2_ref[...] = _norm_mod(x1, gains[1:2], mod_ref[3], mod_ref[4], nb).astype(BF16)
    past_rows = (CONV_WIDTH - 1) * nb

    for j in range(N_FF_CHUNKS):
        h2 = h2_ref[...]
        wg, wu, _, cw, cb = _ffn_chunk_weights(win_ref, wd_ref, cw_ref, cb_ref, j)
        g = _dot(h2, wg)
        u = _dot(h2, wu)
        gall = jnp.concatenate([cpast_ref[j], g], axis=0)
        act_ref[:, j * FF_CHUNK:(j + 1) * FF_CHUNK] = _ffn_chunk_math(
            g, gall[nb:nb + rows, :], gall[0:rows, :], u, cw, cb)
        cs_ref[j] = gall[rows:rows + past_rows, :]
    xo = y_ref[...] + _bcast_rows(mod_ref[5], _dot(act_ref[...], wd_ref[0]), nb)
    if last:
        xo = _rms(xo) * fg_ref[...]
    y_ref[...] = xo


def _layer_sample(mixer, last, x, mod, gains, fgain, mix_args, ffn, conv_past):
    rows = x.shape[0]
    nb = mod.shape[1]
    steps = rows // nb
    layer, ffn_w = ffn
    head = [x, mod, gains, fgain, *mix_args]
    args = [*head, *ffn_w, conv_past]
    out_shape = [jax.ShapeDtypeStruct((rows, D_MODEL), F32),
                 jax.ShapeDtypeStruct(conv_past.shape, F32)]
    if mixer == "pool":
        out_shape.append(jax.ShapeDtypeStruct((POOL_STATE_ROWS, nb, D_MODEL), F32))
    return pl.pallas_call(
        functools.partial(_layer_sample_kernel, mixer, last, nb, steps),
        grid=(1,),
        in_specs=[_resident(a.shape) for a in head] + [_layer_resident(w, layer) for w in ffn_w]
                 + [_resident(conv_past.shape)],
        out_specs=[pl.BlockSpec(s.shape, functools.partial(lambda nd, i: (0,) * nd, len(s.shape))) for s in out_shape],
        out_shape=out_shape,
        scratch_shapes=[pltpu.VMEM((rows, D_MODEL), BF16), pltpu.VMEM((rows, D_FF), BF16)],
        compiler_params=_params("arbitrary"),
        name=f"layer_sample_{mixer}",
    )(*args)


def _qkv_prompt_kernel(seq, x_ref, mod_ref, gains_ref, w_ref, qkvp_ref, kt1_ref, kt2_ref, kt3_ref, h_ref, slab_ref, tmp_ref):
    g = pl.program_id(1)
    kt_refs = (kt1_ref, kt2_ref, kt3_ref)

    @pl.when(g == 0)
    def _():
        m = _prompt_mod(mod_ref, pl.program_id(0))
        h_ref[...] = _norm_mod(x_ref[0], gains_ref[0:1, :], m[0], m[1], None).astype(BF16)

    h = h_ref[...]
    for c in range(3):
        r = _dot(h, w_ref[0, :, c * GROUP_WIDTH:(c + 1) * GROUP_WIDTH])
        if c == 0:
            r = r * (HEAD_DIM ** -0.5)
        slab_ref[2 * c] = r[:, 0:LANES]
        slab_ref[2 * c + 1] = r[:, LANES:2 * LANES]

    qkvp_ref[0, 0, 0:QUERY_BLOCK, :] = jnp.zeros((QUERY_BLOCK, 3 * GROUP_WIDTH), BF16)
    for gi, (_, dil) in enumerate(DILATED_GROUPS):
        @pl.when(g == gi)
        def _(dil=dil):
            per_class = seq // dil
            assert dil in (1, CHEAP_STRIDE, CHEAP_STRIDE ** 2)
            for s in range(6):
                if dil > CHEAP_STRIDE:
                    part = seq // CHEAP_STRIDE
                    for r in range(CHEAP_STRIDE):
                        tmp_ref[r * part:(r + 1) * part, :] = slab_ref[s, pl.ds(r, part, stride=CHEAP_STRIDE), :]
                for rho in range(dil):
                    if dil == 1:
                        v = slab_ref[s]
                    elif dil == CHEAP_STRIDE:
                        v = slab_ref[s, pl.ds(rho, per_class, stride=dil), :]
                    else:
                        r, q = rho % CHEAP_STRIDE, rho // CHEAP_STRIDE
                        v = tmp_ref[pl.ds(r * part + q, per_class, stride=dil // CHEAP_STRIDE), :]
                    r0 = QUERY_BLOCK + rho * per_class
                    qkvp_ref[0, 0, r0:r0 + per_class, s * LANES:(s + 1) * LANES] = v.astype(BF16)

    for gi, kt_ref in enumerate(kt_refs):
        @pl.when(g == gi)
        def _(kt_ref=kt_ref):
            keep = kt_ref.shape[-1]
            for s in range(2, 6):
                tile_t = slab_ref[s, seq - keep:seq, :].T
                for r in range(2):
                    kt_ref[0, 2 * (s - 2) + r] = tile_t[r * HEAD_DIM:(r + 1) * HEAD_DIM, :]


def _qkv_prompt(x3, mod, gains, w3):
    batch, seq, _ = x3.shape
    width = 3 * GROUP_WIDTH
    keeps = [min(window, seq) for window, _ in DILATED_GROUPS]
    return pl.pallas_call(
        functools.partial(_qkv_prompt_kernel, seq),
        grid=(batch, N_GROUPS),
        in_specs=[pl.BlockSpec((1, seq, D_MODEL), lambda b, g: (b, 0, 0)),
                  pl.BlockSpec(mod.shape, lambda b, g: (0, 0, 0)),
                  pl.BlockSpec((2, D_MODEL), lambda b, g: (0, 0)),
                  pl.BlockSpec((1, D_MODEL, width), lambda b, g: (g, 0, 0))],
        out_specs=[pl.BlockSpec((1, 1, seq + QUERY_BLOCK, width), lambda b, g: (b, g, 0, 0))]
                  + [pl.BlockSpec((1, 2 * HEADS_PER_GROUP, HEAD_DIM, keep), lambda b, g: (b, 0, 0, 0)) for keep in keeps],
        out_shape=[jax.ShapeDtypeStruct((batch, N_GROUPS, seq + QUERY_BLOCK, width), BF16)]
                  + [jax.ShapeDtypeStruct((batch, 2 * HEADS_PER_GROUP, HEAD_DIM, keep), F32) for keep in keeps],
        scratch_shapes=[pltpu.VMEM((seq, D_MODEL), BF16), pltpu.VMEM((6, seq, LANES), F32),
                        pltpu.VMEM((seq, LANES), F32)],
        compiler_params=_params("arbitrary", "arbitrary"),
        name="qkv_prompt",
    )(x3, mod, gains, w3)


def _head_lane_mask(rows, h):
    lane = lax.broadcasted_iota(jnp.int32, (rows, GROUP_WIDTH), 1)
    return (lane >= h * HEAD_DIM) & (lane < (h + 1) * HEAD_DIM)


def _attn_block(q, k, v, bias_ref, cols):
    masks = [_head_lane_mask(QUERY_BLOCK, h) for h in range(HEADS_PER_GROUP)]
    qs = jnp.concatenate([jnp.where(hm, q, jnp.zeros_like(q)) for hm in masks], axis=0)
    s = _dot_nt(qs, k) + bias_ref[:, cols]
    m = jnp.max(s, axis=-1, keepdims=True)
    p = jnp.exp(s - m)
    l = jnp.sum(p, axis=-1, keepdims=True)
    pv = _dot(p.astype(BF16), v) * (1.0 / l)
    lse_rows = jnp.broadcast_to(m + jnp.log(l), pv.shape)
    o = pv[0:QUERY_BLOCK]
    lse = lse_rows[0:QUERY_BLOCK]
    for h in range(1, HEADS_PER_GROUP):
        rows = slice(h * QUERY_BLOCK, (h + 1) * QUERY_BLOCK)
        o = jnp.where(masks[h], pv[rows], o)
        lse = jnp.where(masks[h], lse_rows[rows], lse)
    return o, lse


def _attn_prompt_kernel(seq, qkv_ref, brow_ref, o_ref, os_ref, ls_ref, bias_ref):
    @pl.when(pl.program_id(0) == 0)
    def _():
        left = lax.broadcasted_iota(jnp.int32, (QUERY_BLOCK, 2 * QUERY_BLOCK), 1) < QUERY_BLOCK
        for g in range(N_GROUPS):
            for h in range(HEADS_PER_GROUP):
                base = jnp.broadcast_to(brow_ref[g, h], (QUERY_BLOCK, 2 * QUERY_BLOCK))
                band = pltpu.roll(base, 0, 1, stride=1, stride_axis=0)
                rows = slice(h * QUERY_BLOCK, (h + 1) * QUERY_BLOCK)
                bias_ref[g, 0, rows, :] = band
                bias_ref[g, 1, rows, :] = jnp.where(left, NEG_INF, band)

    qc = slice(0, GROUP_WIDTH)
    kc = slice(GROUP_WIDTH, 2 * GROUP_WIDTH)
    vc = slice(2 * GROUP_WIDTH, 3 * GROUP_WIDTH)
    n_blocks = seq // QUERY_BLOCK
    for g, (_, dil) in enumerate(DILATED_GROUPS):
        blocks_per_class = n_blocks // dil

        def do_block(blk, carry, g=g, dil=dil, blocks_per_class=blocks_per_class):
            r0 = pl.multiple_of(blk * QUERY_BLOCK, QUERY_BLOCK)
            cur = pl.ds(r0 + QUERY_BLOCK, QUERY_BLOCK)
            q = qkv_ref[0, g, cur, qc]
            rho = blk // blocks_per_class
            in_class = blk % blocks_per_class
            if blocks_per_class == 1:
                o, lse = _attn_block(q, qkv_ref[0, g, cur, kc], qkv_ref[0, g, cur, vc],
                                     bias_ref.at[g, 0], slice(QUERY_BLOCK, 2 * QUERY_BLOCK))
            else:
                both = pl.ds(r0, 2 * QUERY_BLOCK)
                first = jnp.asarray(in_class == 0, jnp.int32)
                o, lse = _attn_block(q, qkv_ref[0, g, both, kc], qkv_ref[0, g, both, vc],
                                     bias_ref.at[g, first], slice(0, 2 * QUERY_BLOCK))
            start = in_class * (QUERY_BLOCK * dil) + rho
            for s in range(2):
                cols = slice(s * LANES, (s + 1) * LANES)
                if dil == 1:
                    os_ref[g, s, pl.ds(r0, QUERY_BLOCK), :] = o[:, cols]
                    ls_ref[g, s, pl.ds(r0, QUERY_BLOCK), :] = lse[:, cols]
                else:
                    os_ref[g, s, pl.ds(start, QUERY_BLOCK, stride=dil), :] = o[:, cols]
                    ls_ref[g, s, pl.ds(start, QUERY_BLOCK, stride=dil), :] = lse[:, cols]
            return carry

        lax.fori_loop(0, n_blocks, do_block, 0, unroll=ATTN_BLOCK_UNROLL)

    def merge(i, carry):
        r0 = pl.multiple_of(i * QUERY_BLOCK, QUERY_BLOCK)
        rows = pl.ds(r0, QUERY_BLOCK)
        for s in range(2):
            ls = [ls_ref[g, s, rows, :] for g in range(N_GROUPS)]
            mx = jnp.maximum(jnp.maximum(ls[0], ls[1]), ls[2])
            es = [jnp.exp(l - mx) for l in ls]
            inv = 1.0 / (es[0] + es[1] + es[2])
            for g in range(N_GROUPS):
                c0 = g * GROUP_WIDTH + s * LANES
                o_ref[0, rows, c0:c0 + LANES] = (os_ref[g, s, rows, :] * (es[g] * inv)).astype(BF16)
        return carry

    lax.fori_loop(0, seq // QUERY_BLOCK, merge, 0)


def _attn_prompt(qkvp, bias_rows):
    batch, _, padded, width = qkvp.shape
    seq = padded - QUERY_BLOCK
    return pl.pallas_call(
        functools.partial(_attn_prompt_kernel, seq),
        grid=(batch,),
        in_specs=[pl.BlockSpec((1, N_GROUPS, padded, width), lambda b: (b, 0, 0, 0)),
                  _resident(bias_rows.shape)],
        out_specs=pl.BlockSpec((1, seq, ATTN_INNER), lambda b: (b, 0, 0)),
        out_shape=jax.ShapeDtypeStruct((batch, seq, ATTN_INNER), BF16),
        scratch_shapes=[pltpu.VMEM((N_GROUPS, 2, seq, LANES), F32), pltpu.VMEM((N_GROUPS, 2, seq, LANES), F32),
                        pltpu.VMEM((N_GROUPS, 2, HEADS_PER_GROUP * QUERY_BLOCK, 2 * QUERY_BLOCK), F32)],
        compiler_params=_params("arbitrary"),
        name="attn_prompt",
    )(qkvp, bias_rows)


def _proj_sample_kernel(nb, x_ref, mod_ref, gains_ref, w_ref, o_ref):
    h = _norm_mod(x_ref[...], gains_ref[0:1, :], mod_ref[0], mod_ref[1], nb).astype(BF16)
    o_ref[...] = _dot(h, w_ref[...])


def _proj_sample(x, mod, gains, w):
    rows = x.shape[0]
    nb = mod.shape[1]
    args = [x, mod, gains, w]
    return pl.pallas_call(
        functools.partial(_proj_sample_kernel, nb),
        grid=(1,),
        in_specs=[_resident(a.shape) for a in args],
        out_specs=pl.BlockSpec((rows, w.shape[1]), lambda i: (0, 0)),
        out_shape=jax.ShapeDtypeStruct((rows, w.shape[1]), F32),
        compiler_params=_params("arbitrary"),
        name="qkv_sample",
    )(*args)


def _split3_bf16(a):
    hi = a.astype(BF16)
    r1 = a - hi.astype(F32)
    mid = r1.astype(BF16)
    lo = (r1 - mid.astype(F32)).astype(BF16)
    return hi, mid, lo


def _attn_sample_kernel(steps, q_ref, kvn_ref, c1_ref, c2_ref, c3_ref, t1_ref, t2_ref, t3_ref, bn_ref,
                        o_ref, n1_ref, n2_ref, n3_ref):
    for i in range(q_ref.shape[0]):
        _attn_sample_one(steps, i, q_ref, kvn_ref, ((c1_ref, t1_ref, n1_ref), (c2_ref, t2_ref, n2_ref),
                                                  (c3_ref, t3_ref, n3_ref)), bn_ref, o_ref)


def _attn_sample_one(steps, i, q_ref, kvn_ref, groups, bn_ref, o_ref):
    pairs = HEADS_PER_GROUP // 2
    outs, lses = [], []
    lane = lax.broadcasted_iota(jnp.int32, (SUBLANES, LANES), 1)
    row = lax.broadcasted_iota(jnp.int32, (SUBLANES, LANES), 0)
    own_lanes = lane < HEAD_DIM
    sel_head = jnp.where((lane == row) & (row < steps), 1.0, 0.0).astype(BF16)
    sel_tail = jnp.where((lane == row + (LANES - steps)) & (row < steps), 1.0, 0.0).astype(BF16)
    for g, (c_ref, t_ref, n_ref) in enumerate(groups):
        p_rows = c_ref.shape[-1]
        pieces = _split3_bf16(kvn_ref[i, g])
        new_head = sum(_dot_tn(x, sel_head) for x in pieces)
        new_tail = sum(_dot_tn(x, sel_tail) for x in pieces)
        o_pairs, l_pairs = [], []
        for j in range(pairs):
            q = (q_ref[i, g, j] * (HEAD_DIM ** -0.5)).astype(BF16)
            kt = c_ref[i, 2 * j:2 * j + 2].reshape(LANES, p_rows).astype(BF16)
            vt = c_ref[i, HEADS_PER_GROUP + 2 * j:HEADS_PER_GROUP + 2 * j + 2].reshape(LANES, p_rows).astype(BF16)
            k_new = new_head[j * LANES:(j + 1) * LANES].astype(BF16)
            v_new = new_head[(pairs + j) * LANES:(pairs + j + 1) * LANES].astype(BF16)
            s = _dot(q, kt) + t_ref[j]
            s_new = _dot(q, k_new) + bn_ref[g, j]
            m = jnp.maximum(jnp.max(s, axis=-1, keepdims=True), jnp.max(s_new, axis=-1, keepdims=True))
            p = jnp.exp(s - m)
            p_new = jnp.exp(s_new - m)
            l = jnp.sum(p, axis=-1, keepdims=True) + jnp.sum(p_new, axis=-1, keepdims=True)
            o = _dot_nt(p.astype(BF16), vt) + _dot_nt(p_new.astype(BF16), v_new)
            o_pairs.append(o * (1.0 / l))
            l_pairs.append(m + jnp.log(l))
            for idx in (2 * j, 2 * j + 1, HEADS_PER_GROUP + 2 * j, HEADS_PER_GROUP + 2 * j + 1):
                n_ref[i, idx] = pltpu.roll(c_ref[i, idx], p_rows - steps, 1)
                n_ref[i, idx, :, p_rows - steps:p_rows] = new_tail[idx * HEAD_DIM:(idx + 1) * HEAD_DIM, LANES - steps:LANES]
        outs.append(o_pairs)
        lses.append(l_pairs)
    for j in range(pairs):
        ls = [lses[g][j] for g in range(N_GROUPS)]
        mx = jnp.maximum(jnp.maximum(ls[0], ls[1]), ls[2])
        es = [jnp.exp(l - mx) for l in ls]
        inv = 1.0 / (es[0] + es[1] + es[2])
        for g in range(N_GROUPS):
            og = outs[g][j] * (es[g] * inv)
            c0 = g * GROUP_WIDTH + j * LANES
            o_ref[i, :, c0:c0 + LANES] = jnp.where(own_lanes, og[0:SUBLANES], og[SUBLANES:2 * SUBLANES])


def _attn_sample(steps, q, kv_new, caches, tables, bias_new):
    nb = q.shape[0]
    sb = math.gcd(nb, ATTN_SAMPLE_SEQS)
    in_specs = [pl.BlockSpec((sb,) + q.shape[1:], lambda b: (b, 0, 0, 0, 0)),
                pl.BlockSpec((sb,) + kv_new.shape[1:], lambda b: (b, 0, 0, 0))]
    cache_specs = [pl.BlockSpec((sb,) + c.shape[1:], lambda b: (b, 0, 0, 0)) for c in caches]
    in_specs += cache_specs + [_resident(t.shape) for t in tables] + [_resident(bias_new.shape)]
    return pl.pallas_call(
        functools.partial(_attn_sample_kernel, steps),
        grid=(nb // sb,), in_specs=in_specs,
        out_specs=[pl.BlockSpec((sb, SUBLANES, ATTN_INNER), lambda b: (b, 0, 0))] + cache_specs,
        out_shape=[jax.ShapeDtypeStruct((nb, SUBLANES, ATTN_INNER), F32)]
                  + [jax.ShapeDtypeStruct(c.shape, F32) for c in caches],
        compiler_params=_params("arbitrary"),
        name="attn_sample",
    )(q, kv_new, *caches, *tables, bias_new)


def _gla_proj_kernel(nb, tiles_per_seq, x_ref, mod_ref, gains_ref, w_ref, wgd_ref, wgu_ref, bg_ref,
                     q_ref, k_ref, v_ref, r_ref, la_ref):
    if nb is None:
        m = _prompt_mod(mod_ref, pl.program_id(0) // tiles_per_seq)
        shift, scale = m[0], m[1]
    else:
        shift, scale = mod_ref[0], mod_ref[1]
    h = _norm_mod(x_ref[...], gains_ref[0:1, :], shift, scale, nb).astype(BF16)
    q_ref[...] = _dot(h, w_ref[:, 0:GLA_QK]) * (GLA_DK ** -0.5)
    k_ref[...] = _dot(h, w_ref[:, GLA_QK:2 * GLA_QK])
    v_ref[...] = _dot(h, w_ref[:, 2 * GLA_QK:2 * GLA_QK + GLA_V]).astype(BF16)
    r_ref[...] = _dot(h, w_ref[:, 2 * GLA_QK + GLA_V:2 * GLA_QK + 2 * GLA_V])
    gd = _dot(h, wgd_ref[...])
    gate = _dot(gd.astype(BF16), wgu_ref[...]) + bg_ref[...]
    la_ref[...] = jax.nn.log_sigmoid(gate) * (1.0 / GATE_TAU)


def _gla_proj(x, mod, gains, weights, nb, tm):
    rows = x.shape[0]
    tps = None if nb is not None else rows // mod.shape[1] // tm
    widths = (GLA_QK, GLA_QK, GLA_V, GLA_V, GLA_QK)
    dtypes = (F32, F32, BF16, F32, F32)
    return pl.pallas_call(
        functools.partial(_gla_proj_kernel, nb, tps),
        grid=(rows // tm,),
        in_specs=[pl.BlockSpec((tm, D_MODEL), lambda i: (i, 0)), _resident(mod.shape), _resident((2, D_MODEL))]
                 + [_resident(w.shape) for w in weights],
        out_specs=[pl.BlockSpec((tm, w), lambda i: (i, 0)) for w in widths],
        out_shape=[jax.ShapeDtypeStruct((rows, w), dt) for w, dt in zip(widths, dtypes)],
        compiler_params=_params("arbitrary"),
        name="gla_proj",
    )(x, mod, gains, *weights)


def _cumsum_rows(g):
    rows = g.shape[0]
    row = lax.broadcasted_iota(jnp.int32, g.shape, 0)
    b = g
    shift = 1
    while shift < rows:
        b = b + jnp.where(row >= shift, pltpu.roll(b, shift, 0), 0.0)
        shift *= 2
    return b


def _gla_chunk(q, k, v, g, r, gain, mid, get_state, set_state):
    c = q.shape[0]
    b = _cumsum_rows(g)
    b_end = b[c - 1:c, :]
    b_mid = b[mid:mid + 1, :]
    q_in = (q * jnp.exp(b)).astype(BF16)
    q_rel = (q * jnp.exp(b - b_mid)).astype(BF16)
    k_rel = (k * jnp.exp(b_mid - b)).astype(BF16)
    k_out = (k * jnp.exp(b_end - b)).astype(BF16)
    decay_end = jnp.broadcast_to(jnp.exp(b_end), (SUBLANES, GLA_QK))
    ti = lax.broadcasted_iota(jnp.int32, (c, c), 0)
    si = lax.broadcasted_iota(jnp.int32, (c, c), 1)
    outs = []
    for h in range(GLA_HEADS):
        ks = slice(h * GLA_DK, (h + 1) * GLA_DK)
        vs = slice(h * GLA_DV, (h + 1) * GLA_DV)
        state = get_state(h)
        o = _dot(q_in[:, ks], state.astype(BF16))
        att = jnp.where(si <= ti, _dot_nt(q_rel[:, ks], k_rel[:, ks]), 0.0)
        o = o + _dot(att.astype(BF16), v[:, vs])
        decay_col = decay_end[:, ks].T[:, 0:1]
        set_state(h, decay_col * state + _dot_tn(k_out[:, ks], v[:, vs]))
        outs.append(_rms(o) * gain[:, vs] * _silu(r[:, vs]))
    return jnp.concatenate(outs, axis=-1)


def _gla_prompt_kernel(tt, q_ref, k_ref, v_ref, r_ref, la_ref, gain_ref, a_ref, so_ref, s_ref):
    t = pl.program_id(1)

    @pl.when(t == 0)
    def _():
        s_ref[...] = jnp.zeros_like(s_ref)

    gain = gain_ref[...]

    def get_state(h):
        return s_ref[h]

    def set_state(h, val):
        s_ref[h] = val

    def chunk(ci, carry):
        rows = pl.ds(pl.multiple_of(ci * GLA_CHUNK, GLA_CHUNK), GLA_CHUNK)
        a = _gla_chunk(q_ref[0, rows, :], k_ref[0, rows, :], v_ref[0, rows, :], la_ref[0, rows, :],
                       r_ref[0, rows, :], gain, GLA_CHUNK // 2, get_state, set_state)
        a_ref[0, rows, :] = a.astype(BF16)
        return carry

    lax.fori_loop(0, tt // GLA_CHUNK, chunk, 0, unroll=GLA_CHUNK_UNROLL)

    @pl.when(t == pl.num_programs(1) - 1)
    def _():
        so_ref[0] = s_ref[...]


def _gla_prompt(q, k, v, r, la, gain):
    batch, seq, _ = q.shape
    tt = GLA_TIME_TILE

    def spec(width):
        return pl.BlockSpec((1, tt, width), lambda b, t: (b, t, 0))

    return pl.pallas_call(
        functools.partial(_gla_prompt_kernel, tt),
        grid=(batch, seq // tt),
        in_specs=[spec(GLA_QK), spec(GLA_QK), spec(GLA_V), spec(GLA_V), spec(GLA_QK),
                  pl.BlockSpec((1, GLA_V), lambda b, t: (0, 0))],
        out_specs=[spec(GLA_V), pl.BlockSpec((1, GLA_HEADS, GLA_DK, GLA_DV), lambda b, t: (b, 0, 0, 0))],
        out_shape=[jax.ShapeDtypeStruct((batch, seq, GLA_V), BF16),
                   jax.ShapeDtypeStruct((batch, GLA_HEADS, GLA_DK, GLA_DV), F32)],
        scratch_shapes=[pltpu.VMEM((GLA_HEADS, GLA_DK, GLA_DV), F32)],
        compiler_params=_params("arbitrary", "arbitrary"),
        name="gla_prompt",
    )(q, k, v, r, la, gain)


def _gla_sample_kernel(sb, q_ref, k_ref, v_ref, r_ref, la_ref, gain_ref, s0_ref, a_ref, so_ref):
    gain = gain_ref[...]

    def seq_body(i, carry):
        def get_state(h):
            return s0_ref[i, h]

        def set_state(h, val):
            so_ref[i, h] = val

        a_ref[i] = _gla_chunk(q_ref[i], k_ref[i], v_ref[i], la_ref[i], r_ref[i], gain, 0, get_state, set_state)
        return carry

    lax.fori_loop(0, sb, seq_body, 0, unroll=GLA_SAMPLE_UNROLL)


def _gla_sample(q, k, v, r, la, gain, s0):
    nb, pad, _ = q.shape
    sb = math.gcd(nb, 8)

    def spec(width):
        return pl.BlockSpec((sb, pad, width), lambda i: (i, 0, 0))

    state_spec = pl.BlockSpec((sb, GLA_HEADS, GLA_DK, GLA_DV), lambda i: (i, 0, 0, 0))
    return pl.pallas_call(
        functools.partial(_gla_sample_kernel, sb),
        grid=(nb // sb,),
        in_specs=[spec(GLA_QK), spec(GLA_QK), spec(GLA_V), spec(GLA_V), spec(GLA_QK),
                  pl.BlockSpec((1, GLA_V), lambda i: (0, 0)), state_spec],
        out_specs=[spec(GLA_V), state_spec],
        out_shape=[jax.ShapeDtypeStruct((nb, pad, GLA_V), F32),
                   jax.ShapeDtypeStruct((nb, GLA_HEADS, GLA_DK, GLA_DV), F32)],
        compiler_params=_params("arbitrary"),
        name="gla_sample",
    )(q, k, v, r, la, gain, s0)


def _t5_bucket(dist):
    max_exact = NUM_BUCKETS // 2
    d_f = jnp.maximum(dist, 1).astype(F32)
    large = max_exact + (jnp.log(d_f / max_exact) / math.log(MAX_DISTANCE / max_exact)
                         * (NUM_BUCKETS - max_exact)).astype(jnp.int32)
    large = jnp.minimum(large, NUM_BUCKETS - 1)
    return jnp.where(dist < max_exact, dist, large)


def _group_bias(rel_bias):
    rows = []
    for g, (window, dil) in enumerate(DILATED_GROUPS):
        buckets = _t5_bucket(jnp.arange(window // dil + 1) * dil)
        rows.append(rel_bias[buckets][:, g * HEADS_PER_GROUP:(g + 1) * HEADS_PER_GROUP].T)
    return jnp.stack(rows)


def _prompt_bias_rows(gb):
    band = gb[:, :, ::-1]
    off = jnp.full(gb.shape[:2] + (2 * QUERY_BLOCK - KEYS_PER_QUERY,), NEG_INF, F32)
    return jnp.concatenate([band, off], axis=-1)[:, :, None, :]


def _sample_bias_tables(gb, steps, past_rows):
    tables, new_tables = [], []
    t_idx = np.arange(SUBLANES)[:, None]
    c_idx = np.arange(LANES)[None, :]
    for g, (window, dil) in enumerate(DILATED_GROUPS):
        p = past_rows[g]
        assert p == window == (KEYS_PER_QUERY - 1) * dil
        b = gb[g]
        heads = b.shape[0]
        row0 = b[:, :0:-1]
        if dil > 1:
            gaps = jnp.full((heads, KEYS_PER_QUERY - 1, dil - 1), NEG_INF, F32)
            row0 = jnp.concatenate([row0[:, :, None], gaps], axis=2).reshape(heads, p)
        rows = []
        for t in range(SUBLANES):
            if t < steps:
                rows.append(jnp.concatenate([jnp.full((heads, t), NEG_INF, F32), row0[:, :p - t]], axis=1))
            else:
                rows.append(jnp.full((heads, p), NEG_INF, F32))
        past = jnp.stack(rows, axis=1)
        new = jnp.full((heads, SUBLANES, LANES), NEG_INF, F32)
        for j in range((steps - 1) // dil + 1):
            mask = (t_idx - c_idx == j * dil) & (t_idx < steps) & (c_idx < steps)
            new = jnp.where(jnp.asarray(mask)[None], b[:, j][:, None, None], new)
        tables.append(past.reshape(heads // 2, 2 * SUBLANES, p))
        new_tables.append(new.reshape(heads // 2, 2 * SUBLANES, LANES))
    return tables, jnp.stack(new_tables)


def _ffn_weights(w_in, conv_w, conv_b, w_down):
    return (w_in.astype(BF16), w_down.astype(BF16), conv_w, conv_b.reshape(DEPTH, 1, D_FF))


def _conv_tail_prompt(cs):
    batch = cs.shape[0]
    tail = cs[:, :, SUBLANES - (CONV_WIDTH - 1):, :]
    return jnp.transpose(tail, (0, 2, 1, 3)).reshape(batch, CONV_WIDTH - 1, D_FF)


def _conv_past_sample(state):
    nb = state.shape[0]
    s = state.reshape(nb, CONV_WIDTH - 1, N_FF_CHUNKS, FF_CHUNK)
    return jnp.transpose(s, (2, 1, 0, 3)).reshape(N_FF_CHUNKS, (CONV_WIDTH - 1) * nb, FF_CHUNK)


def _conv_tail_sample(cs, nb):
    s = cs.reshape(N_FF_CHUNKS, CONV_WIDTH - 1, nb, FF_CHUNK)
    return jnp.transpose(s, (2, 1, 0, 3)).reshape(nb, CONV_WIDTH - 1, D_FF)


def kernel(x_prompt, x_sample, state_pool, cache_win_g1, cache_win_g2, cache_win_g3, state_gla, state_ffn_conv,
           c_prompt, c_sample, w_ada, b_ada, norm_gain, final_gain, rel_bias, pool_w, pool_scale,
           attn_w_in, attn_w_out, gla_w_in, gla_w_gate_up, gla_b_gate, gla_norm_gain, gla_w_out,
           ffn_w_in, ffn_conv_w, ffn_conv_b, ffn_w_down):
    batch, seq, _ = x_prompt.shape
    nb, steps, _ = x_sample.shape
    caches = (cache_win_g1, cache_win_g2, cache_win_g3)
    assert x_prompt.shape[-1] == D_MODEL and D_FF % FF_CHUNK == 0
    assert seq % PROMPT_ROW_TILE == 0 and seq % GLA_TIME_TILE == 0 and GLA_TIME_TILE % GLA_CHUNK == 0
    assert seq % (QUERY_BLOCK * max(d for _, d in DILATED_GROUPS)) == 0
    assert nb % SUBLANES == 0 and steps <= SUBLANES

    mod_p, mod_s = _modulation(c_prompt, c_sample, w_ada, b_ada)
    fgain = final_gain.reshape(1, D_MODEL)

    xp = x_prompt.reshape(batch * seq, D_MODEL)
    xs = jnp.transpose(x_sample, (1, 0, 2)).reshape(steps * nb, D_MODEL)

    pool_p, pool_s, gla_p, gla_s, conv_p, conv_s = [], [], [], [], [], []
    win_p, win_s = None, None
    ffn_stack = _ffn_weights(ffn_w_in, ffn_conv_w, ffn_conv_b, ffn_w_down)

    for i in range(DEPTH):
        kind, j = i % 3, i // 3
        last = i == DEPTH - 1
        ffn = (i, ffn_stack)
        conv_past = _conv_past_sample(state_ffn_conv[i])
        gains = norm_gain[i]
        if kind == 0:
            mix_w = (pool_w[j].astype(BF16), pool_scale[j].reshape(1, D_MODEL))
            xp, cs, pst = _layer_prompt("pool", last, xp, mod_p[i], gains, fgain, mix_w, ffn)
            pool_p.append(pst[:, POOL_CARRY_ROWS - POOL_STATE_ROWS:])
            past = jnp.transpose(state_pool[j], (1, 0, 2))
            xs, css, psts = _layer_sample("pool", last, xs, mod_s[i], gains, fgain, mix_w + (past,), ffn, conv_past)
            pool_s.append(jnp.transpose(psts, (1, 0, 2)))
        elif kind == 1:
            w = attn_w_in[j]
            w3 = jnp.stack([jnp.concatenate([w[:, s * ATTN_INNER + g * GROUP_WIDTH:
                                                s * ATTN_INNER + (g + 1) * GROUP_WIDTH] for s in range(3)], axis=1)
                            for g in range(N_GROUPS)]).astype(BF16)
            gb = _group_bias(rel_bias)
            wo = attn_w_out[j].astype(BF16)
            qkvp, *kv_t = _qkv_prompt(xp.reshape(batch, seq, D_MODEL), mod_p[i], gains, w3)
            o_all = _attn_prompt(qkvp, _prompt_bias_rows(gb))
            win_p = [jnp.transpose(t.reshape(batch, 2, HEADS_PER_GROUP, HEAD_DIM, t.shape[-1]), (0, 4, 1, 2, 3))[None]
                     for t in kv_t]
            xp, cs = _layer_prompt("proj", last, xp, mod_p[i], gains, fgain,
                                   (o_all.reshape(batch * seq, ATTN_INNER), wo), ffn)
            qkv_s = _proj_sample(xs, mod_s[i], gains, w.astype(BF16))
            q6 = qkv_s.reshape(steps, nb, 3, N_GROUPS, HEADS_PER_GROUP, HEAD_DIM)
            q_s = jnp.pad(jnp.transpose(q6[:, :, 0], (1, 2, 3, 0, 4)),
                          ((0, 0), (0, 0), (0, 0), (0, SUBLANES - steps), (0, 0)))
            q_s = q_s.reshape(nb, N_GROUPS, HEADS_PER_GROUP // 2, 2, SUBLANES, HEAD_DIM)
            zeros = jnp.zeros_like(q_s[:, :, :, 0])
            q_s = jnp.stack([jnp.concatenate([q_s[:, :, :, 0], zeros], axis=-1),
                             jnp.concatenate([zeros, q_s[:, :, :, 1]], axis=-1)], axis=3)
            q_s = q_s.reshape(nb, N_GROUPS, HEADS_PER_GROUP // 2, 2 * SUBLANES, LANES)
            kv_new = jnp.transpose(q6[:, :, 1:], (1, 3, 0, 2, 4, 5)).reshape(nb, N_GROUPS, steps, 2 * GROUP_WIDTH)
            kv_new = jnp.pad(kv_new, ((0, 0), (0, 0), (0, SUBLANES - steps), (0, 0)))
            past_rows = [c.shape[2] for c in caches]
            cache_t = [jnp.transpose(c[j], (0, 2, 3, 4, 1)).reshape(nb, 2 * HEADS_PER_GROUP, HEAD_DIM, p)
                       for c, p in zip(caches, past_rows)]
            bias_past, bias_new = _sample_bias_tables(gb, steps, past_rows)
            o8, *new_caches = _attn_sample(steps, q_s, kv_new, cache_t, bias_past, bias_new)
            a_s = jnp.transpose(o8[:, :steps], (1, 0, 2)).reshape(steps * nb, ATTN_INNER)
            win_s = [jnp.transpose(c.reshape(nb, 2, HEADS_PER_GROUP, HEAD_DIM, p), (0, 4, 1, 2, 3))[None]
                     for c, p in zip(new_caches, past_rows)]
            xs, css = _layer_sample("proj", last, xs, mod_s[i], gains, fgain, (a_s.astype(BF16), wo), ffn, conv_past)
        else:
            w = gla_w_in[j]
            n_main = 2 * GLA_QK + 2 * GLA_V
            weights = (w[:, :n_main].astype(BF16),
                       jnp.pad(w[:, n_main:], ((0, 0), (0, LANES - GATE_RANK))).astype(BF16),
                       jnp.pad(gla_w_gate_up[j], ((0, LANES - GATE_RANK), (0, 0))).astype(BF16),
                       gla_b_gate[j].reshape(1, GLA_QK))
            gain = gla_norm_gain[j].reshape(1, GLA_V)
            wo = gla_w_out[j].astype(BF16)
            q, k, v, r, la = _gla_proj(xp, mod_p[i], gains, weights, None, PROMPT_ROW_TILE)
            shp = lambda a: a.reshape(batch, seq, a.shape[-1])
            a_p, s_p = _gla_prompt(shp(q), shp(k), shp(v), shp(r), shp(la), gain)
            gla_p.append(s_p)
            xp, cs = _layer_prompt("proj", last, xp, mod_p[i], gains, fgain,
                                   (a_p.reshape(batch * seq, GLA_V), wo), ffn)
            outs = _gla_proj(xs, mod_s[i], gains, weights, nb, steps * nb)

            def per_seq(a):
                a = jnp.transpose(a.reshape(steps, nb, a.shape[-1]), (1, 0, 2))
                return jnp.pad(a, ((0, 0), (0, SAMPLE_DEC_PAD - steps), (0, 0)))

            qs, ks, vs, rs, las = (per_seq(a) for a in outs)
            a16, s_s = _gla_sample(qs, ks, vs, rs, las, gain, state_gla[j])
            gla_s.append(s_s)
            a_s = jnp.transpose(a16[:, :steps], (1, 0, 2)).reshape(steps * nb, GLA_V).astype(BF16)
            xs, css = _layer_sample("proj", last, xs, mod_s[i], gains, fgain, (a_s, wo), ffn, conv_past)
        conv_p.append(_conv_tail_prompt(cs))
        conv_s.append(_conv_tail_sample(css, nb))

    y_prompt = xp.reshape(batch, seq, D_MODEL)
    y_sample = jnp.transpose(xs.reshape(steps, nb, D_MODEL), (1, 0, 2))
    return (y_prompt, y_sample, jnp.stack(pool_p), jnp.stack(pool_s),
            win_p[0], win_s[0], win_p[1], win_s[1], win_p[2], win_s[2],
            jnp.stack(gla_p), jnp.stack(gla_s), jnp.stack(conv_p), jnp.stack(conv_s))
```

```python
import functools
import math

import numpy as np
import jax
import jax.numpy as jnp
from jax import lax
from jax.experimental import pallas as pl
from jax.experimental.pallas import tpu as pltpu

F32 = jnp.float32
BF16 = jnp.bfloat16

D_MODEL = 1024
DEPTH = 4
N_MOD = 6
EPS = 1e-6
NEG_INF = -1e30
POOL_WINDOWS = (2, 4, 8, 16)
POOL_GROUP_DIM = D_MODEL // len(POOL_WINDOWS)
POOL_STATE_ROWS = max(POOL_WINDOWS) - 1
POOL_CARRY_ROWS = 16
DILATED_GROUPS = ((128, 1), (512, 4), (2048, 16))
N_GROUPS = len(DILATED_GROUPS)
HEADS_PER_GROUP = 4
HEAD_DIM = 64
GROUP_WIDTH = HEADS_PER_GROUP * HEAD_DIM
ATTN_INNER = N_GROUPS * GROUP_WIDTH
KEYS_PER_QUERY = 129
QUERY_BLOCK = 128
NUM_BUCKETS = 32
MAX_DISTANCE = 2048
GLA_HEADS = 4
GLA_DK = 128
GLA_DV = 256
GLA_QK = GLA_HEADS * GLA_DK
GLA_V = GLA_HEADS * GLA_DV
GATE_RANK = 16
GATE_TAU = 16.0
GLA_CHUNK = 128
D_FF = 2816
CONV_WIDTH = 3

LANES = 128
SUBLANES = 8
FF_CHUNK = 256
CHEAP_STRIDE = 4
N_FF_CHUNKS = D_FF // FF_CHUNK
VMEM_LIMIT_BYTES = 56 * 1024 * 1024
PROMPT_ROW_TILE = 1024
MOD_SLOTS_PER_STEP = 2
GLA_TIME_TILE = 1024
ATTN_BLOCK_UNROLL = 16
GLA_CHUNK_UNROLL = 8
GLA_SAMPLE_SEQS = 16
GLA_SAMPLE_UNROLL = 8
ATTN_SAMPLE_SEQS = 2
SAMPLE_DEC_PAD = 16


def _params(*semantics):
    return pltpu.CompilerParams(dimension_semantics=semantics, vmem_limit_bytes=VMEM_LIMIT_BYTES)


def _resident(shape):
    nd = len(shape)
    return pl.BlockSpec(shape, lambda *_: (0,) * nd, pipeline_mode=pl.Buffered(1))


def _dot(a, b):
    return jnp.dot(a, b, preferred_element_type=F32)


def _dot_nt(a, b):
    return lax.dot_general(a, b, (((1,), (1,)), ((), ())), preferred_element_type=F32)


def _dot_tn(a, b):
    return lax.dot_general(a, b, (((0,), (0,)), ((), ())), preferred_element_type=F32)


def _rms(x):
    return x * lax.rsqrt(jnp.mean(x * x, axis=-1, keepdims=True) + EPS)


def _bcast_rows(v, y, nb):
    if nb is None:
        return v * y
    rows, width = y.shape
    return (y.reshape(rows // nb, nb, width) * v[None]).reshape(rows, width)


def _norm_mod(x, gain, shift, scale, nb):
    y = _rms(x) * gain
    if nb is None:
        return y * (1.0 + scale) + shift
    rows, width = y.shape
    y3 = y.reshape(rows // nb, nb, width)
    return (y3 * (1.0 + scale)[None] + shift[None]).reshape(rows, width)


def _prompt_mod(mod_ref, b):
    return [mod_ref[k, pl.ds(b, 1), :] for k in range(N_MOD)]


def _gelu(x):
    return 0.5 * x * (1.0 + lax.erf(x * (1.0 / math.sqrt(2.0))))


def _silu(x):
    return x * jax.nn.sigmoid(x)


def _split_bf16(a):
    hi = a.astype(BF16)
    lo = (a - hi.astype(F32)).astype(BF16)
    return hi, lo


def _mod_kernel(cp_ref, cs_ref, w_ref, b_ref, op_ref, os_ref):
    batch = op_ref.shape[2]
    pad = cp_ref.shape[0]
    rows = pad + cs_ref.shape[0]
    w_hi, w_lo = _split_bf16(w_ref[0])
    a_hi, a_lo = _split_bf16(_silu(jnp.concatenate([cp_ref[...], cs_ref[...]], axis=0)))
    both = _dot(jnp.concatenate([a_hi, a_lo], axis=0), w_hi)
    out = both[0:rows] + both[rows:2 * rows] + _dot(a_hi, w_lo) + b_ref[0]
    for k in range(MOD_SLOTS_PER_STEP):
        cols = slice(k * D_MODEL, (k + 1) * D_MODEL)
        op_ref[0, k] = out[0:batch, cols]
        os_ref[0, k] = out[pad:rows, cols]


def _modulation(c_prompt, c_sample, w_ada, b_ada):
    batch, nb = c_prompt.shape[0], c_sample.shape[0]
    bf16_rows = 2 * SUBLANES
    c_prompt = jnp.pad(c_prompt, ((0, -batch % bf16_rows), (0, 0)))
    return pl.pallas_call(
        _mod_kernel,
        grid=(DEPTH, N_MOD // MOD_SLOTS_PER_STEP),
        in_specs=[pl.BlockSpec(c_prompt.shape, lambda l, m: (0, 0)),
                  pl.BlockSpec((nb, D_MODEL), lambda l, m: (0, 0)),
                  pl.BlockSpec((1, D_MODEL, MOD_SLOTS_PER_STEP * D_MODEL), lambda l, m: (l, 0, m)),
                  pl.BlockSpec((1, 1, MOD_SLOTS_PER_STEP * D_MODEL), lambda l, m: (l, 0, m))],
        out_specs=[pl.BlockSpec((1, MOD_SLOTS_PER_STEP, batch, D_MODEL), lambda l, m: (l, m, 0, 0)),
                   pl.BlockSpec((1, MOD_SLOTS_PER_STEP, nb, D_MODEL), lambda l, m: (l, m, 0, 0))],
        out_shape=[jax.ShapeDtypeStruct((DEPTH, N_MOD, batch, D_MODEL), F32),
                   jax.ShapeDtypeStruct((DEPTH, N_MOD, nb, D_MODEL), F32)],
        compiler_params=_params("parallel", "parallel"),
        name="adaln_mod",
    )(c_prompt, c_sample, w_ada, b_ada.reshape(DEPTH, 1, N_MOD * D_MODEL))


def _ffn_chunk_math(g, g_m1, g_m2, u, cw, cb):
    gc = cw[2:3] * g + cw[1:2] * g_m1 + cw[0:1] * g_m2 + cb
    return (_gelu(gc) * u).astype(BF16)


def _ffn_chunk_weights(win_ref, wd_ref, cw_ref, cb_ref, j):
    cols = slice(j * FF_CHUNK, (j + 1) * FF_CHUNK)
    ucols = slice(D_FF + j * FF_CHUNK, D_FF + (j + 1) * FF_CHUNK)
    return win_ref[0, :, cols], win_ref[0, :, ucols], wd_ref[0, cols, :], cw_ref[0, :, cols], cb_ref[0, :, cols]


def _layer_resident(array, layer):
    nd = array.ndim
    return pl.BlockSpec((1,) + array.shape[1:], lambda *_: (layer,) + (0,) * (nd - 1),
                        pipeline_mode=pl.Buffered(1))


def _layer_prompt_kernel(mixer, last, tm, tiles_per_seq, *refs):
    refs = list(refs)
    x_ref, mod_ref, gains_ref, fg_ref = refs[:4]
    refs = refs[4:]
    if mixer == "pool":
        pw_ref, ps_ref = refs[:2]
    else:
        a_ref, wp_ref = refs[:2]
    win_ref, wd_ref, cw_ref, cb_ref = refs[2:6]
    refs = refs[6:]
    if mixer == "pool":
        y_ref, cs_ref, pst_ref, h2_ref, act_ref, gext_ref, cc_ref, hext_ref = refs
    else:
        y_ref, cs_ref, h2_ref, act_ref, gext_ref, cc_ref = refs

    i = pl.program_id(0)
    tile_in_seq = i % tiles_per_seq

    @pl.when(tile_in_seq == 0)
    def _():
        cc_ref[...] = jnp.zeros_like(cc_ref)
        if mixer == "pool":
            hext_ref[0:POOL_CARRY_ROWS, :] = jnp.zeros((POOL_CARRY_ROWS, D_MODEL), F32)

    x = x_ref[...]
    m = _prompt_mod(mod_ref, i // tiles_per_seq)
    gains = gains_ref[...]

    if mixer == "pool":
        h = _norm_mod(x, gains[0:1], m[0], m[1], None)
        hext_ref[POOL_CARRY_ROWS:, :] = h
        pos = tile_in_seq * tm + lax.broadcasted_iota(jnp.int32, (tm, 1), 0)
        parts = []
        for g, w in enumerate(POOL_WINDOWS):
            cols = slice(g * POOL_GROUP_DIM, (g + 1) * POOL_GROUP_DIM)
            s = hext_ref[:, cols]
            span = 1
            while span < w:
                s = s + pltpu.roll(s, span, 0)
                span *= 2
            inv_count = 1.0 / jnp.minimum(pos + 1, w).astype(F32)
            d = s[POOL_CARRY_ROWS:, :] * inv_count - h[:, cols]
            parts.append(_dot(d.astype(BF16), pw_ref[g]))
        mix = jnp.concatenate(parts, axis=-1) * ps_ref[...]
        tail = hext_ref[pl.ds(tm, POOL_CARRY_ROWS), :]
        hext_ref[0:POOL_CARRY_ROWS, :] = tail
        pst_ref[0] = tail
    else:
        mix = _dot(a_ref[...], wp_ref[...])

    x1 = x + m[2] * mix
    y_ref[...] = x1
    h2_ref[...] = _norm_mod(x1, gains[1:2], m[3], m[4], None).astype(BF16)

    def up_proj(j):
        wg, wu, _, _, _ = _ffn_chunk_weights(win_ref, wd_ref, cw_ref, cb_ref, j)
        h2 = h2_ref[...]
        return _dot(h2, wg), _dot(h2, wu)

    ahead = up_proj(0)
    for j in range(N_FF_CHUNKS):
        g, u = ahead
        if j + 1 < N_FF_CHUNKS:
            ahead = up_proj(j + 1)
        _, _, _, cw, cb = _ffn_chunk_weights(win_ref, wd_ref, cw_ref, cb_ref, j)
        gx = gext_ref.at[j % 2]
        gx[0:SUBLANES, :] = cc_ref[j]
        gx[SUBLANES:, :] = g
        act_ref[:, j * FF_CHUNK:(j + 1) * FF_CHUNK] = _ffn_chunk_math(
            g, gx[pl.ds(SUBLANES - 1, tm), :], gx[pl.ds(SUBLANES - 2, tm), :], u, cw, cb)
        tail = g[tm - SUBLANES:tm, :]
        cc_ref[j] = tail
        cs_ref[0, j] = tail
    xo = y_ref[...] + m[5] * _dot(act_ref[...], wd_ref[0])
    if last:
        xo = _rms(xo) * fg_ref[...]
    y_ref[...] = xo


def _layer_prompt(mixer, last, x, mod, gains, fgain, mix_args, ffn):
    n = x.shape[0]
    batch = mod.shape[1]
    seq = n // batch
    tm = PROMPT_ROW_TILE
    tps = seq // tm
    in_specs = [pl.BlockSpec((tm, D_MODEL), lambda i: (i, 0)),
                _resident(mod.shape), _resident((2, D_MODEL)), _resident((1, D_MODEL))]
    if mixer == "pool":
        pw, ps = mix_args
        in_specs += [_resident(pw.shape), _resident(ps.shape)]
    else:
        a, wp = mix_args
        in_specs += [pl.BlockSpec((tm, a.shape[1]), lambda i: (i, 0)), _resident(wp.shape)]
    layer, ffn_w = ffn
    in_specs += [_layer_resident(w, layer) for w in ffn_w]
    out_shape = [jax.ShapeDtypeStruct((n, D_MODEL), F32),
                 jax.ShapeDtypeStruct((batch, N_FF_CHUNKS, SUBLANES, FF_CHUNK), F32)]
    out_specs = [pl.BlockSpec((tm, D_MODEL), lambda i: (i, 0)),
                 pl.BlockSpec((1, N_FF_CHUNKS, SUBLANES, FF_CHUNK), lambda i: (i // tps, 0, 0, 0))]
    scratch = [pltpu.VMEM((tm, D_MODEL), BF16), pltpu.VMEM((tm, D_FF), BF16),
               pltpu.VMEM((2, tm + SUBLANES, FF_CHUNK), F32), pltpu.VMEM((N_FF_CHUNKS, SUBLANES, FF_CHUNK), F32)]
    if mixer == "pool":
        out_shape.append(jax.ShapeDtypeStruct((batch, POOL_CARRY_ROWS, D_MODEL), F32))
        out_specs.append(pl.BlockSpec((1, POOL_CARRY_ROWS, D_MODEL), lambda i: (i // tps, 0, 0)))
        scratch.append(pltpu.VMEM((tm + POOL_CARRY_ROWS, D_MODEL), F32))
    return pl.pallas_call(
        functools.partial(_layer_prompt_kernel, mixer, last, tm, tps),
        grid=(n // tm,), in_specs=in_specs, out_specs=out_specs, out_shape=out_shape,
        scratch_shapes=scratch, compiler_params=_params("arbitrary"),
        name=f"layer_prompt_{mixer}",
    )(x, mod, gains, fgain, *mix_args, *ffn_w)


def _layer_sample_kernel(mixer, last, nb, steps, *refs):
    refs = list(refs)
    x_ref, mod_ref, gains_ref, fg_ref = refs[:4]
    refs = refs[4:]
    if mixer == "pool":
        pw_ref, ps_ref, ppast_ref = refs[:3]
        refs = refs[3:]
    else:
        a_ref, wp_ref = refs[:2]
        refs = refs[2:]
    win_ref, wd_ref, cw_ref, cb_ref, cpast_ref = refs[:5]
    refs = refs[5:]
    if mixer == "pool":
        y_ref, cs_ref, pst_ref, h2_ref, act_ref = refs
    else:
        y_ref, cs_ref, h2_ref, act_ref = refs
    rows = steps * nb

    x = x_ref[...]
    gains = gains_ref[...]
    if mixer == "pool":
        h = _norm_mod(x, gains[0:1], mod_ref[0], mod_ref[1], nb)
        new = [h[t * nb:(t + 1) * nb, :] for t in range(steps)]

        def u_rows(p, cols):
            if p < POOL_STATE_ROWS:
                return ppast_ref[p, :, cols]
            return new[p - POOL_STATE_ROWS][:, cols]

        parts = []
        for g, w in enumerate(POOL_WINDOWS):
            cols = slice(g * POOL_GROUP_DIM, (g + 1) * POOL_GROUP_DIM)
            ds = []
            for t in range(steps):
                s = u_rows(POOL_STATE_ROWS + t, cols)
                for k in range(1, w):
                    s = s + u_rows(POOL_STATE_ROWS + t - k, cols)
                ds.append(s * (1.0 / w) - new[t][:, cols])
            parts.append(_dot(jnp.concatenate(ds, axis=0).astype(BF16), pw_ref[g]))
        mix = jnp.concatenate(parts, axis=-1) * ps_ref[...]
        full = slice(0, D_MODEL)
        for p in range(POOL_STATE_ROWS):
            pst_ref[p] = u_rows(p + steps, full)
    else:
        mix = _dot(a_ref[...], wp_ref[...])

    x1 = x + _bcast_rows(mod_ref[2], mix, nb)
    y_ref[...] = x1
    h2_ref[...] = _norm_mod(x1, gains[1:2], mod_ref[3], mod_ref[4], nb).astype(BF16)
    past_rows = (CONV_WIDTH - 1) * nb

    for j in range(N_FF_CHUNKS):
        h2 = h2_ref[...]
        wg, wu, _, cw, cb = _ffn_chunk_weights(win_ref, wd_ref, cw_ref, cb_ref, j)
        g = _dot(h2, wg)
        u = _dot(h2, wu)
        gall = jnp.concatenate([cpast_ref[j], g], axis=0)
        act_ref[:, j * FF_CHUNK:(j + 1) * FF_CHUNK] = _ffn_chunk_math(
            g, gall[nb:nb + rows, :], gall[0:rows, :], u, cw, cb)
        cs_ref[j] = gall[rows:rows + past_rows, :]
    xo = y_ref[...] + _bcast_rows(mod_ref[5], _dot(act_ref[...], wd_ref[0]), nb)
    if last:
        xo = _rms(xo) * fg_ref[...]
    y_ref[...] = xo


def _layer_sample(mixer, last, x, mod, gains, fgain, mix_args, ffn, conv_past):
    rows = x.shape[0]
    nb = mod.shape[1]
    steps = rows // nb
    layer, ffn_w = ffn
    head = [x, mod, gains, fgain, *mix_args]
    args = [*head, *ffn_w, conv_past]
    out_shape = [jax.ShapeDtypeStruct((rows, D_MODEL), F32),
                 jax.ShapeDtypeStruct(conv_past.shape, F32)]
    if mixer == "pool":
        out_shape.append(jax.ShapeDtypeStruct((POOL_STATE_ROWS, nb, D_MODEL), F32))
    return pl.pallas_call(
        functools.partial(_layer_sample_kernel, mixer, last, nb, steps),
        grid=(1,),
        in_specs=[_resident(a.shape) for a in head] + [_layer_resident(w, layer) for w in ffn_w]
                 + [_resident(conv_past.shape)],
        out_specs=[pl.BlockSpec(s.shape, functools.partial(lambda nd, i: (0,) * nd, len(s.shape))) for s in out_shape],
        out_shape=out_shape,
        scratch_shapes=[pltpu.VMEM((rows, D_MODEL), BF16), pltpu.VMEM((rows, D_FF), BF16)],
        compiler_params=_params("arbitrary"),
        name=f"layer_sample_{mixer}",
    )(*args)


def _qkv_prompt_kernel(seq, x_ref, mod_ref, gains_ref, w_ref, qkvp_ref, kt1_ref, kt2_ref, kt3_ref, h_ref, slab_ref, tmp_ref):
    g = pl.program_id(1)
    kt_refs = (kt1_ref, kt2_ref, kt3_ref)

    @pl.when(g == 0)
    def _():
        m = _prompt_mod(mod_ref, pl.program_id(0))
        h_ref[...] = _norm_mod(x_ref[0], gains_ref[0:1, :], m[0], m[1], None).astype(BF16)

    h = h_ref[...]
    for c in range(3):
        r = _dot(h, w_ref[0, :, c * GROUP_WIDTH:(c + 1) * GROUP_WIDTH])
        if c == 0:
            r = r * (HEAD_DIM ** -0.5)
        slab_ref[2 * c] = r[:, 0:LANES]
        slab_ref[2 * c + 1] = r[:, LANES:2 * LANES]

    qkvp_ref[0, 0, 0:QUERY_BLOCK, :] = jnp.zeros((QUERY_BLOCK, 3 * GROUP_WIDTH), BF16)
    for gi, (_, dil) in enumerate(DILATED_GROUPS):
        @pl.when(g == gi)
        def _(dil=dil):
            per_class = seq // dil
            assert dil in (1, CHEAP_STRIDE, CHEAP_STRIDE ** 2)
            for s in range(6):
                if dil > CHEAP_STRIDE:
                    part = seq // CHEAP_STRIDE
                    for r in range(CHEAP_STRIDE):
                        tmp_ref[r * part:(r + 1) * part, :] = slab_ref[s, pl.ds(r, part, stride=CHEAP_STRIDE), :]
                for rho in range(dil):
                    if dil == 1:
                        v = slab_ref[s]
                    elif dil == CHEAP_STRIDE:
                        v = slab_ref[s, pl.ds(rho, per_class, stride=dil), :]
                    else:
                        r, q = rho % CHEAP_STRIDE, rho // CHEAP_STRIDE
                        v = tmp_ref[pl.ds(r * part + q, per_class, stride=dil // CHEAP_STRIDE), :]
                    r0 = QUERY_BLOCK + rho * per_class
                    qkvp_ref[0, 0, r0:r0 + per_class, s * LANES:(s + 1) * LANES] = v.astype(BF16)

    for gi, kt_ref in enumerate(kt_refs):
        @pl.when(g == gi)
        def _(kt_ref=kt_ref):
            keep = kt_ref.shape[-1]
            for s in range(2, 6):
                tile_t = slab_ref[s, seq - keep:seq, :].T
                for r in range(2):
                    kt_ref[0, 2 * (s - 2) + r] = tile_t[r * HEAD_DIM:(r + 1) * HEAD_DIM, :]


def _qkv_prompt(x3, mod, gains, w3):
    batch, seq, _ = x3.shape
    width = 3 * GROUP_WIDTH
    keeps = [min(window, seq) for window, _ in DILATED_GROUPS]
    return pl.pallas_call(
        functools.partial(_qkv_prompt_kernel, seq),
        grid=(batch, N_GROUPS),
        in_specs=[pl.BlockSpec((1, seq, D_MODEL), lambda b, g: (b, 0, 0)),
                  pl.BlockSpec(mod.shape, lambda b, g: (0, 0, 0)),
                  pl.BlockSpec((2, D_MODEL), lambda b, g: (0, 0)),
                  pl.BlockSpec((1, D_MODEL, width), lambda b, g: (g, 0, 0))],
        out_specs=[pl.BlockSpec((1, 1, seq + QUERY_BLOCK, width), lambda b, g: (b, g, 0, 0))]
                  + [pl.BlockSpec((1, 2 * HEADS_PER_GROUP, HEAD_DIM, keep), lambda b, g: (b, 0, 0, 0)) for keep in keeps],
        out_shape=[jax.ShapeDtypeStruct((batch, N_GROUPS, seq + QUERY_BLOCK, width), BF16)]
                  + [jax.ShapeDtypeStruct((batch, 2 * HEADS_PER_GROUP, HEAD_DIM, keep), F32) for keep in keeps],
        scratch_shapes=[pltpu.VMEM((seq, D_MODEL), BF16), pltpu.VMEM((6, seq, LANES), F32),
                        pltpu.VMEM((seq, LANES), F32)],
        compiler_params=_params("arbitrary", "arbitrary"),
        name="qkv_prompt",
    )(x3, mod, gains, w3)


def _head_lane_mask(rows, h):
    lane = lax.broadcasted_iota(jnp.int32, (rows, GROUP_WIDTH), 1)
    return (lane >= h * HEAD_DIM) & (lane < (h + 1) * HEAD_DIM)


def _attn_block(q, k, v, bias_ref, cols):
    masks = [_head_lane_mask(QUERY_BLOCK, h) for h in range(HEADS_PER_GROUP)]
    qs = jnp.concatenate([jnp.where(hm, q, jnp.zeros_like(q)) for hm in masks], axis=0)
    s = _dot_nt(qs, k) + bias_ref[:, cols]
    m = jnp.max(s, axis=-1, keepdims=True)
    p = jnp.exp(s - m)
    l = jnp.sum(p, axis=-1, keepdims=True)
    pv = _dot(p.astype(BF16), v) * (1.0 / l)
    lse_rows = jnp.broadcast_to(m + jnp.log(l), pv.shape)
    o = pv[0:QUERY_BLOCK]
    lse = lse_rows[0:QUERY_BLOCK]
    for h in range(1, HEADS_PER_GROUP):
        rows = slice(h * QUERY_BLOCK, (h + 1) * QUERY_BLOCK)
        o = jnp.where(masks[h], pv[rows], o)
        lse = jnp.where(masks[h], lse_rows[rows], lse)
    return o, lse


def _attn_prompt_kernel(seq, qkv_ref, brow_ref, o_ref, os_ref, ls_ref, bias_ref):
    @pl.when(pl.program_id(0) == 0)
    def _():
        left = lax.broadcasted_iota(jnp.int32, (QUERY_BLOCK, 2 * QUERY_BLOCK), 1) < QUERY_BLOCK
        for g in range(N_GROUPS):
            for h in range(HEADS_PER_GROUP):
                base = jnp.broadcast_to(brow_ref[g, h], (QUERY_BLOCK, 2 * QUERY_BLOCK))
                band = pltpu.roll(base, 0, 1, stride=1, stride_axis=0)
                rows = slice(h * QUERY_BLOCK, (h + 1) * QUERY_BLOCK)
                bias_ref[g, 0, rows, :] = band
                bias_ref[g, 1, rows, :] = jnp.where(left, NEG_INF, band)

    qc = slice(0, GROUP_WIDTH)
    kc = slice(GROUP_WIDTH, 2 * GROUP_WIDTH)
    vc = slice(2 * GROUP_WIDTH, 3 * GROUP_WIDTH)
    n_blocks = seq // QUERY_BLOCK
    for g, (_, dil) in enumerate(DILATED_GROUPS):
        blocks_per_class = n_blocks // dil

        def do_block(blk, carry, g=g, dil=dil, blocks_per_class=blocks_per_class):
            r0 = pl.multiple_of(blk * QUERY_BLOCK, QUERY_BLOCK)
            cur = pl.ds(r0 + QUERY_BLOCK, QUERY_BLOCK)
            q = qkv_ref[0, g, cur, qc]
            rho = blk // blocks_per_class
            in_class = blk % blocks_per_class
            if blocks_per_class == 1:
                o, lse = _attn_block(q, qkv_ref[0, g, cur, kc], qkv_ref[0, g, cur, vc],
                                     bias_ref.at[g, 0], slice(QUERY_BLOCK, 2 * QUERY_BLOCK))
            else:
                both = pl.ds(r0, 2 * QUERY_BLOCK)
                first = jnp.asarray(in_class == 0, jnp.int32)
                o, lse = _attn_block(q, qkv_ref[0, g, both, kc], qkv_ref[0, g, both, vc],
                                     bias_ref.at[g, first], slice(0, 2 * QUERY_BLOCK))
            start = in_class * (QUERY_BLOCK * dil) + rho
            for s in range(2):
                cols = slice(s * LANES, (s + 1) * LANES)
                if dil == 1:
                    os_ref[g, s, pl.ds(r0, QUERY_BLOCK), :] = o[:, cols]
                    ls_ref[g, s, pl.ds(r0, QUERY_BLOCK), :] = lse[:, cols]
                else:
                    os_ref[g, s, pl.ds(start, QUERY_BLOCK, stride=dil), :] = o[:, cols]
                    ls_ref[g, s, pl.ds(start, QUERY_BLOCK, stride=dil), :] = lse[:, cols]
            return carry

        lax.fori_loop(0, n_blocks, do_block, 0, unroll=ATTN_BLOCK_UNROLL)

    def merge(i, carry):
        r0 = pl.multiple_of(i * QUERY_BLOCK, QUERY_BLOCK)
        rows = pl.ds(r0, QUERY_BLOCK)
        for s in range(2):
            ls = [ls_ref[g, s, rows, :] for g in range(N_GROUPS)]
            mx = jnp.maximum(jnp.maximum(ls[0], ls[1]), ls[2])
            es = [jnp.exp(l - mx) for l in ls]
            inv = 1.0 / (es[0] + es[1] + es[2])
            for g in range(N_GROUPS):
                c0 = g * GROUP_WIDTH + s * LANES
                o_ref[0, rows, c0:c0 + LANES] = (os_ref[g, s, rows, :] * (es[g] * inv)).astype(BF16)
        return carry

    lax.fori_loop(0, seq // QUERY_BLOCK, merge, 0)


def _attn_prompt(qkvp, bias_rows):
    batch, _, padded, width = qkvp.shape
    seq = padded - QUERY_BLOCK
    return pl.pallas_call(
        functools.partial(_attn_prompt_kernel, seq),
        grid=(batch,),
        in_specs=[pl.BlockSpec((1, N_GROUPS, padded, width), lambda b: (b, 0, 0, 0)),
                  _resident(bias_rows.shape)],
        out_specs=pl.BlockSpec((1, seq, ATTN_INNER), lambda b: (b, 0, 0)),
        out_shape=jax.ShapeDtypeStruct((batch, seq, ATTN_INNER), BF16),
        scratch_shapes=[pltpu.VMEM((N_GROUPS, 2, seq, LANES), F32), pltpu.VMEM((N_GROUPS, 2, seq, LANES), F32),
                        pltpu.VMEM((N_GROUPS, 2, HEADS_PER_GROUP * QUERY_BLOCK, 2 * QUERY_BLOCK), F32)],
        compiler_params=_params("arbitrary"),
        name="attn_prompt",
    )(qkvp, bias_rows)


def _proj_sample_kernel(nb, x_ref, mod_ref, gains_ref, w_ref, o_ref):
    h = _norm_mod(x_ref[...], gains_ref[0:1, :], mod_ref[0], mod_ref[1], nb).astype(BF16)
    o_ref[...] = _dot(h, w_ref[...])


def _proj_sample(x, mod, gains, w):
    rows = x.shape[0]
    nb = mod.shape[1]
    args = [x, mod, gains, w]
    return pl.pallas_call(
        functools.partial(_proj_sample_kernel, nb),
        grid=(1,),
        in_specs=[_resident(a.shape) for a in args],
        out_specs=pl.BlockSpec((rows, w.shape[1]), lambda i: (0, 0)),
        out_shape=jax.ShapeDtypeStruct((rows, w.shape[1]), F32),
        compiler_params=_params("arbitrary"),
        name="qkv_sample",
    )(*args)


def _split3_bf16(a):
    hi = a.astype(BF16)
    r1 = a - hi.astype(F32)
    mid = r1.astype(BF16)
    lo = (r1 - mid.astype(F32)).astype(BF16)
    return hi, mid, lo


def _attn_sample_kernel(steps, q_ref, kvn_ref, c1_ref, c2_ref, c3_ref, t1_ref, t2_ref, t3_ref, bn_ref,
                        o_ref, n1_ref, n2_ref, n3_ref):
    for i in range(q_ref.shape[0]):
        _attn_sample_one(steps, i, q_ref, kvn_ref, ((c1_ref, t1_ref, n1_ref), (c2_ref, t2_ref, n2_ref),
                                                  (c3_ref, t3_ref, n3_ref)), bn_ref, o_ref)


def _attn_sample_one(steps, i, q_ref, kvn_ref, groups, bn_ref, o_ref):
    pairs = HEADS_PER_GROUP // 2
    outs, lses = [], []
    lane = lax.broadcasted_iota(jnp.int32, (SUBLANES, LANES), 1)
    row = lax.broadcasted_iota(jnp.int32, (SUBLANES, LANES), 0)
    own_lanes = lane < HEAD_DIM
    sel_head = jnp.where((lane == row) & (row < steps), 1.0, 0.0).astype(BF16)
    sel_tail = jnp.where((lane == row + (LANES - steps)) & (row < steps), 1.0, 0.0).astype(BF16)
    for g, (c_ref, t_ref, n_ref) in enumerate(groups):
        p_rows = c_ref.shape[-1]
        pieces = _split3_bf16(kvn_ref[i, g])
        new_head = sum(_dot_tn(x, sel_head) for x in pieces)
        new_tail = sum(_dot_tn(x, sel_tail) for x in pieces)
        o_pairs, l_pairs = [], []
        for j in range(pairs):
            q = (q_ref[i, g, j] * (HEAD_DIM ** -0.5)).astype(BF16)
            kt = c_ref[i, 2 * j:2 * j + 2].reshape(LANES, p_rows).astype(BF16)
            vt = c_ref[i, HEADS_PER_GROUP + 2 * j:HEADS_PER_GROUP + 2 * j + 2].reshape(LANES, p_rows).astype(BF16)
            k_new = new_head[j * LANES:(j + 1) * LANES].astype(BF16)
            v_new = new_head[(pairs + j) * LANES:(pairs + j + 1) * LANES].astype(BF16)
            s = _dot(q, kt) + t_ref[j]
            s_new = _dot(q, k_new) + bn_ref[g, j]
            m = jnp.maximum(jnp.max(s, axis=-1, keepdims=True), jnp.max(s_new, axis=-1, keepdims=True))
            p = jnp.exp(s - m)
            p_new = jnp.exp(s_new - m)
            l = jnp.sum(p, axis=-1, keepdims=True) + jnp.sum(p_new, axis=-1, keepdims=True)
            o = _dot_nt(p.astype(BF16), vt) + _dot_nt(p_new.astype(BF16), v_new)
            o_pairs.append(o * (1.0 / l))
            l_pairs.append(m + jnp.log(l))
            for idx in (2 * j, 2 * j + 1, HEADS_PER_GROUP + 2 * j, HEADS_PER_GROUP + 2 * j + 1):
                n_ref[i, idx] = pltpu.roll(c_ref[i, idx], p_rows - steps, 1)
                n_ref[i, idx, :, p_rows - steps:p_rows] = new_tail[idx * HEAD_DIM:(idx + 1) * HEAD_DIM, LANES - steps:LANES]
        outs.append(o_pairs)
        lses.append(l_pairs)
    for j in range(pairs):
        ls = [lses[g][j] for g in range(N_GROUPS)]
        mx = jnp.maximum(jnp.maximum(ls[0], ls[1]), ls[2])
        es = [jnp.exp(l - mx) for l in ls]
        inv = 1.0 / (es[0] + es[1] + es[2])
        for g in range(N_GROUPS):
            og = outs[g][j] * (es[g] * inv)
            c0 = g * GROUP_WIDTH + j * LANES
            o_ref[i, :, c0:c0 + LANES] = jnp.where(own_lanes, og[0:SUBLANES], og[SUBLANES:2 * SUBLANES])


def _attn_sample(steps, q, kv_new, caches, tables, bias_new):
    nb = q.shape[0]
    sb = math.gcd(nb, ATTN_SAMPLE_SEQS)
    in_specs = [pl.BlockSpec((sb,) + q.shape[1:], lambda b: (b, 0, 0, 0, 0)),
                pl.BlockSpec((sb,) + kv_new.shape[1:], lambda b: (b, 0, 0, 0))]
    cache_specs = [pl.BlockSpec((sb,) + c.shape[1:], lambda b: (b, 0, 0, 0)) for c in caches]
    in_specs += cache_specs + [_resident(t.shape) for t in tables] + [_resident(bias_new.shape)]
    return pl.pallas_call(
        functools.partial(_attn_sample_kernel, steps),
        grid=(nb // sb,), in_specs=in_specs,
        out_specs=[pl.BlockSpec((sb, SUBLANES, ATTN_INNER), lambda b: (b, 0, 0))] + cache_specs,
        out_shape=[jax.ShapeDtypeStruct((nb, SUBLANES, ATTN_INNER), F32)]
                  + [jax.ShapeDtypeStruct(c.shape, F32) for c in caches],
        compiler_params=_params("arbitrary"),
        name="attn_sample",
    )(q, kv_new, *caches, *tables, bias_new)


def _gla_proj_kernel(nb, tiles_per_seq, x_ref, mod_ref, gains_ref, w_ref, wgd_ref, wgu_ref, bg_ref,
                     q_ref, k_ref, v_ref, r_ref, la_ref):
    if nb is None:
        m = _prompt_mod(mod_ref, pl.program_id(0) // tiles_per_seq)
        shift, scale = m[0], m[1]
    else:
        shift, scale = mod_ref[0], mod_ref[1]
    h = _norm_mod(x_ref[...], gains_ref[0:1, :], shift, scale, nb).astype(BF16)
    q_ref[...] = _dot(h, w_ref[:, 0:GLA_QK]) * (GLA_DK ** -0.5)
    k_ref[...] = _dot(h, w_ref[:, GLA_QK:2 * GLA_QK])
    v_ref[...] = _dot(h, w_ref[:, 2 * GLA_QK:2 * GLA_QK + GLA_V]).astype(BF16)
    r_ref[...] = _dot(h, w_ref[:, 2 * GLA_QK + GLA_V:2 * GLA_QK + 2 * GLA_V])
    gd = _dot(h, wgd_ref[...])
    gate = _dot(gd.astype(BF16), wgu_ref[...]) + bg_ref[...]
    la_ref[...] = jax.nn.log_sigmoid(gate) * (1.0 / GATE_TAU)


def _gla_proj(x, mod, gains, weights, nb, tm):
    rows = x.shape[0]
    tps = None if nb is not None else rows // mod.shape[1] // tm
    widths = (GLA_QK, GLA_QK, GLA_V, GLA_V, GLA_QK)
    dtypes = (F32, F32, BF16, F32, F32)
    return pl.pallas_call(
        functools.partial(_gla_proj_kernel, nb, tps),
        grid=(rows // tm,),
        in_specs=[pl.BlockSpec((tm, D_MODEL), lambda i: (i, 0)), _resident(mod.shape), _resident((2, D_MODEL))]
                 + [_resident(w.shape) for w in weights],
        out_specs=[pl.BlockSpec((tm, w), lambda i: (i, 0)) for w in widths],
        out_shape=[jax.ShapeDtypeStruct((rows, w), dt) for w, dt in zip(widths, dtypes)],
        compiler_params=_params("arbitrary"),
        name="gla_proj",
    )(x, mod, gains, *weights)


def _cumsum_rows(g):
    rows = g.shape[0]
    row = lax.broadcasted_iota(jnp.int32, g.shape, 0)
    b = g
    shift = 1
    while shift < rows:
        b = b + jnp.where(row >= shift, pltpu.roll(b, shift, 0), 0.0)
        shift *= 2
    return b


def _gla_chunk(q, k, v, g, r, gain, mid, get_state, set_state):
    c = q.shape[0]
    b = _cumsum_rows(g)
    b_end = b[c - 1:c, :]
    b_mid = b[mid:mid + 1, :]
    q_in = (q * jnp.exp(b)).astype(BF16)
    q_rel = (q * jnp.exp(b - b_mid)).astype(BF16)
    k_rel = (k * jnp.exp(b_mid - b)).astype(BF16)
    k_out = (k * jnp.exp(b_end - b)).astype(BF16)
    decay_end = jnp.broadcast_to(jnp.exp(b_end), (SUBLANES, GLA_QK))
    ti = lax.broadcasted_iota(jnp.int32, (c, c), 0)
    si = lax.broadcasted_iota(jnp.int32, (c, c), 1)
    outs = []
    for h in range(GLA_HEADS):
        ks = slice(h * GLA_DK, (h + 1) * GLA_DK)
        vs = slice(h * GLA_DV, (h + 1) * GLA_DV)
        state = get_state(h)
        o = _dot(q_in[:, ks], state.astype(BF16))
        att = jnp.where(si <= ti, _dot_nt(q_rel[:, ks], k_rel[:, ks]), 0.0)
        o = o + _dot(att.astype(BF16), v[:, vs])
        decay_col = decay_end[:, ks].T[:, 0:1]
        set_state(h, decay_col * state + _dot_tn(k_out[:, ks], v[:, vs]))
        outs.append(_rms(o) * gain[:, vs] * _silu(r[:, vs]))
    return jnp.concatenate(outs, axis=-1)


def _gla_prompt_kernel(tt, q_ref, k_ref, v_ref, r_ref, la_ref, gain_ref, a_ref, so_ref, s_ref):
    t = pl.program_id(1)

    @pl.when(t == 0)
    def _():
        s_ref[...] = jnp.zeros_like(s_ref)

    gain = gain_ref[...]

    def get_state(h):
        return s_ref[h]

    def set_state(h, val):
        s_ref[h] = val

    def chunk(ci, carry):
        rows = pl.ds(pl.multiple_of(ci * GLA_CHUNK, GLA_CHUNK), GLA_CHUNK)
        a = _gla_chunk(q_ref[0, rows, :], k_ref[0, rows, :], v_ref[0, rows, :], la_ref[0, rows, :],
                       r_ref[0, rows, :], gain, GLA_CHUNK // 2, get_state, set_state)
        a_ref[0, rows, :] = a.astype(BF16)
        return carry

    lax.fori_loop(0, tt // GLA_CHUNK, chunk, 0, unroll=GLA_CHUNK_UNROLL)

    @pl.when(t == pl.num_programs(1) - 1)
    def _():
        so_ref[0] = s_ref[...]


def _gla_prompt(q, k, v, r, la, gain):
    batch, seq, _ = q.shape
    tt = GLA_TIME_TILE

    def spec(width):
        return pl.BlockSpec((1, tt, width), lambda b, t: (b, t, 0))

    return pl.pallas_call(
        functools.partial(_gla_prompt_kernel, tt),
        grid=(batch, seq // tt),
        in_specs=[spec(GLA_QK), spec(GLA_QK), spec(GLA_V), spec(GLA_V), spec(GLA_QK),
                  pl.BlockSpec((1, GLA_V), lambda b, t: (0, 0))],
        out_specs=[spec(GLA_V), pl.BlockSpec((1, GLA_HEADS, GLA_DK, GLA_DV), lambda b, t: (b, 0, 0, 0))],
        out_shape=[jax.ShapeDtypeStruct((batch, seq, GLA_V), BF16),
                   jax.ShapeDtypeStruct((batch, GLA_HEADS, GLA_DK, GLA_DV), F32)],
        scratch_shapes=[pltpu.VMEM((GLA_HEADS, GLA_DK, GLA_DV), F32)],
        compiler_params=_params("arbitrary", "arbitrary"),
        name="gla_prompt",
    )(q, k, v, r, la, gain)


def _gla_sample_kernel(sb, q_ref, k_ref, v_ref, r_ref, la_ref, gain_ref, s0_ref, a_ref, so_ref):
    gain = gain_ref[...]

    def seq_body(i, carry):
        def get_state(h):
            return s0_ref[i, h]

        def set_state(h, val):
            so_ref[i, h] = val

        a_ref[i] = _gla_chunk(q_ref[i], k_ref[i], v_ref[i], la_ref[i], r_ref[i], gain, 0, get_state, set_state)
        return carry

    lax.fori_loop(0, sb, seq_body, 0, unroll=GLA_SAMPLE_UNROLL)


def _gla_sample(q, k, v, r, la, gain, s0):
    nb, pad, _ = q.shape
    sb = math.gcd(nb, GLA_SAMPLE_SEQS)

    def spec(width):
        return pl.BlockSpec((sb, pad, width), lambda i: (i, 0, 0))

    state_spec = pl.BlockSpec((sb, GLA_HEADS, GLA_DK, GLA_DV), lambda i: (i, 0, 0, 0))
    return pl.pallas_call(
        functools.partial(_gla_sample_kernel, sb),
        grid=(nb // sb,),
        in_specs=[spec(GLA_QK), spec(GLA_QK), spec(GLA_V), spec(GLA_V), spec(GLA_QK),
                  pl.BlockSpec((1, GLA_V), lambda i: (0, 0)), state_spec],
        out_specs=[spec(GLA_V), state_spec],
        out_shape=[jax.ShapeDtypeStruct((nb, pad, GLA_V), F32),
                   jax.ShapeDtypeStruct((nb, GLA_HEADS, GLA_DK, GLA_DV), F32)],
        compiler_params=_params("arbitrary"),
        name="gla_sample",
    )(q, k, v, r, la, gain, s0)


def _t5_bucket(dist):
    max_exact = NUM_BUCKETS // 2
    d_f = jnp.maximum(dist, 1).astype(F32)
    large = max_exact + (jnp.log(d_f / max_exact) / math.log(MAX_DISTANCE / max_exact)
                         * (NUM_BUCKETS - max_exact)).astype(jnp.int32)
    large = jnp.minimum(large, NUM_BUCKETS - 1)
    return jnp.where(dist < max_exact, dist, large)


def _group_bias(rel_bias):
    rows = []
    for g, (window, dil) in enumerate(DILATED_GROUPS):
        buckets = _t5_bucket(jnp.arange(window // dil + 1) * dil)
        rows.append(rel_bias[buckets][:, g * HEADS_PER_GROUP:(g + 1) * HEADS_PER_GROUP].T)
    return jnp.stack(rows)


def _prompt_bias_rows(gb):
    band = gb[:, :, ::-1]
    off = jnp.full(gb.shape[:2] + (2 * QUERY_BLOCK - KEYS_PER_QUERY,), NEG_INF, F32)
    return jnp.concatenate([band, off], axis=-1)[:, :, None, :]


def _sample_bias_tables(gb, steps, past_rows):
    tables, new_tables = [], []
    t_idx = np.arange(SUBLANES)[:, None]
    c_idx = np.arange(LANES)[None, :]
    for g, (window, dil) in enumerate(DILATED_GROUPS):
        p = past_rows[g]
        assert p == window == (KEYS_PER_QUERY - 1) * dil
        b = gb[g]
        heads = b.shape[0]
        row0 = b[:, :0:-1]
        if dil > 1:
            gaps = jnp.full((heads, KEYS_PER_QUERY - 1, dil - 1), NEG_INF, F32)
            row0 = jnp.concatenate([row0[:, :, None], gaps], axis=2).reshape(heads, p)
        rows = []
        for t in range(SUBLANES):
            if t < steps:
                rows.append(jnp.concatenate([jnp.full((heads, t), NEG_INF, F32), row0[:, :p - t]], axis=1))
            else:
                rows.append(jnp.full((heads, p), NEG_INF, F32))
        past = jnp.stack(rows, axis=1)
        new = jnp.full((heads, SUBLANES, LANES), NEG_INF, F32)
        for j in range((steps - 1) // dil + 1):
            mask = (t_idx - c_idx == j * dil) & (t_idx < steps) & (c_idx < steps)
            new = jnp.where(jnp.asarray(mask)[None], b[:, j][:, None, None], new)
        tables.append(past.reshape(heads // 2, 2 * SUBLANES, p))
        new_tables.append(new.reshape(heads // 2, 2 * SUBLANES, LANES))
    return tables, jnp.stack(new_tables)


def _ffn_weights(w_in, conv_w, conv_b, w_down):
    return (w_in.astype(BF16), w_down.astype(BF16), conv_w, conv_b.reshape(DEPTH, 1, D_FF))


def _conv_tail_prompt(cs):
    batch = cs.shape[0]
    tail = cs[:, :, SUBLANES - (CONV_WIDTH - 1):, :]
    return jnp.transpose(tail, (0, 2, 1, 3)).reshape(batch, CONV_WIDTH - 1, D_FF)


def _conv_past_sample(state):
    nb = state.shape[0]
    s = state.reshape(nb, CONV_WIDTH - 1, N_FF_CHUNKS, FF_CHUNK)
    return jnp.transpose(s, (2, 1, 0, 3)).reshape(N_FF_CHUNKS, (CONV_WIDTH - 1) * nb, FF_CHUNK)


def _conv_tail_sample(cs, nb):
    s = cs.reshape(N_FF_CHUNKS, CONV_WIDTH - 1, nb, FF_CHUNK)
    return jnp.transpose(s, (2, 1, 0, 3)).reshape(nb, CONV_WIDTH - 1, D_FF)


def kernel(x_prompt, x_sample, state_pool, cache_win_g1, cache_win_g2, cache_win_g3, state_gla, state_ffn_conv,
           c_prompt, c_sample, w_ada, b_ada, norm_gain, final_gain, rel_bias, pool_w, pool_scale,
           attn_w_in, attn_w_out, gla_w_in, gla_w_gate_up, gla_b_gate, gla_norm_gain, gla_w_out,
           ffn_w_in, ffn_conv_w, ffn_conv_b, ffn_w_down):
    batch, seq, _ = x_prompt.shape
    nb, steps, _ = x_sample.shape
    caches = (cache_win_g1, cache_win_g2, cache_win_g3)
    assert x_prompt.shape[-1] == D_MODEL and D_FF % FF_CHUNK == 0
    assert seq % PROMPT_ROW_TILE == 0 and seq % GLA_TIME_TILE == 0 and GLA_TIME_TILE % GLA_CHUNK == 0
    assert seq % (QUERY_BLOCK * max(d for _, d in DILATED_GROUPS)) == 0
    assert nb % SUBLANES == 0 and steps <= SUBLANES

    mod_p, mod_s = _modulation(c_prompt, c_sample, w_ada, b_ada)
    fgain = final_gain.reshape(1, D_MODEL)

    xp = x_prompt.reshape(batch * seq, D_MODEL)
    xs = jnp.transpose(x_sample, (1, 0, 2)).reshape(steps * nb, D_MODEL)

    pool_p, pool_s, gla_p, gla_s, conv_p, conv_s = [], [], [], [], [], []
    win_p, win_s = None, None
    ffn_stack = _ffn_weights(ffn_w_in, ffn_conv_w, ffn_conv_b, ffn_w_down)

    for i in range(DEPTH):
        kind, j = i % 3, i // 3
        last = i == DEPTH - 1
        ffn = (i, ffn_stack)
        conv_past = _conv_past_sample(state_ffn_conv[i])
        gains = norm_gain[i]
        if kind == 0:
            mix_w = (pool_w[j].astype(BF16), pool_scale[j].reshape(1, D_MODEL))
            xp, cs, pst = _layer_prompt("pool", last, xp, mod_p[i], gains, fgain, mix_w, ffn)
            pool_p.append(pst[:, POOL_CARRY_ROWS - POOL_STATE_ROWS:])
            past = jnp.transpose(state_pool[j], (1, 0, 2))
            xs, css, psts = _layer_sample("pool", last, xs, mod_s[i], gains, fgain, mix_w + (past,), ffn, conv_past)
            pool_s.append(jnp.transpose(psts, (1, 0, 2)))
        elif kind == 1:
            w = attn_w_in[j]
            w3 = jnp.stack([jnp.concatenate([w[:, s * ATTN_INNER + g * GROUP_WIDTH:
                                                s * ATTN_INNER + (g + 1) * GROUP_WIDTH] for s in range(3)], axis=1)
                            for g in range(N_GROUPS)]).astype(BF16)
            gb = _group_bias(rel_bias)
            wo = attn_w_out[j].astype(BF16)
            qkvp, *kv_t = _qkv_prompt(xp.reshape(batch, seq, D_MODEL), mod_p[i], gains, w3)
            o_all = _attn_prompt(qkvp, _prompt_bias_rows(gb))
            win_p = [jnp.transpose(t.reshape(batch, 2, HEADS_PER_GROUP, HEAD_DIM, t.shape[-1]), (0, 4, 1, 2, 3))[None]
                     for t in kv_t]
            xp, cs = _layer_prompt("proj", last, xp, mod_p[i], gains, fgain,
                                   (o_all.reshape(batch * seq, ATTN_INNER), wo), ffn)
            qkv_s = _proj_sample(xs, mod_s[i], gains, w.astype(BF16))
            q6 = qkv_s.reshape(steps, nb, 3, N_GROUPS, HEADS_PER_GROUP, HEAD_DIM)
            q_s = jnp.pad(jnp.transpose(q6[:, :, 0], (1, 2, 3, 0, 4)),
                          ((0, 0), (0, 0), (0, 0), (0, SUBLANES - steps), (0, 0)))
            q_s = q_s.reshape(nb, N_GROUPS, HEADS_PER_GROUP // 2, 2, SUBLANES, HEAD_DIM)
            zeros = jnp.zeros_like(q_s[:, :, :, 0])
            q_s = jnp.stack([jnp.concatenate([q_s[:, :, :, 0], zeros], axis=-1),
                             jnp.concatenate([zeros, q_s[:, :, :, 1]], axis=-1)], axis=3)
            q_s = q_s.reshape(nb, N_GROUPS, HEADS_PER_GROUP // 2, 2 * SUBLANES, LANES)
            kv_new = jnp.transpose(q6[:, :, 1:], (1, 3, 0, 2, 4, 5)).reshape(nb, N_GROUPS, steps, 2 * GROUP_WIDTH)
            kv_new = jnp.pad(kv_new, ((0, 0), (0, 0), (0, SUBLANES - steps), (0, 0)))
            past_rows = [c.shape[2] for c in caches]
            cache_t = [jnp.transpose(c[j], (0, 2, 3, 4, 1)).reshape(nb, 2 * HEADS_PER_GROUP, HEAD_DIM, p)
                       for c, p in zip(caches, past_rows)]
            bias_past, bias_new = _sample_bias_tables(gb, steps, past_rows)
            o8, *new_caches = _attn_sample(steps, q_s, kv_new, cache_t, bias_past, bias_new)
            a_s = jnp.transpose(o8[:, :steps], (1, 0, 2)).reshape(steps * nb, ATTN_INNER)
            win_s = [jnp.transpose(c.reshape(nb, 2, HEADS_PER_GROUP, HEAD_DIM, p), (0, 4, 1, 2, 3))[None]
                     for c, p in zip(new_caches, past_rows)]
            xs, css = _layer_sample("proj", last, xs, mod_s[i], gains, fgain, (a_s.astype(BF16), wo), ffn, conv_past)
        else:
            w = gla_w_in[j]
            n_main = 2 * GLA_QK + 2 * GLA_V
            weights = (w[:, :n_main].astype(BF16),
                       jnp.pad(w[:, n_main:], ((0, 0), (0, LANES - GATE_RANK))).astype(BF16),
                       jnp.pad(gla_w_gate_up[j], ((0, LANES - GATE_RANK), (0, 0))).astype(BF16),
                       gla_b_gate[j].reshape(1, GLA_QK))
            gain = gla_norm_gain[j].reshape(1, GLA_V)
            wo = gla_w_out[j].astype(BF16)
            q, k, v, r, la = _gla_proj(xp, mod_p[i], gains, weights, None, PROMPT_ROW_TILE)
            shp = lambda a: a.reshape(batch, seq, a.shape[-1])
            a_p, s_p = _gla_prompt(shp(q), shp(k), shp(v), shp(r), shp(la), gain)
            gla_p.append(s_p)
            xp, cs = _layer_prompt("proj", last, xp, mod_p[i], gains, fgain,
                                   (a_p.reshape(batch * seq, GLA_V), wo), ffn)
            outs = _gla_proj(xs, mod_s[i], gains, weights, nb, steps * nb)

            def per_seq(a):
                a = jnp.transpose(a.reshape(steps, nb, a.shape[-1]), (1, 0, 2))
                return jnp.pad(a, ((0, 0), (0, SAMPLE_DEC_PAD - steps), (0, 0)))

            qs, ks, vs, rs, las = (per_seq(a) for a in outs)
            a16, s_s = _gla_sample(qs, ks, vs, rs, las, gain, state_gla[j])
            gla_s.append(s_s)
            a_s = jnp.transpose(a16[:, :steps], (1, 0, 2)).reshape(steps * nb, GLA_V).astype(BF16)
            xs, css = _layer_sample("proj", last, xs, mod_s[i], gains, fgain, (a_s, wo), ffn, conv_past)
        conv_p.append(_conv_tail_prompt(cs))
        conv_s.append(_conv_tail_sample(css, nb))

    y_prompt = xp.reshape(batch, seq, D_MODEL)
    y_sample = jnp.transpose(xs.reshape(steps, nb, D_MODEL), (1, 0, 2))
    return (y_prompt, y_sample, jnp.stack(pool_p), jnp.stack(pool_s),
            win_p[0], win_s[0], win_p[1], win_s[1], win_p[2], win_s[2],
            jnp.stack(gla_p), jnp.stack(gla_s), jnp.stack(conv_p), jnp.stack(conv_s))
```
